```python
import jax, jax.numpy as jnp
from jax import lax
import numpy as np

D_MODEL = 1024
BATCH = 1
SEQ = 16384
DEPTH = 1

N_META = 16
EPS = 1e-6
ATT_HEADS = 8
Q_LORA = 384
KV_LORA = 256
QK_NOPE = 128
QK_ROPE = 64
V_HEAD = 128
ROPE_THETA = 10000.0
ATT_WIDTH = ATT_HEADS * V_HEAD
Q_BLOCK = 128
SSM_HEADS = 16
SSM_HEAD_DIM = 64
SSM_WIDTH = SSM_HEADS * SSM_HEAD_DIM
SSM_GROUPS = 2
SSM_HEADS_PER_GROUP = SSM_HEADS // SSM_GROUPS
SSM_STATE = 128
CONV_K = 4
CHUNK = 128
CONV_DIM = SSM_WIDTH + 2 * SSM_GROUPS * SSM_STATE
MIX_WIDTH = ATT_WIDTH + SSM_WIDTH
D_FF = 4 * D_MODEL
IN_SPLITS = (Q_LORA, KV_LORA, QK_ROPE, SSM_WIDTH, CONV_DIM, SSM_HEADS)
IN_WIDTH = sum(IN_SPLITS)
IN_OFFSETS = tuple(int(v) for v in np.cumsum(IN_SPLITS)[:-1])

kernel_name = "hymba_mla_ssd_sandwich_layer"


def rmsnorm(x, w):
    x32 = x.astype(jnp.float32)
    y = x32 * lax.rsqrt(jnp.mean(jnp.square(x32), axis=-1, keepdims=True) + EPS)
    return y.astype(x.dtype) * w


def gated_group_rmsnorm(y, z, w):
    g = (y * jax.nn.silu(z)).astype(jnp.float32)
    shp = g.shape
    g = g.reshape(*shp[:-1], SSM_GROUPS, shp[-1] // SSM_GROUPS)
    g = g * lax.rsqrt(jnp.mean(jnp.square(g), axis=-1, keepdims=True) + EPS)
    return g.reshape(shp).astype(z.dtype) * w


def rope_tables(length, dtype):
    inv_freq = ROPE_THETA ** (-jnp.arange(0, QK_ROPE, 2, dtype=jnp.float32) / QK_ROPE)
    ang = jnp.arange(length, dtype=jnp.float32)[:, None] * inv_freq[None, :]
    return jnp.cos(ang).astype(dtype), jnp.sin(ang).astype(dtype)


def apply_rope(x, cos, sin):
    x1, x2 = jnp.split(x, 2, axis=-1)
    return jnp.concatenate([x1 * cos - x2 * sin, x2 * cos + x1 * sin], axis=-1)


def mla_attention(q_nope, q_rope, k_nope, k_rope, v):
    bsz, length = q_nope.shape[:2]
    scale = (QK_NOPE + QK_ROPE) ** -0.5
    pos = jnp.arange(length)

    def attend(qn, qr, qpos, kn, kr, vv, kpos):
        s = jnp.einsum('bqhd,bkhd->bhqk', qn, kn) + jnp.einsum('bqhd,bkd->bhqk', qr, kr)
        s = s.astype(jnp.float32) * scale
        s = jnp.where(kpos[None, None, None, :] <= qpos[None, None, :, None], s, -jnp.inf)
        p = jax.nn.softmax(s, axis=-1).astype(vv.dtype)
        return jnp.einsum('bhqk,bkhd->bqhd', p, vv)

    out_meta = attend(q_nope[:, :N_META], q_rope[:, :N_META], pos[:N_META],
                      k_nope[:, :N_META], k_rope[:, :N_META], v[:, :N_META], pos[:N_META])
    n_blocks = (length - N_META) // Q_BLOCK

    def to_blocks(t):
        t = t[:, N_META:]
        return jnp.moveaxis(t.reshape(bsz, n_blocks, Q_BLOCK, *t.shape[2:]), 1, 0)

    def block_fn(args):
        qn, qr, qp = args
        return attend(qn, qr, qp, k_nope, k_rope, v, pos)

    out_real = lax.map(block_fn, (to_blocks(q_nope), to_blocks(q_rope), pos[N_META:].reshape(n_blocks, Q_BLOCK)))
    out_real = jnp.moveaxis(out_real, 0, 1).reshape(bsz, length - N_META, ATT_HEADS, V_HEAD)
    return jnp.concatenate([out_meta, out_real], axis=1)


def causal_depthwise_conv(u, w, bias):
    out = lax.conv_general_dilated(u, w[:, None, :], window_strides=(1,), padding=((CONV_K - 1, 0),),
                                   dimension_numbers=('NWC', 'WIO', 'NWC'), feature_group_count=u.shape[-1])
    return out + bias


def ssd_chunked(x, dt, a, b_mat, c_mat, state0, chunk):
    bsz, total = x.shape[:2]
    nc = total // chunk

    def chunks(t):
        return t.reshape(bsz, nc, chunk, *t.shape[2:])

    x, dt, b_mat, c_mat = chunks(x), chunks(dt), chunks(b_mat), chunks(c_mat)
    a_cum = jnp.cumsum(dt * a, axis=2)
    seg = a_cum[:, :, :, None] - a_cum[:, :, None, :]
    causal = jnp.tril(jnp.ones((chunk, chunk), dtype=bool))[:, :, None, None]
    decay = jnp.where(causal, jnp.exp(jnp.where(causal, seg, 0.0)), 0.0)
    cb = jnp.einsum('bclgn,bcsgn->bclsg', c_mat, b_mat)
    w_ls = cb[..., None] * decay * dt[:, :, None]
    y_diag = jnp.einsum('bclsgr,bcsgrp->bclgrp', w_ls, x)
    decay_to_end = jnp.exp(a_cum[:, :, -1:] - a_cum)
    states = jnp.einsum('bcsgn,bcsgr,bcsgrp->bcgrpn', b_mat, decay_to_end * dt, x)
    chunk_decay = jnp.exp(a_cum[:, :, -1])

    def step(h, inp):
        dec, st = inp
        return dec[..., None, None] * h + st, h

    final, prev = lax.scan(step, state0, (jnp.moveaxis(chunk_decay, 1, 0), jnp.moveaxis(states, 1, 0)))
    prev = jnp.moveaxis(prev, 0, 1)
    y_off = jnp.einsum('bclgn,bcgrpn,bclgr->bclgrp', c_mat, prev, jnp.exp(a_cum))
    y = (y_diag + y_off).reshape(bsz, total, *x.shape[3:])
    return y, final


def hybrid_mixer(h, w_in, q_a_norm, w_q_up, kv_a_norm, w_kv_up, conv_w, conv_b,
                 dt_bias, a_log, d_skip, ssm_norm, w_out, cos, sin):
    bsz, length, _ = h.shape
    proj = h @ w_in
    c_q, c_kv, k_rope, z, xbc, dt_raw = jnp.split(proj, IN_OFFSETS, axis=-1)

    q = (rmsnorm(c_q, q_a_norm) @ w_q_up).reshape(bsz, length, ATT_HEADS, QK_NOPE + QK_ROPE)
    q_nope, q_rope = q[..., :QK_NOPE], q[..., QK_NOPE:]
    kv = (rmsnorm(c_kv, kv_a_norm) @ w_kv_up).reshape(bsz, length, ATT_HEADS, QK_NOPE + V_HEAD)
    k_nope, v = kv[..., :QK_NOPE], kv[..., QK_NOPE:]
    q_rope = apply_rope(q_rope, cos[None, :, None, :], sin[None, :, None, :])
    k_rope = apply_rope(k_rope, cos[None], sin[None])
    att = mla_attention(q_nope, q_rope, k_nope, k_rope, v).reshape(bsz, length, ATT_WIDTH)

    xbc = jax.nn.silu(causal_depthwise_conv(xbc, conv_w, conv_b))
    xs, bm, cm = jnp.split(xbc, (SSM_WIDTH, SSM_WIDTH + SSM_GROUPS * SSM_STATE), axis=-1)
    xs32 = xs.astype(jnp.float32).reshape(bsz, length, SSM_GROUPS, SSM_HEADS_PER_GROUP, SSM_HEAD_DIM)
    bm = bm.astype(jnp.float32).reshape(bsz, length, SSM_GROUPS, SSM_STATE)
    cm = cm.astype(jnp.float32).reshape(bsz, length, SSM_GROUPS, SSM_STATE)
    dt = jax.nn.softplus(dt_raw.astype(jnp.float32) + dt_bias.astype(jnp.float32))
    dt = dt.reshape(bsz, length, SSM_GROUPS, SSM_HEADS_PER_GROUP)
    a = -jnp.exp(a_log.astype(jnp.float32)).reshape(SSM_GROUPS, SSM_HEADS_PER_GROUP)
    state0 = jnp.zeros((bsz, SSM_GROUPS, SSM_HEADS_PER_GROUP, SSM_HEAD_DIM, SSM_STATE), jnp.float32)
    y_meta, st = ssd_chunked(xs32[:, :N_META], dt[:, :N_META], a, bm[:, :N_META], cm[:, :N_META], state0, N_META)
    y_real, _ = ssd_chunked(xs32[:, N_META:], dt[:, N_META:], a, bm[:, N_META:], cm[:, N_META:], st, CHUNK)
    y = jnp.concatenate([y_meta, y_real], axis=1)
    y = y + d_skip.astype(jnp.float32).reshape(SSM_GROUPS, SSM_HEADS_PER_GROUP)[:, :, None] * xs32
    y = y.reshape(bsz, length, SSM_WIDTH).astype(h.dtype)
    ssm = gated_group_rmsnorm(y, z, ssm_norm)

    return jnp.concatenate([att, ssm], axis=-1) @ w_out


def setup_inputs(seed: int = 0) -> dict:
    key = jax.random.key(seed)
    ks = jax.random.split(key, 24)
    f32 = jnp.float32

    def nrm(k, shape, fan_in):
        return jax.random.normal(k, shape, f32) * (fan_in ** -0.5)

    def gain(k, shape):
        return 1.0 + 0.05 * jax.random.normal(k, shape, f32)

    dt0 = jnp.exp(jax.random.uniform(ks[10], (DEPTH, SSM_HEADS), f32) * (jnp.log(0.1) - jnp.log(0.001)) + jnp.log(0.001))
    return {
        "x": jax.random.normal(ks[0], (BATCH, SEQ, D_MODEL), f32),
        "meta_tokens": jax.random.normal(ks[1], (N_META, D_MODEL), f32),
        "norm_mix_pre": gain(ks[2], (DEPTH, D_MODEL)),
        "w_in": nrm(ks[3], (DEPTH, D_MODEL, IN_WIDTH), D_MODEL),
        "q_a_norm": gain(ks[4], (DEPTH, Q_LORA)),
        "w_q_up": nrm(ks[5], (DEPTH, Q_LORA, ATT_HEADS * (QK_NOPE + QK_ROPE)), Q_LORA),
        "kv_a_norm": gain(ks[6], (DEPTH, KV_LORA)),
        "w_kv_up": nrm(ks[7], (DEPTH, KV_LORA, ATT_HEADS * (QK_NOPE + V_HEAD)), KV_LORA),
        "conv_w": nrm(ks[8], (DEPTH, CONV_K, CONV_DIM), CONV_K),
        "conv_b": 0.01 * jax.random.normal(ks[9], (DEPTH, CONV_DIM), f32),
        "dt_bias": dt0 + jnp.log(-jnp.expm1(-dt0)),
        "a_log": jnp.log(jax.random.uniform(ks[11], (DEPTH, SSM_HEADS), f32, 1.0, 16.0)),
        "d_skip": gain(ks[12], (DEPTH, SSM_HEADS)),
        "ssm_norm": gain(ks[13], (DEPTH, SSM_WIDTH)),
        "w_out": nrm(ks[14], (DEPTH, MIX_WIDTH, D_MODEL), MIX_WIDTH),
        "norm_mix_post": gain(ks[15], (DEPTH, D_MODEL)),
        "norm_mlp_pre": gain(ks[16], (DEPTH, D_MODEL)),
        "w_mlp_up": nrm(ks[17], (DEPTH, D_MODEL, D_FF), D_MODEL),
        "w_mlp_down": nrm(ks[18], (DEPTH, D_FF, D_MODEL), D_FF),
        "norm_mlp_post": gain(ks[19], (DEPTH, D_MODEL)),
    }


def reference(x, meta_tokens, norm_mix_pre, w_in, q_a_norm, w_q_up, kv_a_norm, w_kv_up, conv_w, conv_b,
              dt_bias, a_log, d_skip, ssm_norm, w_out, norm_mix_post, norm_mlp_pre, w_mlp_up, w_mlp_down,
              norm_mlp_post):
    bsz = x.shape[0]
    h = jnp.concatenate([jnp.broadcast_to(meta_tokens[None].astype(x.dtype), (bsz, N_META, D_MODEL)), x], axis=1)
    cos, sin = rope_tables(h.shape[1], h.dtype)
    for layer in range(DEPTH):
        mix = hybrid_mixer(rmsnorm(h, norm_mix_pre[layer]), w_in[layer], q_a_norm[layer], w_q_up[layer],
                           kv_a_norm[layer], w_kv_up[layer], conv_w[layer], conv_b[layer], dt_bias[layer],
                           a_log[layer], d_skip[layer], ssm_norm[layer], w_out[layer], cos, sin)
        h = h + rmsnorm(mix, norm_mix_post[layer])
        f = jnp.square(jax.nn.relu(rmsnorm(h, norm_mlp_pre[layer]) @ w_mlp_up[layer])) @ w_mlp_down[layer]
        h = h + rmsnorm(f, norm_mlp_post[layer])
    return h[:, N_META:]
```

```python
import functools
import math

import jax
import jax.numpy as jnp
import numpy as np
from jax import lax
from jax.experimental import pallas as pl
from jax.experimental.pallas import tpu as pltpu

D_MODEL = 1024
N_META = 16
EPS = 1e-6
ATT_HEADS = 8
Q_LORA = 384
KV_LORA = 256
QK_NOPE = 128
QK_ROPE = 64
V_HEAD = 128
QK_DIM = QK_NOPE + QK_ROPE
ROPE_THETA = 10000.0
ATT_WIDTH = ATT_HEADS * V_HEAD
SSM_HEADS = 16
SSM_HEAD_DIM = 64
SSM_WIDTH = SSM_HEADS * SSM_HEAD_DIM
SSM_GROUPS = 2
SSM_STATE = 128
GROUP_WIDTH = SSM_WIDTH // SSM_GROUPS
CONV_K = 4
CONV_DIM = SSM_WIDTH + 2 * SSM_GROUPS * SSM_STATE
D_FF = 4 * D_MODEL

LANES = 128
SUBLANES = 8
VMEM_LIMIT_BYTES = 56 * 1024 * 1024

PAD = 256
ROW_BLOCK = 256
ATT_BLOCK = 512
SSD_CHUNK = 128
FF_CHUNK = 1024
ATT_SCALE_LOG2 = (QK_DIM ** -0.5) * math.log2(math.e)

BF16 = jnp.bfloat16
F32 = jnp.float32


def _const_spec(shape):
    nd = len(shape)
    return pl.BlockSpec(shape, lambda *_: (0,) * nd)


def _rms(x, gain):
    return x * lax.rsqrt(jnp.mean(jnp.square(x), axis=-1, keepdims=True) + EPS) * gain


def _proj_kernel(meta_ref, x_ref, cs_ref, g_pre_ref, wqa_ref, wkva_ref, wz_ref, wxbc_ref, wsm_ref,
                 gq_ref, gkv_ref, wq_ref, wkv_ref,
                 q_ref, k_ref, v_ref, z_ref, xbc_ref, dt_ref):
    i = pl.program_id(0)
    xin = jnp.where(i == 0, meta_ref[...], x_ref[...])
    hn = _rms(xin, g_pre_ref[...]).astype(BF16)

    z_ref[...] = jnp.dot(hn, wz_ref[...], preferred_element_type=F32)
    xbc_ref[...] = jnp.dot(hn, wxbc_ref[...], preferred_element_type=F32)
    small = jnp.dot(hn, wsm_ref[...], preferred_element_type=F32)
    dt_ref[...] = small[:, 2 * LANES:3 * LANES]

    cs = cs_ref[...]
    cos = cs[:, :QK_ROPE]
    sin = cs[:, QK_ROPE:]
    k_rope = (small[:, :QK_ROPE] * cos + small[:, LANES:LANES + QK_ROPE] * sin).astype(BF16)

    c_q = jnp.dot(hn, wqa_ref[...], preferred_element_type=F32)
    cqn = _rms(c_q, gq_ref[...]).astype(BF16)
    qall = jnp.dot(cqn, wq_ref[...], preferred_element_type=F32)
    c_kv = jnp.dot(hn, wkva_ref[...], preferred_element_type=F32)
    ckvn = _rms(c_kv, gkv_ref[...]).astype(BF16)
    kvall = jnp.dot(ckvn, wkv_ref[...], preferred_element_type=F32)

    for h in range(ATT_HEADS):
        base = h * 2 * LANES
        q_ref[h, :, :QK_NOPE] = (qall[:, base:base + QK_NOPE] * ATT_SCALE_LOG2).astype(BF16)
        rope = qall[:, base + LANES:base + LANES + QK_ROPE]
        rot = qall[:, base + LANES + QK_ROPE:base + 2 * LANES]
        q_ref[h, :, QK_NOPE:] = ((rope * cos + rot * sin) * ATT_SCALE_LOG2).astype(BF16)
        k_ref[h, :, :QK_NOPE] = kvall[:, base:base + QK_NOPE].astype(BF16)
        k_ref[h, :, QK_NOPE:] = k_rope
        v_ref[h] = kvall[:, base + LANES:base + 2 * LANES].astype(BF16)


def _input_projection(x2d, meta_pad, cs, g_pre, wqa, wkva, wz, wxbc, wsm, gq, gkv, wq, wkv):
    seq = x2d.shape[0]
    n_blocks = seq // ROW_BLOCK + 1
    rows_p = seq + PAD
    real = lambda i: (jnp.maximum(i - 1, 0), 0)
    real3 = lambda i: (0, jnp.maximum(i - 1, 0), 0)
    padded = lambda i: (i, 0)
    padded3 = lambda i: (0, i, 0)
    weights = (g_pre, wqa, wkva, wz, wxbc, wsm, gq, gkv, wq, wkv)
    return pl.pallas_call(
        _proj_kernel,
        grid=(n_blocks,),
        in_specs=[_const_spec(meta_pad.shape),
                  pl.BlockSpec((ROW_BLOCK, D_MODEL), real),
                  pl.BlockSpec((ROW_BLOCK, LANES), padded)] + [_const_spec(w.shape) for w in weights],
        out_specs=[pl.BlockSpec((ATT_HEADS, ROW_BLOCK, QK_DIM), real3),
                   pl.BlockSpec((ATT_HEADS, ROW_BLOCK, QK_DIM), padded3),
                   pl.BlockSpec((ATT_HEADS, ROW_BLOCK, V_HEAD), padded3),
                   pl.BlockSpec((ROW_BLOCK, SSM_WIDTH), real),
                   pl.BlockSpec((ROW_BLOCK, CONV_DIM), padded),
                   pl.BlockSpec((ROW_BLOCK, LANES), padded)],
        out_shape=[jax.ShapeDtypeStruct((ATT_HEADS, seq, QK_DIM), BF16),
                   jax.ShapeDtypeStruct((ATT_HEADS, rows_p, QK_DIM), BF16),
                   jax.ShapeDtypeStruct((ATT_HEADS, rows_p, V_HEAD), BF16),
                   jax.ShapeDtypeStruct((seq, SSM_WIDTH), F32),
                   jax.ShapeDtypeStruct((rows_p, CONV_DIM), F32),
                   jax.ShapeDtypeStruct((rows_p, LANES), F32)],
        compiler_params=pltpu.CompilerParams(dimension_semantics=("arbitrary",),
                                             vmem_limit_bytes=VMEM_LIMIT_BYTES),
        name="input_projection",
    )(meta_pad, x2d, cs, *weights)


def _dot_nt(a, b):
    return lax.dot_general(a, b, (((1,), (1,)), ((), ())), preferred_element_type=F32)


def _attn_kernel(q_ref, k_ref, v_ref, o_ref):
    i = pl.program_id(1)
    q = q_ref[0]

    def update(carry, kc, vc, mask):
        m, l, acc = carry
        s = _dot_nt(q, kc)
        if mask is not None:
            s = jnp.where(mask, s, -jnp.inf)
        m_new = jnp.maximum(m, jnp.max(s, axis=1, keepdims=True))
        alpha = jnp.exp2(m - m_new)
        p = jnp.exp2(s - m_new)
        l = alpha * l + jnp.sum(p, axis=1, keepdims=True)
        acc = alpha * acc + jnp.dot(p.astype(BF16), vc, preferred_element_type=F32)
        return m_new, l, acc

    km = k_ref[0, PAD - N_META:PAD, :]
    vm = v_ref[0, PAD - N_META:PAD, :]
    s0 = _dot_nt(q, km)
    m0 = jnp.max(s0, axis=1, keepdims=True)
    p0 = jnp.exp2(s0 - m0)
    carry = (m0, jnp.sum(p0, axis=1, keepdims=True),
             jnp.dot(p0.astype(BF16), vm, preferred_element_type=F32))

    def body(j, carry):
        start = pl.multiple_of(PAD + j * ATT_BLOCK, ATT_BLOCK // 2)
        return update(carry, k_ref[0, pl.ds(start, ATT_BLOCK), :], v_ref[0, pl.ds(start, ATT_BLOCK), :], None)

    carry = lax.fori_loop(0, i, body, carry)

    start = pl.multiple_of(PAD + i * ATT_BLOCK, ATT_BLOCK // 2)
    rows = lax.broadcasted_iota(jnp.int32, (ATT_BLOCK, ATT_BLOCK), 0)
    cols = lax.broadcasted_iota(jnp.int32, (ATT_BLOCK, ATT_BLOCK), 1)
    m, l, acc = update(carry, k_ref[0, pl.ds(start, ATT_BLOCK), :], v_ref[0, pl.ds(start, ATT_BLOCK), :],
                       cols <= rows)
    o_ref[...] = (acc / l).astype(o_ref.dtype)


def _attention(q, k, v):
    heads, seq, _ = q.shape
    rows_p = k.shape[1]
    return pl.pallas_call(
        _attn_kernel,
        grid=(heads, seq // ATT_BLOCK),
        in_specs=[pl.BlockSpec((1, ATT_BLOCK, QK_DIM), lambda h, i: (h, i, 0)),
                  pl.BlockSpec((1, rows_p, QK_DIM), lambda h, i: (h, 0, 0)),
                  pl.BlockSpec((1, rows_p, V_HEAD), lambda h, i: (h, 0, 0))],
        out_specs=pl.BlockSpec((ATT_BLOCK, V_HEAD), lambda h, i: (i, h)),
        out_shape=jax.ShapeDtypeStruct((seq, ATT_WIDTH), BF16),
        compiler_params=pltpu.CompilerParams(dimension_semantics=("arbitrary", "arbitrary"),
                                             vmem_limit_bytes=VMEM_LIMIT_BYTES),
        name="causal_attention",
    )(q, k, v)


def _split_dot(v, mat, terms):
    out = None
    rem = v
    for t in range(terms):
        piece = rem.astype(BF16)
        part = jnp.dot(piece, mat, preferred_element_type=F32)
        out = part if out is None else out + part
        if t + 1 < terms:
            rem = rem - piece.astype(F32)
    return out


def _ssd_kernel(xbc_ref, dt_ref, convw_ref, convb_ref, dtb_ref, alog_ref, dskip_ref, tril_ref, expand_ref,
                y_ref, stage_ref, state_ref):
    i = pl.program_id(0)
    rows_b = xbc_ref.shape[0]

    @pl.when(i == 0)
    def _():
        stage_ref[0:SUBLANES, :] = jnp.zeros((SUBLANES, CONV_DIM), F32)
        state_ref[...] = jnp.zeros_like(state_ref)

    stage_ref[SUBLANES:, :] = xbc_ref[...]
    conv = convb_ref[...]
    for kk in range(CONV_K):
        off = SUBLANES - (CONV_K - 1) + kk
        conv = conv + convw_ref[kk:kk + 1, :] * stage_ref[off:off + rows_b, :]
    tail = stage_ref[rows_b:rows_b + SUBLANES, :]
    stage_ref[0:SUBLANES, :] = tail
    xc = conv * jax.nn.sigmoid(conv)

    lane = lax.broadcasted_iota(jnp.int32, (SSD_CHUNK, LANES), 1)
    row = lax.broadcasted_iota(jnp.int32, (SSD_CHUNK, LANES), 0)
    causal = lax.broadcasted_iota(jnp.int32, (SSD_CHUNK, SSD_CHUNK), 0) >= lax.broadcasted_iota(
        jnp.int32, (SSD_CHUNK, SSD_CHUNK), 1)
    low_half = lane < SSM_HEAD_DIM
    a_row = -jnp.exp(alog_ref[...])
    tril = tril_ref[...]
    expand = expand_ref[...]
    heads_per_group = SSM_HEADS // SSM_GROUPS

    for c in range(rows_b // SSD_CHUNK):
        r0 = c * SSD_CHUNK
        xs = xc[r0:r0 + SSD_CHUNK, :SSM_WIDTH]
        valid = (lane < SSM_HEADS) & (row + (i * rows_b + r0) >= PAD - N_META)
        dt = jnp.where(valid, jax.nn.softplus(dt_ref[r0:r0 + SSD_CHUNK, :] + dtb_ref[...]), 0.0)
        da = dt * a_row
        acum = _cumsum_rows(tril, da)
        acum_t = acum.T
        dt_t = dt.T
        a_last = acum[SSD_CHUNK - 1:SSD_CHUNK, :]
        e_acum = _split_dot(jnp.exp(acum), expand, 2)
        w_end = _split_dot(jnp.exp(a_last - acum) * dt, expand, 2)
        c_decay = _split_dot(jnp.broadcast_to(jnp.exp(a_last), (SUBLANES, LANES)), expand, 2)[0:1, :]

        y_diag = []
        y_off = []
        for g in range(SSM_GROUPS):
            b_g = xc[r0:r0 + SSD_CHUNK, SSM_WIDTH + g * SSM_STATE:SSM_WIDTH + (g + 1) * SSM_STATE]
            c_g = xc[r0:r0 + SSD_CHUNK, SSM_WIDTH + (SSM_GROUPS + g) * SSM_STATE:
                     SSM_WIDTH + (SSM_GROUPS + g + 1) * SSM_STATE].astype(BF16)
            b_t = b_g.T.astype(BF16)
            cb = jnp.dot(c_g, b_t, preferred_element_type=F32)
            gcols = slice(g * GROUP_WIDTH, (g + 1) * GROUP_WIDTH)
            for jp in range(heads_per_group // 2):
                ws = []
                for r in (g * heads_per_group + 2 * jp, g * heads_per_group + 2 * jp + 1):
                    seg = acum[:, r:r + 1] - acum_t[r:r + 1, :]
                    decay = jnp.where(causal, jnp.exp(seg), 0.0)
                    ws.append((cb * decay * dt_t[r:r + 1, :]).astype(BF16))
                col = g * GROUP_WIDTH + jp * LANES
                x_pair = xs[:, col:col + LANES]
                x_cat = jnp.concatenate([jnp.where(low_half, x_pair, 0.0), jnp.where(low_half, 0.0, x_pair)],
                                        axis=0).astype(BF16)
                y_diag.append(jnp.dot(jnp.concatenate(ws, axis=1), x_cat, preferred_element_type=F32))
            h_prev = state_ref[g]
            y_off.append(jnp.dot(c_g, h_prev.astype(BF16), preferred_element_type=F32))
            x_end = (xs[:, gcols] * w_end[:, gcols]).astype(BF16)
            state_ref[g] = c_decay[:, gcols] * h_prev + jnp.dot(b_t, x_end, preferred_element_type=F32)
        y = (jnp.concatenate(y_diag, axis=1) + jnp.concatenate(y_off, axis=1) * e_acum
             + xs * dskip_ref[...])
        y_ref[r0:r0 + SSD_CHUNK, :] = y


def _cumsum_rows(tril, v):
    out = None
    rem = v
    for t in range(3):
        piece = rem.astype(BF16)
        part = jnp.dot(tril, piece, preferred_element_type=F32)
        out = part if out is None else out + part
        if t < 2:
            rem = rem - piece.astype(F32)
    return out


def _ssd_scan(xbc, dt_raw, conv_w, conv_b, dt_bias, a_log, d_skip_x, tril, expand, seq):
    n_blocks = seq // ROW_BLOCK + 1
    consts = (conv_w, conv_b, dt_bias, a_log, d_skip_x, tril, expand)
    return pl.pallas_call(
        _ssd_kernel,
        grid=(n_blocks,),
        in_specs=[pl.BlockSpec((ROW_BLOCK, CONV_DIM), lambda i: (i, 0)),
                  pl.BlockSpec((ROW_BLOCK, LANES), lambda i: (i, 0))] + [_const_spec(c.shape) for c in consts],
        out_specs=pl.BlockSpec((ROW_BLOCK, SSM_WIDTH), lambda i: (jnp.maximum(i - 1, 0), 0)),
        out_shape=jax.ShapeDtypeStruct((seq, SSM_WIDTH), F32),
        scratch_shapes=[pltpu.VMEM((SUBLANES + ROW_BLOCK, CONV_DIM), F32),
                        pltpu.VMEM((SSM_GROUPS, SSM_STATE, GROUP_WIDTH), F32)],
        compiler_params=pltpu.CompilerParams(dimension_semantics=("arbitrary",),
                                             vmem_limit_bytes=VMEM_LIMIT_BYTES),
        name="ssd_scan",
    )(xbc, dt_raw, *consts)


def _out_mlp_kernel(att_ref, y_ref, z_ref, x_ref, g_ssm_ref, wout_ref, g_post_ref, g_mpre_ref,
                    wup_ref, wdown_ref, g_mpost_ref, o_ref):
    z = z_ref[...]
    gated = y_ref[...] * (z * jax.nn.sigmoid(z))
    parts = []
    for g in range(SSM_GROUPS):
        gg = gated[:, g * GROUP_WIDTH:(g + 1) * GROUP_WIDTH]
        parts.append(gg * lax.rsqrt(jnp.mean(jnp.square(gg), axis=-1, keepdims=True) + EPS))
    ssm = (jnp.concatenate(parts, axis=1) * g_ssm_ref[...]).astype(BF16)
    mix = (jnp.dot(att_ref[...], wout_ref[:ATT_WIDTH, :], preferred_element_type=F32)
           + jnp.dot(ssm, wout_ref[ATT_WIDTH:, :], preferred_element_type=F32))
    h1 = x_ref[...] + _rms(mix, g_post_ref[...])
    hn = _rms(h1, g_mpre_ref[...]).astype(BF16)
    f = None
    for c in range(D_FF // FF_CHUNK):
        u = jnp.dot(hn, wup_ref[:, c * FF_CHUNK:(c + 1) * FF_CHUNK], preferred_element_type=F32)
        u = jnp.square(jnp.maximum(u, 0.0)).astype(BF16)
        part = jnp.dot(u, wdown_ref[c * FF_CHUNK:(c + 1) * FF_CHUNK, :], preferred_element_type=F32)
        f = part if f is None else f + part
    o_ref[...] = h1 + _rms(f, g_mpost_ref[...])


def _out_mlp(att, y, z, x2d, g_ssm, wout, g_post, g_mpre, wup, wdown, g_mpost):
    seq = x2d.shape[0]
    weights = (g_ssm, wout, g_post, g_mpre, wup, wdown, g_mpost)
    row = lambda i: (i, 0)
    return pl.pallas_call(
        _out_mlp_kernel,
        grid=(seq // ROW_BLOCK,),
        in_specs=[pl.BlockSpec((ROW_BLOCK, ATT_WIDTH), row),
                  pl.BlockSpec((ROW_BLOCK, SSM_WIDTH), row),
                  pl.BlockSpec((ROW_BLOCK, SSM_WIDTH), row),
                  pl.BlockSpec((ROW_BLOCK, D_MODEL), row)] + [_const_spec(w.shape) for w in weights],
        out_specs=pl.BlockSpec((ROW_BLOCK, D_MODEL), row),
        out_shape=jax.ShapeDtypeStruct((seq, D_MODEL), F32),
        compiler_params=pltpu.CompilerParams(dimension_semantics=("arbitrary",),
                                             vmem_limit_bytes=VMEM_LIMIT_BYTES),
        name="out_proj_mlp",
    )(att, y, z, x2d, *weights)


def _rot_cols(w):
    half = QK_ROPE // 2
    return jnp.concatenate([-w[..., half:], w[..., :half]], axis=-1)


def _lane_pad(v, width=LANES):
    return jnp.pad(v, [(0, 0)] * (v.ndim - 1) + [(0, width - v.shape[-1])])


def kernel(x, meta_tokens, norm_mix_pre, w_in, q_a_norm, w_q_up, kv_a_norm, w_kv_up, conv_w, conv_b, dt_bias,
           a_log, d_skip, ssm_norm, w_out, norm_mix_post, norm_mlp_pre, w_mlp_up, w_mlp_down, norm_mlp_post):
    bsz, seq, _ = x.shape
    assert bsz == 1 and norm_mix_pre.shape[0] == 1, "single sequence, single layer"
    assert seq % ATT_BLOCK == 0 and seq % ROW_BLOCK == 0
    x2d = x[0]
    row = lambda v: v.reshape(1, -1)

    o_ckv, o_kr, o_z, o_xbc, o_dt = Q_LORA, Q_LORA + KV_LORA, Q_LORA + KV_LORA + QK_ROPE, \
        Q_LORA + KV_LORA + QK_ROPE + SSM_WIDTH, Q_LORA + KV_LORA + QK_ROPE + SSM_WIDTH + CONV_DIM
    w_in0 = w_in[0]
    w_kr = w_in0[:, o_kr:o_z]
    wsm = jnp.concatenate([_lane_pad(w_kr), _lane_pad(_rot_cols(w_kr)), _lane_pad(w_in0[:, o_dt:])], axis=1)
    wq3 = w_q_up[0].reshape(Q_LORA, ATT_HEADS, QK_DIM)
    wq = jnp.concatenate([wq3, _rot_cols(wq3[..., QK_NOPE:])], axis=-1).reshape(Q_LORA, ATT_HEADS * 2 * LANES)

    pos = jnp.maximum(jnp.arange(seq + PAD) - (PAD - N_META), 0).astype(F32)
    inv_freq = ROPE_THETA ** (-jnp.arange(0, QK_ROPE, 2, dtype=F32) / QK_ROPE)
    ang = pos[:, None] * inv_freq[None, :]
    cos, sin = jnp.cos(ang), jnp.sin(ang)
    cs = jnp.concatenate([cos, cos, sin, sin], axis=1)

    meta_pad = jnp.concatenate([jnp.zeros((PAD - N_META, D_MODEL), F32), meta_tokens.astype(F32)], axis=0)

    q, k, v, z, xbc, dt_raw = _input_projection(
        x2d, meta_pad, cs, row(norm_mix_pre[0]),
        w_in0[:, :o_ckv].astype(BF16), w_in0[:, o_ckv:o_kr].astype(BF16), w_in0[:, o_z:o_xbc].astype(BF16),
        w_in0[:, o_xbc:o_dt].astype(BF16), wsm.astype(BF16),
        row(q_a_norm[0]), row(kv_a_norm[0]), wq.astype(BF16), w_kv_up[0].astype(BF16))

    att = _attention(q, k, v)

    tril = jnp.tril(jnp.ones((SSD_CHUNK, SSD_CHUNK), BF16))
    expand = (jnp.arange(LANES)[:, None] == jnp.arange(SSM_WIDTH)[None, :] // SSM_HEAD_DIM).astype(BF16)
    y = _ssd_scan(xbc, dt_raw, conv_w[0], row(conv_b[0]), _lane_pad(row(dt_bias[0])), _lane_pad(row(a_log[0])),
                  row(jnp.repeat(d_skip[0], SSM_HEAD_DIM)), tril, expand, seq)

    out = _out_mlp(att, y, z, x2d, row(ssm_norm[0]), w_out[0].astype(BF16), row(norm_mix_post[0]),
                   row(norm_mlp_pre[0]), w_mlp_up[0].astype(BF16), w_mlp_down[0].astype(BF16),
                   row(norm_mlp_post[0]))
    return out[None]
```

```python
import functools
import math

import jax
import jax.numpy as jnp
import numpy as np
from jax import lax
from jax.experimental import pallas as pl
from jax.experimental.pallas import tpu as pltpu

D_MODEL = 1024
N_META = 16
EPS = 1e-6
ATT_HEADS = 8
Q_LORA = 384
KV_LORA = 256
QK_NOPE = 128
QK_ROPE = 64
V_HEAD = 128
QK_DIM = QK_NOPE + QK_ROPE
ROPE_THETA = 10000.0
ATT_WIDTH = ATT_HEADS * V_HEAD
SSM_HEADS = 16
SSM_HEAD_DIM = 64
SSM_WIDTH = SSM_HEADS * SSM_HEAD_DIM
SSM_GROUPS = 2
SSM_STATE = 128
GROUP_WIDTH = SSM_WIDTH // SSM_GROUPS
CONV_K = 4
CONV_DIM = SSM_WIDTH + 2 * SSM_GROUPS * SSM_STATE
D_FF = 4 * D_MODEL

LANES = 128
SUBLANES = 8
VMEM_LIMIT_BYTES = 56 * 1024 * 1024

PAD = 256
ROW_BLOCK = 256
ATT_BLOCK = 512
SSD_CHUNK = 128
FF_CHUNK = 1024
ATT_SCALE_LOG2 = (QK_DIM ** -0.5) * math.log2(math.e)

BF16 = jnp.bfloat16
F32 = jnp.float32


def _const_spec(shape):
    nd = len(shape)
    return pl.BlockSpec(shape, lambda *_: (0,) * nd)


def _rms(x, gain):
    return x * lax.rsqrt(jnp.mean(jnp.square(x), axis=-1, keepdims=True) + EPS) * gain


def _proj_kernel(meta_ref, x_ref, cs_ref, g_pre_ref, wqa_ref, wkva_ref, wz_ref, wxbc_ref, wsm_ref,
                 gq_ref, gkv_ref, wq_ref, wkv_ref,
                 q_ref, k_ref, v_ref, z_ref, xbc_ref, dt_ref):
    i = pl.program_id(0)
    xin = jnp.where(i == 0, meta_ref[...], x_ref[...])
    hn = _rms(xin, g_pre_ref[...]).astype(BF16)

    z_ref[...] = jnp.dot(hn, wz_ref[...], preferred_element_type=F32)
    xbc_ref[...] = jnp.dot(hn, wxbc_ref[...], preferred_element_type=F32)
    small = jnp.dot(hn, wsm_ref[...], preferred_element_type=F32)
    dt_ref[...] = small[:, 2 * LANES:3 * LANES]

    cs = cs_ref[...]
    cos = cs[:, :QK_ROPE]
    sin = cs[:, QK_ROPE:]
    k_rope = (small[:, :QK_ROPE] * cos + small[:, LANES:LANES + QK_ROPE] * sin).astype(BF16)

    c_q = jnp.dot(hn, wqa_ref[...], preferred_element_type=F32)
    cqn = _rms(c_q, gq_ref[...]).astype(BF16)
    qall = jnp.dot(cqn, wq_ref[...], preferred_element_type=F32)
    c_kv = jnp.dot(hn, wkva_ref[...], preferred_element_type=F32)
    ckvn = _rms(c_kv, gkv_ref[...]).astype(BF16)
    kvall = jnp.dot(ckvn, wkv_ref[...], preferred_element_type=F32)

    for h in range(ATT_HEADS):
        base = h * 2 * LANES
        q_ref[h, :, :QK_NOPE] = (qall[:, base:base + QK_NOPE] * ATT_SCALE_LOG2).astype(BF16)
        rope = qall[:, base + LANES:base + LANES + QK_ROPE]
        rot = qall[:, base + LANES + QK_ROPE:base + 2 * LANES]
        q_ref[h, :, QK_NOPE:] = ((rope * cos + rot * sin) * ATT_SCALE_LOG2).astype(BF16)
        k_ref[h, :, :QK_NOPE] = kvall[:, base:base + QK_NOPE].astype(BF16)
        k_ref[h, :, QK_NOPE:] = k_rope
        v_ref[h] = kvall[:, base + LANES:base + 2 * LANES].astype(BF16)


def _input_projection(x2d, meta_pad, cs, g_pre, wqa, wkva, wz, wxbc, wsm, gq, gkv, wq, wkv):
    seq = x2d.shape[0]
    n_blocks = seq // ROW_BLOCK + 1
    rows_p = seq + PAD
    real = lambda i: (jnp.maximum(i - 1, 0), 0)
    real3 = lambda i: (0, jnp.maximum(i - 1, 0), 0)
    padded = lambda i: (i, 0)
    padded3 = lambda i: (0, i, 0)
    weights = (g_pre, wqa, wkva, wz, wxbc, wsm, gq, gkv, wq, wkv)
    return pl.pallas_call(
        _proj_kernel,
        grid=(n_blocks,),
        in_specs=[_const_spec(meta_pad.shape),
                  pl.BlockSpec((ROW_BLOCK, D_MODEL), real),
                  pl.BlockSpec((ROW_BLOCK, LANES), padded)] + [_const_spec(w.shape) for w in weights],
        out_specs=[pl.BlockSpec((ATT_HEADS, ROW_BLOCK, QK_DIM), real3),
                   pl.BlockSpec((ATT_HEADS, ROW_BLOCK, QK_DIM), padded3),
                   pl.BlockSpec((ATT_HEADS, ROW_BLOCK, V_HEAD), padded3),
                   pl.BlockSpec((ROW_BLOCK, SSM_WIDTH), real),
                   pl.BlockSpec((ROW_BLOCK, CONV_DIM), padded),
                   pl.BlockSpec((ROW_BLOCK, LANES), padded)],
        out_shape=[jax.ShapeDtypeStruct((ATT_HEADS, seq, QK_DIM), BF16),
                   jax.ShapeDtypeStruct((ATT_HEADS, rows_p, QK_DIM), BF16),
                   jax.ShapeDtypeStruct((ATT_HEADS, rows_p, V_HEAD), BF16),
                   jax.ShapeDtypeStruct((seq, SSM_WIDTH), F32),
                   jax.ShapeDtypeStruct((rows_p, CONV_DIM), F32),
                   jax.ShapeDtypeStruct((rows_p, LANES), F32)],
        compiler_params=pltpu.CompilerParams(dimension_semantics=("arbitrary",),
                                             vmem_limit_bytes=VMEM_LIMIT_BYTES),
        name="input_projection",
    )(meta_pad, x2d, cs, *weights)


def _dot_nt(a, b):
    return lax.dot_general(a, b, (((1,), (1,)), ((), ())), preferred_element_type=F32)


def _attn_kernel(q_ref, k_ref, v_ref, o_ref, sa_ref, sb_ref, p_ref, m_ref, l_ref, acc_ref):
    i = pl.program_id(1)
    q = q_ref[0]
    n_tiles = ATT_BLOCK // LANES

    def keys(ref, j):
        start = pl.multiple_of(PAD + j * ATT_BLOCK, ATT_BLOCK // 2)
        return ref[0, pl.ds(start, ATT_BLOCK), :]

    def score_tiles(s_ref, masked):
        tiles = [s_ref[:, t * LANES:(t + 1) * LANES] for t in range(n_tiles)]
        if masked:
            rows = lax.broadcasted_iota(jnp.int32, (ATT_BLOCK, LANES), 0)
            cols = lax.broadcasted_iota(jnp.int32, (ATT_BLOCK, LANES), 1)
            tiles = [jnp.where(cols + t * LANES <= rows, tiles[t], -jnp.inf) for t in range(n_tiles)]
        return tiles

    def consume(s_ref, j, masked):
        tiles = score_tiles(s_ref, masked)
        mx = tiles[0]
        for t in range(1, n_tiles):
            mx = jnp.maximum(mx, tiles[t])
        m_old = m_ref[...]
        m_new = jnp.maximum(m_old, jnp.max(mx, axis=1, keepdims=True))
        m_ref[...] = m_new
        alpha = jnp.exp2(m_old - m_new)
        tiles = score_tiles(s_ref, masked)
        lsum = None
        for t in range(n_tiles):
            p = jnp.exp2(tiles[t] - m_new)
            lsum = p if lsum is None else lsum + p
            p_ref[:, t * LANES:(t + 1) * LANES] = p.astype(BF16)
        l_ref[...] = alpha * l_ref[...] + lsum
        acc_ref[...] = alpha * acc_ref[...] + jnp.dot(p_ref[...], keys(v_ref, j), preferred_element_type=F32)

    km = k_ref[0, PAD - N_META:PAD, :]
    vm = v_ref[0, PAD - N_META:PAD, :]
    s0 = _dot_nt(q, km)
    m0 = jnp.max(s0, axis=1, keepdims=True)
    p0 = jnp.exp2(s0 - m0)
    m_ref[...] = jnp.broadcast_to(m0, (ATT_BLOCK, LANES))
    l_ref[...] = jnp.where(lax.broadcasted_iota(jnp.int32, (ATT_BLOCK, LANES), 1) == 0,
                           jnp.sum(p0, axis=1, keepdims=True), 0.0)
    acc_ref[...] = jnp.dot(p0.astype(BF16), vm, preferred_element_type=F32)
    sa_ref[...] = _dot_nt(q, keys(k_ref, 0))

    def step(cur_ref, nxt_ref, j):
        nxt_ref[...] = _dot_nt(q, keys(k_ref, j + 1))
        consume(cur_ref, j, False)

    def pair(t, carry):
        step(sa_ref, sb_ref, 2 * t)
        step(sb_ref, sa_ref, 2 * t + 1)
        return carry

    lax.fori_loop(0, i // 2, pair, 0)

    def finish(s_ref):
        consume(s_ref, i, True)
        o_ref[...] = (acc_ref[...] / jnp.sum(l_ref[...], axis=1, keepdims=True)).astype(o_ref.dtype)

    @pl.when(i % 2 == 1)
    def _():
        step(sa_ref, sb_ref, i - 1)
        finish(sb_ref)

    @pl.when(i % 2 == 0)
    def _():
        finish(sa_ref)


def _attention(q, k, v):
    heads, seq, _ = q.shape
    rows_p = k.shape[1]
    return pl.pallas_call(
        _attn_kernel,
        grid=(heads, seq // ATT_BLOCK),
        in_specs=[pl.BlockSpec((1, ATT_BLOCK, QK_DIM), lambda h, i: (h, i, 0)),
                  pl.BlockSpec((1, rows_p, QK_DIM), lambda h, i: (h, 0, 0)),
                  pl.BlockSpec((1, rows_p, V_HEAD), lambda h, i: (h, 0, 0))],
        out_specs=pl.BlockSpec((ATT_BLOCK, V_HEAD), lambda h, i: (i, h)),
        out_shape=jax.ShapeDtypeStruct((seq, ATT_WIDTH), BF16),
        scratch_shapes=[pltpu.VMEM((ATT_BLOCK, ATT_BLOCK), F32),
                        pltpu.VMEM((ATT_BLOCK, ATT_BLOCK), F32),
                        pltpu.VMEM((ATT_BLOCK, ATT_BLOCK), BF16),
                        pltpu.VMEM((ATT_BLOCK, LANES), F32),
                        pltpu.VMEM((ATT_BLOCK, LANES), F32),
                        pltpu.VMEM((ATT_BLOCK, V_HEAD), F32)],
        compiler_params=pltpu.CompilerParams(dimension_semantics=("arbitrary", "arbitrary"),
                                             vmem_limit_bytes=VMEM_LIMIT_BYTES),
        name="causal_attention",
    )(q, k, v)


def _split_dot(v, mat, terms):
    out = None
    rem = v
    for t in range(terms):
        piece = rem.astype(BF16)
        part = jnp.dot(piece, mat, preferred_element_type=F32)
        out = part if out is None else out + part
        if t + 1 < terms:
            rem = rem - piece.astype(F32)
    return out


def _ssd_kernel(xbc_ref, dt_ref, convw_ref, convb_ref, dtb_ref, alog_ref, dskip_ref, tril_ref, expand_ref,
                y_ref, stage_ref, state_ref):
    i = pl.program_id(0)
    rows_b = xbc_ref.shape[0]

    @pl.when(i == 0)
    def _():
        stage_ref[0:SUBLANES, :] = jnp.zeros((SUBLANES, CONV_DIM), F32)
        state_ref[...] = jnp.zeros_like(state_ref)

    stage_ref[SUBLANES:, :] = xbc_ref[...]
    conv = convb_ref[...]
    for kk in range(CONV_K):
        off = SUBLANES - (CONV_K - 1) + kk
        conv = conv + convw_ref[kk:kk + 1, :] * stage_ref[off:off + rows_b, :]
    tail = stage_ref[rows_b:rows_b + SUBLANES, :]
    stage_ref[0:SUBLANES, :] = tail
    xc = conv * jax.nn.sigmoid(conv)

    lane = lax.broadcasted_iota(jnp.int32, (SSD_CHUNK, LANES), 1)
    row = lax.broadcasted_iota(jnp.int32, (SSD_CHUNK, LANES), 0)
    causal = lax.broadcasted_iota(jnp.int32, (SSD_CHUNK, SSD_CHUNK), 0) >= lax.broadcasted_iota(
        jnp.int32, (SSD_CHUNK, SSD_CHUNK), 1)
    low_half = lane < SSM_HEAD_DIM
    a_row = -jnp.exp(alog_ref[...])
    tril = tril_ref[...]
    expand = expand_ref[...]
    heads_per_group = SSM_HEADS // SSM_GROUPS

    for c in range(rows_b // SSD_CHUNK):
        r0 = c * SSD_CHUNK
        xs = xc[r0:r0 + SSD_CHUNK, :SSM_WIDTH]
        valid = (lane < SSM_HEADS) & (row + (i * rows_b + r0) >= PAD - N_META)
        dt = jnp.where(valid, jax.nn.softplus(dt_ref[r0:r0 + SSD_CHUNK, :] + dtb_ref[...]), 0.0)
        da = dt * a_row
        acum = _cumsum_rows(tril, da)
        acum_t = acum.T
        dt_t = dt.T
        a_last = acum[SSD_CHUNK - 1:SSD_CHUNK, :]
        e_acum = _split_dot(jnp.exp(acum), expand, 2)
        w_end = _split_dot(jnp.exp(a_last - acum) * dt, expand, 2)
        c_decay = _split_dot(jnp.broadcast_to(jnp.exp(a_last), (SUBLANES, LANES)), expand, 2)[0:1, :]

        y_diag = []
        y_off = []
        for g in range(SSM_GROUPS):
            b_g = xc[r0:r0 + SSD_CHUNK, SSM_WIDTH + g * SSM_STATE:SSM_WIDTH + (g + 1) * SSM_STATE]
            c_g = xc[r0:r0 + SSD_CHUNK, SSM_WIDTH + (SSM_GROUPS + g) * SSM_STATE:
                     SSM_WIDTH + (SSM_GROUPS + g + 1) * SSM_STATE].astype(BF16)
            b_t = b_g.T.astype(BF16)
            cb = jnp.dot(c_g, b_t, preferred_element_type=F32)
            gcols = slice(g * GROUP_WIDTH, (g + 1) * GROUP_WIDTH)
            for jp in range(heads_per_group // 2):
                ws = []
                for r in (g * heads_per_group + 2 * jp, g * heads_per_group + 2 * jp + 1):
                    seg = acum[:, r:r + 1] - acum_t[r:r + 1, :]
                    decay = jnp.where(causal, jnp.exp(seg), 0.0)
                    ws.append((cb * decay * dt_t[r:r + 1, :]).astype(BF16))
                col = g * GROUP_WIDTH + jp * LANES
                x_pair = xs[:, col:col + LANES]
                x_cat = jnp.concatenate([jnp.where(low_half, x_pair, 0.0), jnp.where(low_half, 0.0, x_pair)],
                                        axis=0).astype(BF16)
                y_diag.append(jnp.dot(jnp.concatenate(ws, axis=1), x_cat, preferred_element_type=F32))
            h_prev = state_ref[g]
            y_off.append(jnp.dot(c_g, h_prev.astype(BF16), preferred_element_type=F32))
            x_end = (xs[:, gcols] * w_end[:, gcols]).astype(BF16)
            state_ref[g] = c_decay[:, gcols] * h_prev + jnp.dot(b_t, x_end, preferred_element_type=F32)
        y = (jnp.concatenate(y_diag, axis=1) + jnp.concatenate(y_off, axis=1) * e_acum
             + xs * dskip_ref[...])
        y_ref[r0:r0 + SSD_CHUNK, :] = y


def _cumsum_rows(tril, v):
    out = None
    rem = v
    for t in range(3):
        piece = rem.astype(BF16)
        part = jnp.dot(tril, piece, preferred_element_type=F32)
        out = part if out is None else out + part
        if t < 2:
            rem = rem - piece.astype(F32)
    return out


def _ssd_scan(xbc, dt_raw, conv_w, conv_b, dt_bias, a_log, d_skip_x, tril, expand, seq):
    n_blocks = seq // ROW_BLOCK + 1
    consts = (conv_w, conv_b, dt_bias, a_log, d_skip_x, tril, expand)
    return pl.pallas_call(
        _ssd_kernel,
        grid=(n_blocks,),
        in_specs=[pl.BlockSpec((ROW_BLOCK, CONV_DIM), lambda i: (i, 0)),
                  pl.BlockSpec((ROW_BLOCK, LANES), lambda i: (i, 0))] + [_const_spec(c.shape) for c in consts],
        out_specs=pl.BlockSpec((ROW_BLOCK, SSM_WIDTH), lambda i: (jnp.maximum(i - 1, 0), 0)),
        out_shape=jax.ShapeDtypeStruct((seq, SSM_WIDTH), F32),
        scratch_shapes=[pltpu.VMEM((SUBLANES + ROW_BLOCK, CONV_DIM), F32),
                        pltpu.VMEM((SSM_GROUPS, SSM_STATE, GROUP_WIDTH), F32)],
        compiler_params=pltpu.CompilerParams(dimension_semantics=("arbitrary",),
                                             vmem_limit_bytes=VMEM_LIMIT_BYTES),
        name="ssd_scan",
    )(xbc, dt_raw, *consts)


def _out_mlp_kernel(att_ref, y_ref, z_ref, x_ref, g_ssm_ref, wout_ref, g_post_ref, g_mpre_ref,
                    wup_ref, wdown_ref, g_mpost_ref, o_ref):
    z = z_ref[...]
    gated = y_ref[...] * (z * jax.nn.sigmoid(z))
    parts = []
    for g in range(SSM_GROUPS):
        gg = gated[:, g * GROUP_WIDTH:(g + 1) * GROUP_WIDTH]
        parts.append(gg * lax.rsqrt(jnp.mean(jnp.square(gg), axis=-1, keepdims=True) + EPS))
    ssm = (jnp.concatenate(parts, axis=1) * g_ssm_ref[...]).astype(BF16)
    mix = (jnp.dot(att_ref[...], wout_ref[:ATT_WIDTH, :], preferred_element_type=F32)
           + jnp.dot(ssm, wout_ref[ATT_WIDTH:, :], preferred_element_type=F32))
    h1 = x_ref[...] + _rms(mix, g_post_ref[...])
    hn = _rms(h1, g_mpre_ref[...]).astype(BF16)
    f = None
    for c in range(D_FF // FF_CHUNK):
        u = jnp.dot(hn, wup_ref[:, c * FF_CHUNK:(c + 1) * FF_CHUNK], preferred_element_type=F32)
        u = jnp.square(jnp.maximum(u, 0.0)).astype(BF16)
        part = jnp.dot(u, wdown_ref[c * FF_CHUNK:(c + 1) * FF_CHUNK, :], preferred_element_type=F32)
        f = part if f is None else f + part
    o_ref[...] = h1 + _rms(f, g_mpost_ref[...])


def _out_mlp(att, y, z, x2d, g_ssm, wout, g_post, g_mpre, wup, wdown, g_mpost):
    seq = x2d.shape[0]
    weights = (g_ssm, wout, g_post, g_mpre, wup, wdown, g_mpost)
    row = lambda i: (i, 0)
    return pl.pallas_call(
        _out_mlp_kernel,
        grid=(seq // ROW_BLOCK,),
        in_specs=[pl.BlockSpec((ROW_BLOCK, ATT_WIDTH), row),
                  pl.BlockSpec((ROW_BLOCK, SSM_WIDTH), row),
                  pl.BlockSpec((ROW_BLOCK, SSM_WIDTH), row),
                  pl.BlockSpec((ROW_BLOCK, D_MODEL), row)] + [_const_spec(w.shape) for w in weights],
        out_specs=pl.BlockSpec((ROW_BLOCK, D_MODEL), row),
        out_shape=jax.ShapeDtypeStruct((seq, D_MODEL), F32),
        compiler_params=pltpu.CompilerParams(dimension_semantics=("arbitrary",),
                                             vmem_limit_bytes=VMEM_LIMIT_BYTES),
        name="out_proj_mlp",
    )(att, y, z, x2d, *weights)


def _rot_cols(w):
    half = QK_ROPE // 2
    return jnp.concatenate([-w[..., half:], w[..., :half]], axis=-1)


def _lane_pad(v, width=LANES):
    return jnp.pad(v, [(0, 0)] * (v.ndim - 1) + [(0, width - v.shape[-1])])


def kernel(x, meta_tokens, norm_mix_pre, w_in, q_a_norm, w_q_up, kv_a_norm, w_kv_up, conv_w, conv_b, dt_bias,
           a_log, d_skip, ssm_norm, w_out, norm_mix_post, norm_mlp_pre, w_mlp_up, w_mlp_down, norm_mlp_post):
    bsz, seq, _ = x.shape
    assert bsz == 1 and norm_mix_pre.shape[0] == 1, "single sequence, single layer"
    assert seq % ATT_BLOCK == 0 and seq % ROW_BLOCK == 0
    x2d = x[0]
    row = lambda v: v.reshape(1, -1)

    o_ckv, o_kr, o_z, o_xbc, o_dt = Q_LORA, Q_LORA + KV_LORA, Q_LORA + KV_LORA + QK_ROPE, \
        Q_LORA + KV_LORA + QK_ROPE + SSM_WIDTH, Q_LORA + KV_LORA + QK_ROPE + SSM_WIDTH + CONV_DIM
    w_in0 = w_in[0]
    w_kr = w_in0[:, o_kr:o_z]
    wsm = jnp.concatenate([_lane_pad(w_kr), _lane_pad(_rot_cols(w_kr)), _lane_pad(w_in0[:, o_dt:])], axis=1)
    wq3 = w_q_up[0].reshape(Q_LORA, ATT_HEADS, QK_DIM)
    wq = jnp.concatenate([wq3, _rot_cols(wq3[..., QK_NOPE:])], axis=-1).reshape(Q_LORA, ATT_HEADS * 2 * LANES)

    pos = jnp.maximum(jnp.arange(seq + PAD) - (PAD - N_META), 0).astype(F32)
    inv_freq = ROPE_THETA ** (-jnp.arange(0, QK_ROPE, 2, dtype=F32) / QK_ROPE)
    ang = pos[:, None] * inv_freq[None, :]
    cos, sin = jnp.cos(ang), jnp.sin(ang)
    cs = jnp.concatenate([cos, cos, sin, sin], axis=1)

    meta_pad = jnp.concatenate([jnp.zeros((PAD - N_META, D_MODEL), F32), meta_tokens.astype(F32)], axis=0)

    q, k, v, z, xbc, dt_raw = _input_projection(
        x2d, meta_pad, cs, row(norm_mix_pre[0]),
        w_in0[:, :o_ckv].astype(BF16), w_in0[:, o_ckv:o_kr].astype(BF16), w_in0[:, o_z:o_xbc].astype(BF16),
        w_in0[:, o_xbc:o_dt].astype(BF16), wsm.astype(BF16),
        row(q_a_norm[0]), row(kv_a_norm[0]), wq.astype(BF16), w_kv_up[0].astype(BF16))

    att = _attention(q, k, v)

    tril = jnp.tril(jnp.ones((SSD_CHUNK, SSD_CHUNK), BF16))
    expand = (jnp.arange(LANES)[:, None] == jnp.arange(SSM_WIDTH)[None, :] // SSM_HEAD_DIM).astype(BF16)
    y = _ssd_scan(xbc, dt_raw, conv_w[0], row(conv_b[0]), _lane_pad(row(dt_bias[0])), _lane_pad(row(a_log[0])),
                  row(jnp.repeat(d_skip[0], SSM_HEAD_DIM)), tril, expand, seq)

    out = _out_mlp(att, y, z, x2d, row(ssm_norm[0]), w_out[0].astype(BF16), row(norm_mix_post[0]),
                   row(norm_mlp_pre[0]), w_mlp_up[0].astype(BF16), w_mlp_down[0].astype(BF16),
                   row(norm_mlp_post[0]))
    return out[None]
```

```python
import functools
import math

import jax
import jax.numpy as jnp
import numpy as np
from jax import lax
from jax.experimental import pallas as pl
from jax.experimental.pallas import tpu as pltpu

D_MODEL = 1024
N_META = 16
EPS = 1e-6
ATT_HEADS = 8
Q_LORA = 384
KV_LORA = 256
QK_NOPE = 128
QK_ROPE = 64
V_HEAD = 128
QK_DIM = QK_NOPE + QK_ROPE
ROPE_THETA = 10000.0
ATT_WIDTH = ATT_HEADS * V_HEAD
SSM_HEADS = 16
SSM_HEAD_DIM = 64
SSM_WIDTH = SSM_HEADS * SSM_HEAD_DIM
SSM_GROUPS = 2
SSM_STATE = 128
GROUP_WIDTH = SSM_WIDTH // SSM_GROUPS
CONV_K = 4
CONV_DIM = SSM_WIDTH + 2 * SSM_GROUPS * SSM_STATE
D_FF = 4 * D_MODEL

LANES = 128
SUBLANES = 8
VMEM_LIMIT_BYTES = 56 * 1024 * 1024

PAD = 256
ROW_BLOCK = 256
ATT_Q = 512
ATT_K = 512
ATT_HEAD_GROUP = 1
ATT_UNROLL = 4
SSD_CHUNK = 128
FF_CHUNK = 1024
ATT_SCALE_LOG2 = (QK_DIM ** -0.5) * math.log2(math.e)

BF16 = jnp.bfloat16
F32 = jnp.float32


def _const_spec(shape):
    nd = len(shape)
    return pl.BlockSpec(shape, lambda *_: (0,) * nd)


def _rms(x, gain):
    return x * lax.rsqrt(jnp.mean(jnp.square(x), axis=-1, keepdims=True) + EPS) * gain


def _proj_kernel(meta_ref, x_ref, cs_ref, g_pre_ref, wqa_ref, wkva_ref, wz_ref, wxbc_ref, wsm_ref,
                 gq_ref, gkv_ref, wq_ref, wkv_ref,
                 q_ref, k_ref, v_ref, z_ref, xbc_ref, dt_ref):
    i = pl.program_id(0)
    xin = jnp.where(i == 0, meta_ref[...], x_ref[...])
    hn = _rms(xin, g_pre_ref[...]).astype(BF16)

    z_ref[...] = jnp.dot(hn, wz_ref[...], preferred_element_type=F32)
    xbc_ref[...] = jnp.dot(hn, wxbc_ref[...], preferred_element_type=F32)
    small = jnp.dot(hn, wsm_ref[...], preferred_element_type=F32)
    dt_ref[...] = small[:, 2 * LANES:3 * LANES]

    cs = cs_ref[...]
    cos = cs[:, :QK_ROPE]
    sin = cs[:, QK_ROPE:]
    k_rope = (small[:, :QK_ROPE] * cos + small[:, LANES:LANES + QK_ROPE] * sin).astype(BF16)

    c_q = jnp.dot(hn, wqa_ref[...], preferred_element_type=F32)
    cqn = _rms(c_q, gq_ref[...]).astype(BF16)
    qall = jnp.dot(cqn, wq_ref[...], preferred_element_type=F32)
    c_kv = jnp.dot(hn, wkva_ref[...], preferred_element_type=F32)
    ckvn = _rms(c_kv, gkv_ref[...]).astype(BF16)
    kvall = jnp.dot(ckvn, wkv_ref[...], preferred_element_type=F32)

    for h in range(ATT_HEADS):
        base = h * 2 * LANES
        q_ref[h, :, :QK_NOPE] = (qall[:, base:base + QK_NOPE] * ATT_SCALE_LOG2).astype(BF16)
        rope = qall[:, base + LANES:base + LANES + QK_ROPE]
        rot = qall[:, base + LANES + QK_ROPE:base + 2 * LANES]
        q_ref[h, :, QK_NOPE:] = ((rope * cos + rot * sin) * ATT_SCALE_LOG2).astype(BF16)
        k_ref[h, :, :QK_NOPE] = kvall[:, base:base + QK_NOPE].astype(BF16)
        k_ref[h, :, QK_NOPE:] = k_rope
        v_ref[h] = kvall[:, base + LANES:base + 2 * LANES].astype(BF16)


def _input_projection(x2d, meta_pad, cs, g_pre, wqa, wkva, wz, wxbc, wsm, gq, gkv, wq, wkv):
    seq = x2d.shape[0]
    n_blocks = seq // ROW_BLOCK + 1
    rows_p = seq + PAD
    real = lambda i: (jnp.maximum(i - 1, 0), 0)
    real3 = lambda i: (0, jnp.maximum(i - 1, 0), 0)
    padded = lambda i: (i, 0)
    padded3 = lambda i: (0, i, 0)
    weights = (g_pre, wqa, wkva, wz, wxbc, wsm, gq, gkv, wq, wkv)
    return pl.pallas_call(
        _proj_kernel,
        grid=(n_blocks,),
        in_specs=[_const_spec(meta_pad.shape),
                  pl.BlockSpec((ROW_BLOCK, D_MODEL), real),
                  pl.BlockSpec((ROW_BLOCK, LANES), padded)] + [_const_spec(w.shape) for w in weights],
        out_specs=[pl.BlockSpec((ATT_HEADS, ROW_BLOCK, QK_DIM), real3),
                   pl.BlockSpec((ATT_HEADS, ROW_BLOCK, QK_DIM), padded3),
                   pl.BlockSpec((ATT_HEADS, ROW_BLOCK, V_HEAD), padded3),
                   pl.BlockSpec((ROW_BLOCK, SSM_WIDTH), real),
                   pl.BlockSpec((ROW_BLOCK, CONV_DIM), padded),
                   pl.BlockSpec((ROW_BLOCK, LANES), padded)],
        out_shape=[jax.ShapeDtypeStruct((ATT_HEADS, seq, QK_DIM), BF16),
                   jax.ShapeDtypeStruct((ATT_HEADS, rows_p, QK_DIM), BF16),
                   jax.ShapeDtypeStruct((ATT_HEADS, rows_p, V_HEAD), BF16),
                   jax.ShapeDtypeStruct((seq, SSM_WIDTH), F32),
                   jax.ShapeDtypeStruct((rows_p, CONV_DIM), F32),
                   jax.ShapeDtypeStruct((rows_p, LANES), F32)],
        compiler_params=pltpu.CompilerParams(dimension_semantics=("arbitrary",),
                                             vmem_limit_bytes=VMEM_LIMIT_BYTES),
        name="input_projection",
    )(meta_pad, x2d, cs, *weights)


def _dot_nt(a, b):
    return lax.dot_general(a, b, (((1,), (1,)), ((), ())), preferred_element_type=F32)


def _attn_kernel(q_ref, k_ref, v_ref, o_ref, sd_ref, pd_ref, s_ref, p_ref, m_ref, l_ref, acc_ref):
    i = pl.program_id(1)
    n_tiles = ATT_K // LANES
    assert ATT_Q == ATT_K, "one diagonal chunk per query block"
    assert ATT_UNROLL % 2 == 0 and PAD >= LANES

    def rows_at(ref, hh, j):
        start = pl.multiple_of(PAD + j * ATT_K, math.gcd(PAD, ATT_K))
        return ref[hh, pl.ds(start, ATT_K), :]

    def scores(hh, buf, j):
        s_ref[hh, buf] = _dot_nt(q_ref[hh], rows_at(k_ref, hh, j))

    def score_tiles(hh, buf):
        return [s_ref[hh, buf, :, t * LANES:(t + 1) * LANES] for t in range(n_tiles)]

    def consume(hh, buf, j):
        tiles = score_tiles(hh, buf)
        mx = tiles[0]
        for t in range(1, n_tiles):
            mx = jnp.maximum(mx, tiles[t])
        m_old = m_ref[hh]
        m_new = jnp.maximum(m_old, jnp.max(mx, axis=1, keepdims=True))
        m_ref[hh] = m_new
        alpha = jnp.exp2(m_old - m_new)
        tiles = score_tiles(hh, buf)
        lsum = None
        for t in range(n_tiles):
            p = jnp.exp2(tiles[t] - m_new)
            lsum = p if lsum is None else lsum + p
            p_ref[hh, :, t * LANES:(t + 1) * LANES] = p.astype(BF16)
        l_ref[hh] = alpha * l_ref[hh] + lsum
        acc_ref[hh] = alpha * acc_ref[hh] + jnp.dot(p_ref[hh], rows_at(v_ref, hh, j),
                                                    preferred_element_type=F32)

    def step(cur, nxt, j):
        scores(0, nxt, j + 1)
        consume(0, cur, j)

    def first_chunk():
        rows = lax.broadcasted_iota(jnp.int32, (ATT_Q, LANES), 0)
        cols = lax.broadcasted_iota(jnp.int32, (ATT_Q, LANES), 1)

        def tiles():
            out = [jnp.where(cols >= LANES - N_META, sd_ref[:, :LANES], -jnp.inf)]
            for t in range(n_tiles):
                out.append(jnp.where(cols + t * LANES <= rows, sd_ref[:, (t + 1) * LANES:(t + 2) * LANES],
                                     -jnp.inf))
            return out

        ts = tiles()
        mx = ts[0]
        for t in range(1, n_tiles + 1):
            mx = jnp.maximum(mx, ts[t])
        m_new = jnp.broadcast_to(jnp.max(mx, axis=1, keepdims=True), (ATT_Q, LANES))
        m_ref[0] = m_new
        ts = tiles()
        lsum = None
        for t in range(n_tiles + 1):
            p = jnp.exp2(ts[t] - m_new)
            lsum = p if lsum is None else lsum + p
            pd_ref[:, t * LANES:(t + 1) * LANES] = p.astype(BF16)
        l_ref[0] = lsum
        acc_ref[0] = (jnp.dot(pd_ref[:, :LANES], v_ref[0, PAD - LANES:PAD, :], preferred_element_type=F32)
                      + jnp.dot(pd_ref[:, LANES:], rows_at(v_ref, 0, i), preferred_element_type=F32))

    sd_ref[:, :LANES] = _dot_nt(q_ref[0], k_ref[0, PAD - LANES:PAD, :])
    sd_ref[:, LANES:] = _dot_nt(q_ref[0], rows_at(k_ref, 0, i))
    scores(0, 0, 0)
    first_chunk()

    def quad(t, carry):
        for u in range(ATT_UNROLL):
            step(u % 2, (u + 1) % 2, ATT_UNROLL * t + u)
        return carry

    lax.fori_loop(0, i // ATT_UNROLL, quad, 0)
    rem = i % ATT_UNROLL
    base = i - rem
    for u in range(ATT_UNROLL - 1):
        @pl.when(rem > u)
        def _():
            step(u % 2, (u + 1) % 2, base + u)

    o_ref[...] = (acc_ref[0] / jnp.sum(l_ref[0], axis=1, keepdims=True)).astype(o_ref.dtype)


def _attention(q, k, v):
    heads, seq, _ = q.shape
    rows_p = k.shape[1]
    hg = ATT_HEAD_GROUP
    return pl.pallas_call(
        _attn_kernel,
        grid=(heads // hg, seq // ATT_Q),
        in_specs=[pl.BlockSpec((hg, ATT_Q, QK_DIM), lambda h, i: (h, i, 0)),
                  pl.BlockSpec((hg, rows_p, QK_DIM), lambda h, i: (h, 0, 0)),
                  pl.BlockSpec((hg, rows_p, V_HEAD), lambda h, i: (h, 0, 0))],
        out_specs=pl.BlockSpec((ATT_Q, hg * V_HEAD), lambda h, i: (i, h)),
        out_shape=jax.ShapeDtypeStruct((seq, ATT_WIDTH), BF16),
        scratch_shapes=[pltpu.VMEM((ATT_Q, LANES + ATT_K), F32),
                        pltpu.VMEM((ATT_Q, LANES + ATT_K), BF16),
                        pltpu.VMEM((hg, 2, ATT_Q, ATT_K), F32),
                        pltpu.VMEM((hg, ATT_Q, ATT_K), BF16),
                        pltpu.VMEM((hg, ATT_Q, LANES), F32),
                        pltpu.VMEM((hg, ATT_Q, LANES), F32),
                        pltpu.VMEM((hg, ATT_Q, V_HEAD), F32)],
        compiler_params=pltpu.CompilerParams(dimension_semantics=("arbitrary", "arbitrary"),
                                             vmem_limit_bytes=VMEM_LIMIT_BYTES),
        name="causal_attention",
    )(q, k, v)


def _split_dot(v, mat, terms):
    out = None
    rem = v
    for t in range(terms):
        piece = rem.astype(BF16)
        part = jnp.dot(piece, mat, preferred_element_type=F32)
        out = part if out is None else out + part
        if t + 1 < terms:
            rem = rem - piece.astype(F32)
    return out


def _ssd_kernel(xbc_ref, dt_ref, convw_ref, convb_ref, dtb_ref, alog_ref, dskip_ref, tril_ref, expand_ref,
                y_ref, stage_ref, state_ref):
    i = pl.program_id(0)
    rows_b = xbc_ref.shape[0]

    @pl.when(i == 0)
    def _():
        stage_ref[0:SUBLANES, :] = jnp.zeros((SUBLANES, CONV_DIM), F32)
        state_ref[...] = jnp.zeros_like(state_ref)

    stage_ref[SUBLANES:, :] = xbc_ref[...]
    conv = convb_ref[...]
    for kk in range(CONV_K):
        off = SUBLANES - (CONV_K - 1) + kk
        conv = conv + convw_ref[kk:kk + 1, :] * stage_ref[off:off + rows_b, :]
    tail = stage_ref[rows_b:rows_b + SUBLANES, :]
    stage_ref[0:SUBLANES, :] = tail
    xc = conv * jax.nn.sigmoid(conv)

    lane = lax.broadcasted_iota(jnp.int32, (SSD_CHUNK, LANES), 1)
    row = lax.broadcasted_iota(jnp.int32, (SSD_CHUNK, LANES), 0)
    causal = lax.broadcasted_iota(jnp.int32, (SSD_CHUNK, SSD_CHUNK), 0) >= lax.broadcasted_iota(
        jnp.int32, (SSD_CHUNK, SSD_CHUNK), 1)
    low_half = lane < SSM_HEAD_DIM
    a_row = -jnp.exp(alog_ref[...])
    tril = tril_ref[...]
    expand = expand_ref[...]
    heads_per_group = SSM_HEADS // SSM_GROUPS

    for c in range(rows_b // SSD_CHUNK):
        r0 = c * SSD_CHUNK
        xs = xc[r0:r0 + SSD_CHUNK, :SSM_WIDTH]
        valid = (lane < SSM_HEADS) & (row + (i * rows_b + r0) >= PAD - N_META)
        dt = jnp.where(valid, jax.nn.softplus(dt_ref[r0:r0 + SSD_CHUNK, :] + dtb_ref[...]), 0.0)
        da = dt * a_row
        acum = _cumsum_rows(tril, da)
        acum_t = acum.T
        dt_t = dt.T
        a_last = acum[SSD_CHUNK - 1:SSD_CHUNK, :]
        e_acum = _split_dot(jnp.exp(acum), expand, 2)
        w_end = _split_dot(jnp.exp(a_last - acum) * dt, expand, 2)
        c_decay = _split_dot(jnp.broadcast_to(jnp.exp(a_last), (SUBLANES, LANES)), expand, 2)[0:1, :]

        y_diag = []
        y_off = []
        for g in range(SSM_GROUPS):
            b_g = xc[r0:r0 + SSD_CHUNK, SSM_WIDTH + g * SSM_STATE:SSM_WIDTH + (g + 1) * SSM_STATE]
            c_g = xc[r0:r0 + SSD_CHUNK, SSM_WIDTH + (SSM_GROUPS + g) * SSM_STATE:
                     SSM_WIDTH + (SSM_GROUPS + g + 1) * SSM_STATE].astype(BF16)
            b_t = b_g.T.astype(BF16)
            cb = jnp.dot(c_g, b_t, preferred_element_type=F32)
            gcols = slice(g * GROUP_WIDTH, (g + 1) * GROUP_WIDTH)
            for jp in range(heads_per_group // 2):
                ws = []
                for r in (g * heads_per_group + 2 * jp, g * heads_per_group + 2 * jp + 1):
                    seg = acum[:, r:r + 1] - acum_t[r:r + 1, :]
                    decay = jnp.where(causal, jnp.exp(seg), 0.0)
                    ws.append((cb * decay * dt_t[r:r + 1, :]).astype(BF16))
                col = g * GROUP_WIDTH + jp * LANES
                x_pair = xs[:, col:col + LANES]
                x_cat = jnp.concatenate([jnp.where(low_half, x_pair, 0.0), jnp.where(low_half, 0.0, x_pair)],
                                        axis=0).astype(BF16)
                y_diag.append(jnp.dot(jnp.concatenate(ws, axis=1), x_cat, preferred_element_type=F32))
            h_prev = state_ref[g]
            y_off.append(jnp.dot(c_g, h_prev.astype(BF16), preferred_element_type=F32))
            x_end = (xs[:, gcols] * w_end[:, gcols]).astype(BF16)
            state_ref[g] = c_decay[:, gcols] * h_prev + jnp.dot(b_t, x_end, preferred_element_type=F32)
        y = (jnp.concatenate(y_diag, axis=1) + jnp.concatenate(y_off, axis=1) * e_acum
             + xs * dskip_ref[...])
        y_ref[r0:r0 + SSD_CHUNK, :] = y


def _cumsum_rows(tril, v):
    out = None
    rem = v
    for t in range(3):
        piece = rem.astype(BF16)
        part = jnp.dot(tril, piece, preferred_element_type=F32)
        out = part if out is None else out + part
        if t < 2:
            rem = rem - piece.astype(F32)
    return out


def _ssd_scan(xbc, dt_raw, conv_w, conv_b, dt_bias, a_log, d_skip_x, tril, expand, seq):
    n_blocks = seq // ROW_BLOCK + 1
    consts = (conv_w, conv_b, dt_bias, a_log, d_skip_x, tril, expand)
    return pl.pallas_call(
        _ssd_kernel,
        grid=(n_blocks,),
        in_specs=[pl.BlockSpec((ROW_BLOCK, CONV_DIM), lambda i: (i, 0)),
                  pl.BlockSpec((ROW_BLOCK, LANES), lambda i: (i, 0))] + [_const_spec(c.shape) for c in consts],
        out_specs=pl.BlockSpec((ROW_BLOCK, SSM_WIDTH), lambda i: (jnp.maximum(i - 1, 0), 0)),
        out_shape=jax.ShapeDtypeStruct((seq, SSM_WIDTH), F32),
        scratch_shapes=[pltpu.VMEM((SUBLANES + ROW_BLOCK, CONV_DIM), F32),
                        pltpu.VMEM((SSM_GROUPS, SSM_STATE, GROUP_WIDTH), F32)],
        compiler_params=pltpu.CompilerParams(dimension_semantics=("arbitrary",),
                                             vmem_limit_bytes=VMEM_LIMIT_BYTES),
        name="ssd_scan",
    )(xbc, dt_raw, *consts)


def _out_mlp_kernel(att_ref, y_ref, z_ref, x_ref, g_ssm_ref, wout_ref, g_post_ref, g_mpre_ref,
                    wup_ref, wdown_ref, g_mpost_ref, o_ref):
    z = z_ref[...]
    gated = y_ref[...] * (z * jax.nn.sigmoid(z))
    parts = []
    for g in range(SSM_GROUPS):
        gg = gated[:, g * GROUP_WIDTH:(g + 1) * GROUP_WIDTH]
        parts.append(gg * lax.rsqrt(jnp.mean(jnp.square(gg), axis=-1, keepdims=True) + EPS))
    ssm = (jnp.concatenate(parts, axis=1) * g_ssm_ref[...]).astype(BF16)
    mix = (jnp.dot(att_ref[...], wout_ref[:ATT_WIDTH, :], preferred_element_type=F32)
           + jnp.dot(ssm, wout_ref[ATT_WIDTH:, :], preferred_element_type=F32))
    h1 = x_ref[...] + _rms(mix, g_post_ref[...])
    hn = _rms(h1, g_mpre_ref[...]).astype(BF16)
    f = None
    for c in range(D_FF // FF_CHUNK):
        u = jnp.dot(hn, wup_ref[:, c * FF_CHUNK:(c + 1) * FF_CHUNK], preferred_element_type=F32)
        u = jnp.square(jnp.maximum(u, 0.0)).astype(BF16)
        part = jnp.dot(u, wdown_ref[c * FF_CHUNK:(c + 1) * FF_CHUNK, :], preferred_element_type=F32)
        f = part if f is None else f + part
    o_ref[...] = h1 + _rms(f, g_mpost_ref[...])


def _out_mlp(att, y, z, x2d, g_ssm, wout, g_post, g_mpre, wup, wdown, g_mpost):
    seq = x2d.shape[0]
    weights = (g_ssm, wout, g_post, g_mpre, wup, wdown, g_mpost)
    row = lambda i: (i, 0)
    return pl.pallas_call(
        _out_mlp_kernel,
        grid=(seq // ROW_BLOCK,),
        in_specs=[pl.BlockSpec((ROW_BLOCK, ATT_WIDTH), row),
                  pl.BlockSpec((ROW_BLOCK, SSM_WIDTH), row),
                  pl.BlockSpec((ROW_BLOCK, SSM_WIDTH), row),
                  pl.BlockSpec((ROW_BLOCK, D_MODEL), row)] + [_const_spec(w.shape) for w in weights],
        out_specs=pl.BlockSpec((ROW_BLOCK, D_MODEL), row),
        out_shape=jax.ShapeDtypeStruct((seq, D_MODEL), F32),
        compiler_params=pltpu.CompilerParams(dimension_semantics=("arbitrary",),
                                             vmem_limit_bytes=VMEM_LIMIT_BYTES),
        name="out_proj_mlp",
    )(att, y, z, x2d, *weights)


def _rot_cols(w):
    half = QK_ROPE // 2
    return jnp.concatenate([-w[..., half:], w[..., :half]], axis=-1)


def _lane_pad(v, width=LANES):
    return jnp.pad(v, [(0, 0)] * (v.ndim - 1) + [(0, width - v.shape[-1])])


def kernel(x, meta_tokens, norm_mix_pre, w_in, q_a_norm, w_q_up, kv_a_norm, w_kv_up, conv_w, conv_b, dt_bias,
           a_log, d_skip, ssm_norm, w_out, norm_mix_post, norm_mlp_pre, w_mlp_up, w_mlp_down, norm_mlp_post):
    bsz, seq, _ = x.shape
    assert bsz == 1 and norm_mix_pre.shape[0] == 1, "single sequence, single layer"
    assert seq % ATT_Q == 0 and seq % ROW_BLOCK == 0
    x2d = x[0]
    row = lambda v: v.reshape(1, -1)

    o_ckv, o_kr, o_z, o_xbc, o_dt = Q_LORA, Q_LORA + KV_LORA, Q_LORA + KV_LORA + QK_ROPE, \
        Q_LORA + KV_LORA + QK_ROPE + SSM_WIDTH, Q_LORA + KV_LORA + QK_ROPE + SSM_WIDTH + CONV_DIM
    w_in0 = w_in[0]
    w_kr = w_in0[:, o_kr:o_z]
    wsm = jnp.concatenate([_lane_pad(w_kr), _lane_pad(_rot_cols(w_kr)), _lane_pad(w_in0[:, o_dt:])], axis=1)
    wq3 = w_q_up[0].reshape(Q_LORA, ATT_HEADS, QK_DIM)
    wq = jnp.concatenate([wq3, _rot_cols(wq3[..., QK_NOPE:])], axis=-1).reshape(Q_LORA, ATT_HEADS * 2 * LANES)

    pos = jnp.maximum(jnp.arange(seq + PAD) - (PAD - N_META), 0).astype(F32)
    inv_freq = ROPE_THETA ** (-jnp.arange(0, QK_ROPE, 2, dtype=F32) / QK_ROPE)
    ang = pos[:, None] * inv_freq[None, :]
    cos, sin = jnp.cos(ang), jnp.sin(ang)
    cs = jnp.concatenate([cos, cos, sin, sin], axis=1)

    meta_pad = jnp.concatenate([jnp.zeros((PAD - N_META, D_MODEL), F32), meta_tokens.astype(F32)], axis=0)

    q, k, v, z, xbc, dt_raw = _input_projection(
        x2d, meta_pad, cs, row(norm_mix_pre[0]),
        w_in0[:, :o_ckv].astype(BF16), w_in0[:, o_ckv:o_kr].astype(BF16), w_in0[:, o_z:o_xbc].astype(BF16),
        w_in0[:, o_xbc:o_dt].astype(BF16), wsm.astype(BF16),
        row(q_a_norm[0]), row(kv_a_norm[0]), wq.astype(BF16), w_kv_up[0].astype(BF16))

    att = _attention(q, k, v)

    tril = jnp.tril(jnp.ones((SSD_CHUNK, SSD_CHUNK), BF16))
    expand = (jnp.arange(LANES)[:, None] == jnp.arange(SSM_WIDTH)[None, :] // SSM_HEAD_DIM).astype(BF16)
    y = _ssd_scan(xbc, dt_raw, conv_w[0], row(conv_b[0]), _lane_pad(row(dt_bias[0])), _lane_pad(row(a_log[0])),
                  row(jnp.repeat(d_skip[0], SSM_HEAD_DIM)), tril, expand, seq)

    out = _out_mlp(att, y, z, x2d, row(ssm_norm[0]), w_out[0].astype(BF16), row(norm_mix_post[0]),
                   row(norm_mlp_pre[0]), w_mlp_up[0].astype(BF16), w_mlp_down[0].astype(BF16),
                   row(norm_mlp_post[0]))
    return out[None]
```

```python
import functools
import math

import jax
import jax.numpy as jnp
import numpy as np
from jax import lax
from jax.experimental import pallas as pl
from jax.experimental.pallas import tpu as pltpu

D_MODEL = 1024
N_META = 16
EPS = 1e-6
ATT_HEADS = 8
Q_LORA = 384
KV_LORA = 256
QK_NOPE = 128
QK_ROPE = 64
V_HEAD = 128
QK_DIM = QK_NOPE + QK_ROPE
ROPE_THETA = 10000.0
ATT_WIDTH = ATT_HEADS * V_HEAD
SSM_HEADS = 16
SSM_HEAD_DIM = 64
SSM_WIDTH = SSM_HEADS * SSM_HEAD_DIM
SSM_GROUPS = 2
SSM_STATE = 128
GROUP_WIDTH = SSM_WIDTH // SSM_GROUPS
CONV_K = 4
CONV_DIM = SSM_WIDTH + 2 * SSM_GROUPS * SSM_STATE
D_FF = 4 * D_MODEL

LANES = 128
SUBLANES = 8
VMEM_LIMIT_BYTES = 56 * 1024 * 1024

QK_PAD = 2 * LANES
PAD = 256
ROW_BLOCK = 256
MLP_BLOCK = 512
ROW_SPLIT = 1
ATT_Q = 512
ATT_K = 512
ATT_SUB = 2
ATT_UNROLL = 4
SSD_CHUNK = 128
FF_CHUNK = 1024
ATT_SCALE_LOG2 = (QK_DIM ** -0.5) * math.log2(math.e)

BF16 = jnp.bfloat16
F32 = jnp.float32


def _const_spec(shape):
    nd = len(shape)
    return pl.BlockSpec(shape, lambda *_: (0,) * nd, pipeline_mode=pl.Buffered(1))


def _rms(x, gain):
    return x * lax.rsqrt(jnp.mean(jnp.square(x), axis=-1, keepdims=True) + EPS) * gain


def _proj_kernel(meta_ref, x_ref, rope_blk_ref, rope_loc_ref, g_pre_ref, wqa_ref, wkva_ref, wz_ref, wxbc_ref,
                 wsm_ref, gq_ref, gkv_ref, wq_ref, wkv_ref, convw_ref, convb_ref,
                 q_ref, k_ref, v_ref, z_ref, xbc_ref, dt_ref, stage_ref):
    i = pl.program_id(0)

    @pl.when(i == 0)
    def _():
        stage_ref[0:SUBLANES, :] = jnp.zeros((SUBLANES, CONV_DIM), F32)

    xin = jnp.where(i == 0, meta_ref[...], x_ref[...])
    hn = _rms(xin, g_pre_ref[...]).astype(BF16)

    blk = rope_blk_ref[0]
    cs = blk[0:1, :] * rope_loc_ref[0] + blk[1:2, :] * rope_loc_ref[1]
    low = lax.broadcasted_iota(jnp.int32, cs.shape, 1) < QK_ROPE

    def rope(t):
        prod = t * cs
        return prod + pltpu.roll(prod, QK_ROPE, axis=1)

    z_ref[...] = jnp.dot(hn, wz_ref[...], preferred_element_type=F32)

    rows_b = xin.shape[0]
    stage_ref[SUBLANES:, :] = jnp.dot(hn, wxbc_ref[...], preferred_element_type=F32)

    small = jnp.dot(hn, wsm_ref[...], preferred_element_type=F32)
    dt_ref[...] = small[:, LANES:]
    k_rope = jnp.where(low, rope(small[:, :LANES]), 0.0).astype(BF16)

    c_q = jnp.dot(hn, wqa_ref[...], preferred_element_type=F32)
    cqn = _rms(c_q, gq_ref[...]).astype(BF16)
    c_kv = jnp.dot(hn, wkva_ref[...], preferred_element_type=F32)
    ckvn = _rms(c_kv, gkv_ref[...]).astype(BF16)

    for h in range(ATT_HEADS):
        cols = slice(h * QK_PAD, (h + 1) * QK_PAD)
        qh = jnp.dot(cqn, wq_ref[:, cols], preferred_element_type=F32)
        q_ref[h, :, :QK_NOPE] = (qh[:, :QK_NOPE] * ATT_SCALE_LOG2).astype(BF16)
        q_ref[h, :, QK_NOPE:] = (rope(qh[:, QK_NOPE:]) * ATT_SCALE_LOG2).astype(BF16)
        kvh = jnp.dot(ckvn, wkv_ref[:, cols], preferred_element_type=F32)
        k_ref[h, :, :QK_NOPE] = kvh[:, :QK_NOPE].astype(BF16)
        k_ref[h, :, QK_NOPE:] = k_rope
        v_ref[h] = kvh[:, QK_NOPE:].astype(BF16)

    conv = convb_ref[...]
    for kk in range(CONV_K):
        off = SUBLANES - (CONV_K - 1) + kk
        conv = conv + convw_ref[kk:kk + 1, :] * stage_ref[off:off + rows_b, :]
    stage_ref[0:SUBLANES, :] = stage_ref[rows_b:rows_b + SUBLANES, :]
    xbc_ref[...] = conv * jax.nn.sigmoid(conv)


def _input_projection(x2d, meta_pad, rope_blk, rope_loc, g_pre, wqa, wkva, wz, wxbc, wsm, gq, gkv, wq, wkv,
                      conv_w, conv_b):
    seq = x2d.shape[0]
    n_blocks = seq // ROW_BLOCK + 1
    rows_p = seq + PAD
    real = lambda i: (jnp.maximum(i - 1, 0), 0)
    real3 = lambda i: (0, jnp.maximum(i - 1, 0), 0)
    padded = lambda i: (i, 0)
    padded3 = lambda i: (0, i, 0)
    weights = (g_pre, wqa, wkva, wz, wxbc, wsm, gq, gkv, wq, wkv, conv_w, conv_b)
    return pl.pallas_call(
        _proj_kernel,
        grid=(n_blocks,),
        in_specs=[_const_spec(meta_pad.shape),
                  pl.BlockSpec((ROW_BLOCK, D_MODEL), real),
                  pl.BlockSpec((1, SUBLANES, LANES), lambda i: (i, 0, 0)),
                  _const_spec(rope_loc.shape)] + [_const_spec(w.shape) for w in weights],
        out_specs=[pl.BlockSpec((ATT_HEADS, ROW_BLOCK, QK_PAD), real3),
                   pl.BlockSpec((ATT_HEADS, ROW_BLOCK, QK_PAD), padded3),
                   pl.BlockSpec((ATT_HEADS, ROW_BLOCK, V_HEAD), padded3),
                   pl.BlockSpec((ROW_BLOCK, SSM_WIDTH), real),
                   pl.BlockSpec((ROW_BLOCK, CONV_DIM), padded),
                   pl.BlockSpec((ROW_BLOCK, LANES), padded)],
        out_shape=[jax.ShapeDtypeStruct((ATT_HEADS, seq, QK_PAD), BF16),
                   jax.ShapeDtypeStruct((ATT_HEADS, rows_p, QK_PAD), BF16),
                   jax.ShapeDtypeStruct((ATT_HEADS, rows_p, V_HEAD), BF16),
                   jax.ShapeDtypeStruct((seq, SSM_WIDTH), F32),
                   jax.ShapeDtypeStruct((rows_p, CONV_DIM), F32),
                   jax.ShapeDtypeStruct((rows_p, LANES), F32)],
        scratch_shapes=[pltpu.VMEM((SUBLANES + ROW_BLOCK, CONV_DIM), F32)],
        compiler_params=pltpu.CompilerParams(dimension_semantics=("arbitrary",),
                                             vmem_limit_bytes=VMEM_LIMIT_BYTES),
        name="input_projection",
    )(meta_pad, x2d, rope_blk, rope_loc, *weights)


def _dot_nt(a, b):
    return lax.dot_general(a, b, (((1,), (1,)), ((), ())), preferred_element_type=F32)


def _attn_kernel(q_ref, k_ref, v_ref, o_ref, sd_ref, pd_ref, s_ref, p_ref, m_ref, l_ref, acc_ref):
    n_tiles = ATT_K // LANES
    assert ATT_Q == ATT_K, "one diagonal chunk per query block"
    assert ATT_UNROLL % 2 == 0 and PAD >= LANES

    def rows_at(ref, j):
        start = pl.multiple_of(PAD + j * ATT_K, math.gcd(PAD, ATT_K))
        return ref[0, pl.ds(start, ATT_K), :]

    def query_block(i, qrows):
        def scores(buf, j):
            s_ref[buf] = _dot_nt(q_ref[0, qrows, :], rows_at(k_ref, j))

        def score_tiles(buf):
            return [s_ref[buf, :, t * LANES:(t + 1) * LANES] for t in range(n_tiles)]

        def consume(buf, j):
            tiles = score_tiles(buf)
            mx = tiles[0]
            for t in range(1, n_tiles):
                mx = jnp.maximum(mx, tiles[t])
            m_old = m_ref[...]
            m_new = jnp.maximum(m_old, jnp.max(mx, axis=1, keepdims=True))
            m_ref[...] = m_new
            alpha = jnp.exp2(m_old - m_new)
            tiles = score_tiles(buf)
            lsum = None
            for t in range(n_tiles):
                p = jnp.exp2(tiles[t] - m_new)
                lsum = p if lsum is None else lsum + p
                p_ref[:, t * LANES:(t + 1) * LANES] = p.astype(BF16)
            l_ref[...] = alpha * l_ref[...] + lsum
            acc_ref[...] = alpha * acc_ref[...] + jnp.dot(p_ref[...], rows_at(v_ref, j),
                                                          preferred_element_type=F32)

        def step(cur, nxt, j):
            scores(nxt, j + 1)
            consume(cur, j)

        def first_chunk():
            rows = lax.broadcasted_iota(jnp.int32, (ATT_Q, LANES), 0)
            cols = lax.broadcasted_iota(jnp.int32, (ATT_Q, LANES), 1)

            def tiles():
                out = [jnp.where(cols >= LANES - N_META, sd_ref[:, :LANES], -jnp.inf)]
                for t in range(n_tiles):
                    out.append(jnp.where(cols + t * LANES <= rows,
                                         sd_ref[:, (t + 1) * LANES:(t + 2) * LANES], -jnp.inf))
                return out

            ts = tiles()
            mx = ts[0]
            for t in range(1, n_tiles + 1):
                mx = jnp.maximum(mx, ts[t])
            m_new = jnp.broadcast_to(jnp.max(mx, axis=1, keepdims=True), (ATT_Q, LANES))
            m_ref[...] = m_new
            ts = tiles()
            lsum = None
            for t in range(n_tiles + 1):
                p = jnp.exp2(ts[t] - m_new)
                lsum = p if lsum is None else lsum + p
                pd_ref[:, t * LANES:(t + 1) * LANES] = p.astype(BF16)
            l_ref[...] = lsum
            acc_ref[...] = (
                jnp.dot(pd_ref[:, :LANES], v_ref[0, PAD - LANES:PAD, :], preferred_element_type=F32)
                + jnp.dot(pd_ref[:, LANES:], rows_at(v_ref, i), preferred_element_type=F32))

        sd_ref[:, :LANES] = _dot_nt(q_ref[0, qrows, :], k_ref[0, PAD - LANES:PAD, :])
        sd_ref[:, LANES:] = _dot_nt(q_ref[0, qrows, :], rows_at(k_ref, i))
        scores(0, 0)
        first_chunk()

        def quad(t, carry):
            for u in range(ATT_UNROLL):
                step(u % 2, (u + 1) % 2, ATT_UNROLL * t + u)
            return carry

        lax.fori_loop(0, i // ATT_UNROLL, quad, 0)
        rem = i % ATT_UNROLL
        base = i - rem
        for u in range(ATT_UNROLL - 1):
            @pl.when(rem > u)
            def _():
                step(u % 2, (u + 1) % 2, base + u)

        o_ref[qrows, :] = (acc_ref[...] / jnp.sum(l_ref[...], axis=1, keepdims=True)).astype(o_ref.dtype)

    for sub in range(ATT_SUB):
        query_block(ATT_SUB * pl.program_id(1) + sub, slice(sub * ATT_Q, (sub + 1) * ATT_Q))


def _attention(q, k, v):
    heads, seq, _ = q.shape
    rows_p = k.shape[1]
    step_rows = ATT_SUB * ATT_Q
    return pl.pallas_call(
        _attn_kernel,
        grid=(heads, seq // step_rows),
        in_specs=[pl.BlockSpec((1, step_rows, QK_PAD), lambda h, i: (h, i, 0)),
                  pl.BlockSpec((1, rows_p, QK_PAD), lambda h, i: (h, 0, 0)),
                  pl.BlockSpec((1, rows_p, V_HEAD), lambda h, i: (h, 0, 0))],
        out_specs=pl.BlockSpec((step_rows, V_HEAD), lambda h, i: (i, h)),
        out_shape=jax.ShapeDtypeStruct((seq, ATT_WIDTH), BF16),
        scratch_shapes=[pltpu.VMEM((ATT_Q, LANES + ATT_K), F32),
                        pltpu.VMEM((ATT_Q, LANES + ATT_K), BF16),
                        pltpu.VMEM((2, ATT_Q, ATT_K), F32),
                        pltpu.VMEM((ATT_Q, ATT_K), BF16),
                        pltpu.VMEM((ATT_Q, LANES), F32),
                        pltpu.VMEM((ATT_Q, LANES), F32),
                        pltpu.VMEM((ATT_Q, V_HEAD), F32)],
        compiler_params=pltpu.CompilerParams(dimension_semantics=("arbitrary", "arbitrary"),
                                             vmem_limit_bytes=VMEM_LIMIT_BYTES),
        name="causal_attention",
    )(q, k, v)


def _split_dot(v, mat, terms):
    out = None
    rem = v
    for t in range(terms):
        piece = rem.astype(BF16)
        part = jnp.dot(piece, mat, preferred_element_type=F32)
        out = part if out is None else out + part
        if t + 1 < terms:
            rem = rem - piece.astype(F32)
    return out


def _ssd_kernel(xc_ref, dt_ref, dtb_ref, alog_ref, dskip_ref, tril_ref, expand_ref, y_ref, state_ref):
    i = pl.program_id(0)
    rows_b = xc_ref.shape[0]

    @pl.when(i == 0)
    def _():
        state_ref[...] = jnp.zeros_like(state_ref)

    xc = xc_ref

    lane = lax.broadcasted_iota(jnp.int32, (SSD_CHUNK, LANES), 1)
    row = lax.broadcasted_iota(jnp.int32, (SSD_CHUNK, LANES), 0)
    causal = lax.broadcasted_iota(jnp.int32, (SSD_CHUNK, SSD_CHUNK), 0) >= lax.broadcasted_iota(
        jnp.int32, (SSD_CHUNK, SSD_CHUNK), 1)
    low_half = lane < SSM_HEAD_DIM
    a_row = -jnp.exp(alog_ref[...])
    tril = tril_ref[...]
    expand = expand_ref[...]
    heads_per_group = SSM_HEADS // SSM_GROUPS

    for c in range(rows_b // SSD_CHUNK):
        r0 = c * SSD_CHUNK
        xs = xc[r0:r0 + SSD_CHUNK, :SSM_WIDTH]
        valid = (lane < SSM_HEADS) & (row + (i * rows_b + r0) >= PAD - N_META)
        dt = jnp.where(valid, jax.nn.softplus(dt_ref[r0:r0 + SSD_CHUNK, :] + dtb_ref[...]), 0.0)
        da = dt * a_row
        acum = _cumsum_rows(tril, da)
        acum_t = acum.T
        dt_t = dt.T
        a_last = acum[SSD_CHUNK - 1:SSD_CHUNK, :]
        e_acum = _split_dot(jnp.exp(acum), expand, 2)
        w_end = _split_dot(jnp.exp(a_last - acum) * dt, expand, 2)
        c_decay = _split_dot(jnp.broadcast_to(jnp.exp(a_last), (SUBLANES, LANES)), expand, 2)[0:1, :]

        y_diag = []
        y_off = []
        for g in range(SSM_GROUPS):
            b_g = xc[r0:r0 + SSD_CHUNK, SSM_WIDTH + g * SSM_STATE:SSM_WIDTH + (g + 1) * SSM_STATE]
            c_g = xc[r0:r0 + SSD_CHUNK, SSM_WIDTH + (SSM_GROUPS + g) * SSM_STATE:
                     SSM_WIDTH + (SSM_GROUPS + g + 1) * SSM_STATE].astype(BF16)
            b_t = b_g.T.astype(BF16)
            cb = jnp.dot(c_g, b_t, preferred_element_type=F32)
            gcols = slice(g * GROUP_WIDTH, (g + 1) * GROUP_WIDTH)
            for jp in range(heads_per_group // 2):
                ws = []
                for r in (g * heads_per_group + 2 * jp, g * heads_per_group + 2 * jp + 1):
                    seg = acum[:, r:r + 1] - acum_t[r:r + 1, :]
                    decay = jnp.where(causal, jnp.exp(seg), 0.0)
                    ws.append((cb * decay * dt_t[r:r + 1, :]).astype(BF16))
                col = g * GROUP_WIDTH + jp * LANES
                x_pair = xs[:, col:col + LANES]
                x_cat = jnp.concatenate([jnp.where(low_half, x_pair, 0.0), jnp.where(low_half, 0.0, x_pair)],
                                        axis=0).astype(BF16)
                y_diag.append(jnp.dot(jnp.concatenate(ws, axis=1), x_cat, preferred_element_type=F32))
            h_prev = state_ref[g]
            y_off.append(jnp.dot(c_g, h_prev.astype(BF16), preferred_element_type=F32))
            x_end = (xs[:, gcols] * w_end[:, gcols]).astype(BF16)
            state_ref[g] = c_decay[:, gcols] * h_prev + jnp.dot(b_t, x_end, preferred_element_type=F32)
        y = (jnp.concatenate(y_diag, axis=1) + jnp.concatenate(y_off, axis=1) * e_acum
             + xs * dskip_ref[...])
        y_ref[r0:r0 + SSD_CHUNK, :] = y


def _cumsum_rows(tril, v):
    out = None
    rem = v
    for t in range(3):
        piece = rem.astype(BF16)
        part = jnp.dot(tril, piece, preferred_element_type=F32)
        out = part if out is None else out + part
        if t < 2:
            rem = rem - piece.astype(F32)
    return out


def _ssd_scan(xc, dt_raw, dt_bias, a_log, d_skip_x, tril, expand, seq):
    n_blocks = seq // ROW_BLOCK + 1
    consts = (dt_bias, a_log, d_skip_x, tril, expand)
    return pl.pallas_call(
        _ssd_kernel,
        grid=(n_blocks,),
        in_specs=[pl.BlockSpec((ROW_BLOCK, CONV_DIM), lambda i: (i, 0)),
                  pl.BlockSpec((ROW_BLOCK, LANES), lambda i: (i, 0))] + [_const_spec(c.shape) for c in consts],
        out_specs=pl.BlockSpec((ROW_BLOCK, SSM_WIDTH), lambda i: (jnp.maximum(i - 1, 0), 0)),
        out_shape=jax.ShapeDtypeStruct((seq, SSM_WIDTH), F32),
        scratch_shapes=[pltpu.VMEM((SSM_GROUPS, SSM_STATE, GROUP_WIDTH), F32)],
        compiler_params=pltpu.CompilerParams(dimension_semantics=("arbitrary",),
                                             vmem_limit_bytes=VMEM_LIMIT_BYTES),
        name="ssd_scan",
    )(xc, dt_raw, *consts)


def _out_mlp_kernel(att_ref, y_ref, z_ref, x_ref, g_ssm_ref, wout_ref, g_post_ref, g_mpre_ref,
                    wup_ref, wdown_ref, g_mpost_ref, o_ref):
    sub = att_ref.shape[0] // ROW_SPLIT
    for r in range(ROW_SPLIT):
        rows = slice(r * sub, (r + 1) * sub)
        z = z_ref[rows, :]
        gated = y_ref[rows, :] * (z * jax.nn.sigmoid(z))
        parts = []
        for g in range(SSM_GROUPS):
            gg = gated[:, g * GROUP_WIDTH:(g + 1) * GROUP_WIDTH]
            parts.append(gg * lax.rsqrt(jnp.mean(jnp.square(gg), axis=-1, keepdims=True) + EPS))
        ssm = (jnp.concatenate(parts, axis=1) * g_ssm_ref[...]).astype(BF16)
        mix = (jnp.dot(att_ref[rows, :], wout_ref[:ATT_WIDTH, :], preferred_element_type=F32)
               + jnp.dot(ssm, wout_ref[ATT_WIDTH:, :], preferred_element_type=F32))
        h1 = x_ref[rows, :] + _rms(mix, g_post_ref[...])
        hn = _rms(h1, g_mpre_ref[...]).astype(BF16)
        f = None
        for c in range(D_FF // FF_CHUNK):
            u = jnp.dot(hn, wup_ref[:, c * FF_CHUNK:(c + 1) * FF_CHUNK], preferred_element_type=F32)
            u = jnp.square(jnp.maximum(u, 0.0)).astype(BF16)
            part = jnp.dot(u, wdown_ref[c * FF_CHUNK:(c + 1) * FF_CHUNK, :], preferred_element_type=F32)
            f = part if f is None else f + part
        o_ref[rows, :] = h1 + _rms(f, g_mpost_ref[...])


def _out_mlp(att, y, z, x2d, g_ssm, wout, g_post, g_mpre, wup, wdown, g_mpost):
    seq = x2d.shape[0]
    weights = (g_ssm, wout, g_post, g_mpre, wup, wdown, g_mpost)
    row = lambda i: (i, 0)
    return pl.pallas_call(
        _out_mlp_kernel,
        grid=(seq // MLP_BLOCK,),
        in_specs=[pl.BlockSpec((MLP_BLOCK, ATT_WIDTH), row),
                  pl.BlockSpec((MLP_BLOCK, SSM_WIDTH), row),
                  pl.BlockSpec((MLP_BLOCK, SSM_WIDTH), row),
                  pl.BlockSpec((MLP_BLOCK, D_MODEL), row)] + [_const_spec(w.shape) for w in weights],
        out_specs=pl.BlockSpec((MLP_BLOCK, D_MODEL), row),
        out_shape=jax.ShapeDtypeStruct((seq, D_MODEL), F32),
        compiler_params=pltpu.CompilerParams(dimension_semantics=("arbitrary",),
                                             vmem_limit_bytes=VMEM_LIMIT_BYTES),
        name="out_proj_mlp",
    )(att, y, z, x2d, *weights)


def _rot_cols(w):
    half = QK_ROPE // 2
    return jnp.concatenate([-w[..., half:], w[..., :half]], axis=-1)


def _lane_pad(v, width=LANES):
    return jnp.pad(v, [(0, 0)] * (v.ndim - 1) + [(0, width - v.shape[-1])])


def kernel(x, meta_tokens, norm_mix_pre, w_in, q_a_norm, w_q_up, kv_a_norm, w_kv_up, conv_w, conv_b, dt_bias,
           a_log, d_skip, ssm_norm, w_out, norm_mix_post, norm_mlp_pre, w_mlp_up, w_mlp_down, norm_mlp_post):
    bsz, seq, _ = x.shape
    assert bsz == 1 and norm_mix_pre.shape[0] == 1, "single sequence, single layer"
    assert seq % (ATT_SUB * ATT_Q) == 0 and seq % ROW_BLOCK == 0 and seq % MLP_BLOCK == 0
    assert PAD == ROW_BLOCK, "the front pad is exactly the first block of the projection / SSD grids"
    x2d = x[0]
    row = lambda v: v.reshape(1, -1)

    o_ckv, o_kr, o_z, o_xbc, o_dt = Q_LORA, Q_LORA + KV_LORA, Q_LORA + KV_LORA + QK_ROPE, \
        Q_LORA + KV_LORA + QK_ROPE + SSM_WIDTH, Q_LORA + KV_LORA + QK_ROPE + SSM_WIDTH + CONV_DIM
    w_in0 = w_in[0]
    w_kr = w_in0[:, o_kr:o_z]
    wsm = jnp.concatenate([w_kr, _rot_cols(w_kr), _lane_pad(w_in0[:, o_dt:])], axis=1)
    wq3 = w_q_up[0].reshape(Q_LORA, ATT_HEADS, QK_DIM)
    wq = jnp.concatenate([wq3, _rot_cols(wq3[..., QK_NOPE:])], axis=-1).reshape(Q_LORA, ATT_HEADS * QK_PAD)

    inv_freq = ROPE_THETA ** (-jnp.arange(0, QK_ROPE, 2, dtype=F32) / QK_ROPE)
    freq = jnp.tile(inv_freq, 4)
    low = jnp.arange(LANES) < QK_ROPE
    base = (jnp.arange(seq // ROW_BLOCK + 1) * ROW_BLOCK - (PAD - N_META)).astype(F32)
    ang_b = base[:, None] * freq[None, :]
    rope_blk = jnp.stack([jnp.cos(ang_b), jnp.where(low, -jnp.sin(ang_b), jnp.sin(ang_b))], axis=1)
    rope_blk = jnp.pad(rope_blk, ((0, 0), (0, SUBLANES - 2), (0, 0)))
    ang_l = jnp.arange(ROW_BLOCK, dtype=F32)[:, None] * freq[None, :]
    rope_loc = jnp.stack([jnp.where(low, jnp.cos(ang_l), jnp.sin(ang_l)),
                          jnp.where(low, jnp.sin(ang_l), jnp.cos(ang_l))])

    meta_pad = jnp.concatenate([jnp.zeros((PAD - N_META, D_MODEL), F32), meta_tokens.astype(F32)], axis=0)

    q, k, v, z, xc, dt_raw = _input_projection(
        x2d, meta_pad, rope_blk, rope_loc, row(norm_mix_pre[0]),
        w_in0[:, :o_ckv].astype(BF16), w_in0[:, o_ckv:o_kr].astype(BF16), w_in0[:, o_z:o_xbc].astype(BF16),
        w_in0[:, o_xbc:o_dt].astype(BF16), wsm.astype(BF16),
        row(q_a_norm[0]), row(kv_a_norm[0]), wq.astype(BF16), w_kv_up[0].astype(BF16),
        conv_w[0], row(conv_b[0]))

    att = _attention(q, k, v)

    tril = jnp.tril(jnp.ones((SSD_CHUNK, SSD_CHUNK), BF16))
    expand = (jnp.arange(LANES)[:, None] == jnp.arange(SSM_WIDTH)[None, :] // SSM_HEAD_DIM).astype(BF16)
    y = _ssd_scan(xc, dt_raw, _lane_pad(row(dt_bias[0])), _lane_pad(row(a_log[0])),
                  row(jnp.repeat(d_skip[0], SSM_HEAD_DIM)), tril, expand, seq)

    out = _out_mlp(att, y, z, x2d, row(ssm_norm[0]), w_out[0].astype(BF16), row(norm_mix_post[0]),
                   row(norm_mlp_pre[0]), w_mlp_up[0].astype(BF16), w_mlp_down[0].astype(BF16),
                   row(norm_mlp_post[0]))
    return out[None]
```

```python
import functools
import math

import jax
import jax.numpy as jnp
import numpy as np
from jax import lax
from jax.experimental import pallas as pl
from jax.experimental.pallas import tpu as pltpu

D_MODEL = 1024
N_META = 16
EPS = 1e-6
ATT_HEADS = 8
Q_LORA = 384
KV_LORA = 256
QK_NOPE = 128
QK_ROPE = 64
V_HEAD = 128
QK_DIM = QK_NOPE + QK_ROPE
ROPE_THETA = 10000.0
ATT_WIDTH = ATT_HEADS * V_HEAD
SSM_HEADS = 16
SSM_HEAD_DIM = 64
SSM_WIDTH = SSM_HEADS * SSM_HEAD_DIM
SSM_GROUPS = 2
SSM_STATE = 128
GROUP_WIDTH = SSM_WIDTH // SSM_GROUPS
CONV_K = 4
CONV_DIM = SSM_WIDTH + 2 * SSM_GROUPS * SSM_STATE
D_FF = 4 * D_MODEL

LANES = 128
SUBLANES = 8
VMEM_LIMIT_BYTES = 56 * 1024 * 1024

QK_PAD = 2 * LANES
PAD = 256
ROW_BLOCK = 256
MLP_BLOCK = 512
ROW_SPLIT = 1
ATT_Q = 512
ATT_K = 512
ATT_SUB = 1
ATT_UNROLL = 8
SSD_CHUNK = 128
FF_CHUNK = 1024
ATT_SCALE_LOG2 = (QK_DIM ** -0.5) * math.log2(math.e)

BF16 = jnp.bfloat16
F32 = jnp.float32


def _const_spec(shape):
    nd = len(shape)
    return pl.BlockSpec(shape, lambda *_: (0,) * nd, pipeline_mode=pl.Buffered(1))


def _rms(x, gain):
    return x * lax.rsqrt(jnp.mean(jnp.square(x), axis=-1, keepdims=True) + EPS) * gain


def _proj_kernel(meta_ref, x_ref, rope_blk_ref, rope_loc_ref, g_pre_ref, wqa_ref, wkva_ref, wz_ref, wxbc_ref,
                 wsm_ref, gq_ref, gkv_ref, wq_ref, wkv_ref, convw_ref, convb_ref,
                 q_ref, k_ref, v_ref, z_ref, xbc_ref, dt_ref, stage_ref):
    i = pl.program_id(0)
    rows_b = x_ref.shape[0]

    @pl.when(i == 0)
    def _():
        stage_ref[...] = jnp.zeros_like(stage_ref)

    conv = convb_ref[...]
    for kk in range(CONV_K):
        off = SUBLANES - (CONV_K - 1) + kk
        conv = conv + convw_ref[kk:kk + 1, :] * stage_ref[off:off + rows_b, :]
    xbc_ref[...] = conv * jax.nn.sigmoid(conv)
    stage_ref[0:SUBLANES, :] = stage_ref[rows_b:rows_b + SUBLANES, :]

    xin = jnp.where(i == 0, meta_ref[...], x_ref[...])
    hn = _rms(xin, g_pre_ref[...]).astype(BF16)

    blk = rope_blk_ref[0]
    cs = blk[0:1, :] * rope_loc_ref[0] + blk[1:2, :] * rope_loc_ref[1]
    low = lax.broadcasted_iota(jnp.int32, cs.shape, 1) < QK_ROPE

    def rope(t):
        prod = t * cs
        return prod + pltpu.roll(prod, QK_ROPE, axis=1)

    z_ref[...] = jnp.dot(hn, wz_ref[...], preferred_element_type=F32)
    stage_ref[SUBLANES:, :] = jnp.dot(hn, wxbc_ref[...], preferred_element_type=F32)

    small = jnp.dot(hn, wsm_ref[...], preferred_element_type=F32)
    dt_ref[...] = small[:, LANES:]
    k_rope = jnp.where(low, rope(small[:, :LANES]), 0.0).astype(BF16)

    c_q = jnp.dot(hn, wqa_ref[...], preferred_element_type=F32)
    cqn = _rms(c_q, gq_ref[...]).astype(BF16)
    c_kv = jnp.dot(hn, wkva_ref[...], preferred_element_type=F32)
    ckvn = _rms(c_kv, gkv_ref[...]).astype(BF16)

    for h in range(ATT_HEADS):
        cols = slice(h * QK_PAD, (h + 1) * QK_PAD)
        qh = jnp.dot(cqn, wq_ref[:, cols], preferred_element_type=F32)
        q_ref[h, :, :QK_NOPE] = (qh[:, :QK_NOPE] * ATT_SCALE_LOG2).astype(BF16)
        q_ref[h, :, QK_NOPE:] = (rope(qh[:, QK_NOPE:]) * ATT_SCALE_LOG2).astype(BF16)
        kvh = jnp.dot(ckvn, wkv_ref[:, cols], preferred_element_type=F32)
        k_ref[h, :, :QK_NOPE] = kvh[:, :QK_NOPE].astype(BF16)
        k_ref[h, :, QK_NOPE:] = k_rope
        v_ref[h] = kvh[:, QK_NOPE:].astype(BF16)


def _input_projection(x2d, meta_pad, rope_blk, rope_loc, g_pre, wqa, wkva, wz, wxbc, wsm, gq, gkv, wq, wkv,
                      conv_w, conv_b):
    seq = x2d.shape[0]
    n_blocks = seq // ROW_BLOCK + 1
    rows_p = seq + PAD
    pblk = lambda i: jnp.minimum(i, n_blocks - 1)
    rblk = lambda i: jnp.maximum(pblk(i) - 1, 0)
    real = lambda i: (rblk(i), 0)
    real3 = lambda i: (0, rblk(i), 0)
    padded = lambda i: (pblk(i), 0)
    padded3 = lambda i: (0, pblk(i), 0)
    lagged = lambda i: (jnp.maximum(i - 1, 0), 0)
    weights = (g_pre, wqa, wkva, wz, wxbc, wsm, gq, gkv, wq, wkv, conv_w, conv_b)
    return pl.pallas_call(
        _proj_kernel,
        grid=(n_blocks + 1,),
        in_specs=[_const_spec(meta_pad.shape),
                  pl.BlockSpec((ROW_BLOCK, D_MODEL), real),
                  pl.BlockSpec((1, SUBLANES, LANES), lambda i: (pblk(i), 0, 0)),
                  _const_spec(rope_loc.shape)] + [_const_spec(w.shape) for w in weights],
        out_specs=[pl.BlockSpec((ATT_HEADS, ROW_BLOCK, QK_PAD), real3),
                   pl.BlockSpec((ATT_HEADS, ROW_BLOCK, QK_PAD), padded3),
                   pl.BlockSpec((ATT_HEADS, ROW_BLOCK, V_HEAD), padded3),
                   pl.BlockSpec((ROW_BLOCK, SSM_WIDTH), real),
                   pl.BlockSpec((ROW_BLOCK, CONV_DIM), lagged),
                   pl.BlockSpec((ROW_BLOCK, LANES), padded)],
        out_shape=[jax.ShapeDtypeStruct((ATT_HEADS, seq, QK_PAD), BF16),
                   jax.ShapeDtypeStruct((ATT_HEADS, rows_p, QK_PAD), BF16),
                   jax.ShapeDtypeStruct((ATT_HEADS, rows_p, V_HEAD), BF16),
                   jax.ShapeDtypeStruct((seq, SSM_WIDTH), F32),
                   jax.ShapeDtypeStruct((rows_p, CONV_DIM), F32),
                   jax.ShapeDtypeStruct((rows_p, LANES), F32)],
        scratch_shapes=[pltpu.VMEM((SUBLANES + ROW_BLOCK, CONV_DIM), F32)],
        compiler_params=pltpu.CompilerParams(dimension_semantics=("arbitrary",),
                                             vmem_limit_bytes=VMEM_LIMIT_BYTES),
        name="input_projection",
    )(meta_pad, x2d, rope_blk, rope_loc, *weights)


def _dot_nt(a, b):
    return lax.dot_general(a, b, (((1,), (1,)), ((), ())), preferred_element_type=F32)


def _attn_kernel(q_ref, k_ref, v_ref, o_ref, sd_ref, pd_ref, s_ref, p_ref, m_ref, l_ref, acc_ref):
    n_tiles = ATT_K // LANES
    assert ATT_Q == ATT_K, "one diagonal chunk per query block"
    assert ATT_UNROLL % 2 == 0 and PAD >= LANES

    def rows_at(ref, j):
        start = pl.multiple_of(PAD + j * ATT_K, math.gcd(PAD, ATT_K))
        return ref[0, pl.ds(start, ATT_K), :]

    def query_block(i, qrows):
        def scores(buf, j):
            s_ref[buf] = _dot_nt(q_ref[0, qrows, :], rows_at(k_ref, j))

        def score_tiles(buf):
            return [s_ref[buf, :, t * LANES:(t + 1) * LANES] for t in range(n_tiles)]

        def consume(buf, j):
            tiles = score_tiles(buf)
            mx = tiles[0]
            for t in range(1, n_tiles):
                mx = jnp.maximum(mx, tiles[t])
            m_old = m_ref[...]
            m_new = jnp.maximum(m_old, jnp.max(mx, axis=1, keepdims=True))
            m_ref[...] = m_new
            alpha = jnp.exp2(m_old - m_new)
            tiles = score_tiles(buf)
            lsum = None
            for t in range(n_tiles):
                p = jnp.exp2(tiles[t] - m_new)
                lsum = p if lsum is None else lsum + p
                p_ref[:, t * LANES:(t + 1) * LANES] = p.astype(BF16)
            l_ref[...] = alpha * l_ref[...] + lsum
            acc_ref[...] = alpha * acc_ref[...] + jnp.dot(p_ref[...], rows_at(v_ref, j),
                                                          preferred_element_type=F32)

        def step(cur, nxt, j):
            scores(nxt, j + 1)
            consume(cur, j)

        def first_chunk():
            rows = lax.broadcasted_iota(jnp.int32, (ATT_Q, LANES), 0)
            cols = lax.broadcasted_iota(jnp.int32, (ATT_Q, LANES), 1)

            def tiles():
                out = [jnp.where(cols >= LANES - N_META, sd_ref[:, :LANES], -jnp.inf)]
                for t in range(n_tiles):
                    out.append(jnp.where(cols + t * LANES <= rows,
                                         sd_ref[:, (t + 1) * LANES:(t + 2) * LANES], -jnp.inf))
                return out

            ts = tiles()
            mx = ts[0]
            for t in range(1, n_tiles + 1):
                mx = jnp.maximum(mx, ts[t])
            m_new = jnp.broadcast_to(jnp.max(mx, axis=1, keepdims=True), (ATT_Q, LANES))
            m_ref[...] = m_new
            ts = tiles()
            lsum = None
            for t in range(n_tiles + 1):
                p = jnp.exp2(ts[t] - m_new)
                lsum = p if lsum is None else lsum + p
                pd_ref[:, t * LANES:(t + 1) * LANES] = p.astype(BF16)
            l_ref[...] = lsum
            acc_ref[...] = (
                jnp.dot(pd_ref[:, :LANES], v_ref[0, PAD - LANES:PAD, :], preferred_element_type=F32)
                + jnp.dot(pd_ref[:, LANES:], rows_at(v_ref, i), preferred_element_type=F32))

        sd_ref[:, :LANES] = _dot_nt(q_ref[0, qrows, :], k_ref[0, PAD - LANES:PAD, :])
        sd_ref[:, LANES:] = _dot_nt(q_ref[0, qrows, :], rows_at(k_ref, i))
        scores(0, 0)
        first_chunk()

        def run(first, count):
            for u in range(count):
                step(u % 2, (u + 1) % 2, first + u)

        def trip(t, carry):
            run(ATT_UNROLL * t, ATT_UNROLL)
            return carry

        lax.fori_loop(0, i // ATT_UNROLL, trip, 0)

        done = i - i % ATT_UNROLL
        size = ATT_UNROLL // 2
        while size >= 2:
            @pl.when((i - done) >= size)
            def _():
                run(done, size)
            done = done + jnp.where((i - done) >= size, size, 0)
            size //= 2

        @pl.when(i - done >= 1)
        def _():
            run(done, 1)

        o_ref[qrows, :] = (acc_ref[...] / jnp.sum(l_ref[...], axis=1, keepdims=True)).astype(o_ref.dtype)

    for sub in range(ATT_SUB):
        query_block(ATT_SUB * pl.program_id(1) + sub, slice(sub * ATT_Q, (sub + 1) * ATT_Q))


def _attention(q, k, v):
    heads, seq, _ = q.shape
    rows_p = k.shape[1]
    step_rows = ATT_SUB * ATT_Q
    return pl.pallas_call(
        _attn_kernel,
        grid=(heads, seq // step_rows),
        in_specs=[pl.BlockSpec((1, step_rows, QK_PAD), lambda h, i: (h, i, 0)),
                  pl.BlockSpec((1, rows_p, QK_PAD), lambda h, i: (h, 0, 0)),
                  pl.BlockSpec((1, rows_p, V_HEAD), lambda h, i: (h, 0, 0))],
        out_specs=pl.BlockSpec((step_rows, V_HEAD), lambda h, i: (i, h)),
        out_shape=jax.ShapeDtypeStruct((seq, ATT_WIDTH), BF16),
        scratch_shapes=[pltpu.VMEM((ATT_Q, LANES + ATT_K), F32),
                        pltpu.VMEM((ATT_Q, LANES + ATT_K), BF16),
                        pltpu.VMEM((2, ATT_Q, ATT_K), F32),
                        pltpu.VMEM((ATT_Q, ATT_K), BF16),
                        pltpu.VMEM((ATT_Q, LANES), F32),
                        pltpu.VMEM((ATT_Q, LANES), F32),
                        pltpu.VMEM((ATT_Q, V_HEAD), F32)],
        compiler_params=pltpu.CompilerParams(dimension_semantics=("arbitrary", "arbitrary"),
                                             vmem_limit_bytes=VMEM_LIMIT_BYTES),
        name="causal_attention",
    )(q, k, v)


def _split_dot(v, mat, terms):
    out = None
    rem = v
    for t in range(terms):
        piece = rem.astype(BF16)
        part = jnp.dot(piece, mat, preferred_element_type=F32)
        out = part if out is None else out + part
        if t + 1 < terms:
            rem = rem - piece.astype(F32)
    return out


def _ssd_kernel(xc_ref, dt_ref, dtb_ref, alog_ref, dskip_ref, tril_ref, expand_ref, y_ref, state_ref):
    i = pl.program_id(0)
    rows_b = xc_ref.shape[0]

    @pl.when(i == 0)
    def _():
        state_ref[...] = jnp.zeros_like(state_ref)

    xc = xc_ref

    lane = lax.broadcasted_iota(jnp.int32, (SSD_CHUNK, LANES), 1)
    row = lax.broadcasted_iota(jnp.int32, (SSD_CHUNK, LANES), 0)
    causal = lax.broadcasted_iota(jnp.int32, (SSD_CHUNK, SSD_CHUNK), 0) >= lax.broadcasted_iota(
        jnp.int32, (SSD_CHUNK, SSD_CHUNK), 1)
    low_half = lane < SSM_HEAD_DIM
    a_row = -jnp.exp(alog_ref[...])
    tril = tril_ref[...]
    expand = expand_ref[...]
    heads_per_group = SSM_HEADS // SSM_GROUPS

    for c in range(rows_b // SSD_CHUNK):
        r0 = c * SSD_CHUNK
        xs = xc[r0:r0 + SSD_CHUNK, :SSM_WIDTH]
        valid = (lane < SSM_HEADS) & (row + (i * rows_b + r0) >= PAD - N_META)
        dt = jnp.where(valid, jax.nn.softplus(dt_ref[r0:r0 + SSD_CHUNK, :] + dtb_ref[...]), 0.0)
        da = dt * a_row
        acum = _cumsum_rows(tril, da)
        acum_t = acum.T
        dt_t = dt.T
        a_last = acum[SSD_CHUNK - 1:SSD_CHUNK, :]
        e_acum = _split_dot(jnp.exp(acum), expand, 2)
        w_end = _split_dot(jnp.exp(a_last - acum) * dt, expand, 2)
        c_decay = _split_dot(jnp.broadcast_to(jnp.exp(a_last), (SUBLANES, LANES)), expand, 2)[0:1, :]

        y_diag = []
        y_off = []
        for g in range(SSM_GROUPS):
            b_g = xc[r0:r0 + SSD_CHUNK, SSM_WIDTH + g * SSM_STATE:SSM_WIDTH + (g + 1) * SSM_STATE]
            c_g = xc[r0:r0 + SSD_CHUNK, SSM_WIDTH + (SSM_GROUPS + g) * SSM_STATE:
                     SSM_WIDTH + (SSM_GROUPS + g + 1) * SSM_STATE].astype(BF16)
            b_t = b_g.T.astype(BF16)
            cb = jnp.dot(c_g, b_t, preferred_element_type=F32)
            gcols = slice(g * GROUP_WIDTH, (g + 1) * GROUP_WIDTH)
            for jp in range(heads_per_group // 2):
                ws = []
                for r in (g * heads_per_group + 2 * jp, g * heads_per_group + 2 * jp + 1):
                    seg = acum[:, r:r + 1] - acum_t[r:r + 1, :]
                    decay = jnp.where(causal, jnp.exp(seg), 0.0)
                    ws.append((cb * decay * dt_t[r:r + 1, :]).astype(BF16))
                col = g * GROUP_WIDTH + jp * LANES
                x_pair = xs[:, col:col + LANES]
                x_cat = jnp.concatenate([jnp.where(low_half, x_pair, 0.0), jnp.where(low_half, 0.0, x_pair)],
                                        axis=0).astype(BF16)
                y_diag.append(jnp.dot(jnp.concatenate(ws, axis=1), x_cat, preferred_element_type=F32))
            h_prev = state_ref[g]
            y_off.append(jnp.dot(c_g, h_prev.astype(BF16), preferred_element_type=F32))
            x_end = (xs[:, gcols] * w_end[:, gcols]).astype(BF16)
            state_ref[g] = c_decay[:, gcols] * h_prev + jnp.dot(b_t, x_end, preferred_element_type=F32)
        y = (jnp.concatenate(y_diag, axis=1) + jnp.concatenate(y_off, axis=1) * e_acum
             + xs * dskip_ref[...])
        y_ref[r0:r0 + SSD_CHUNK, :] = y


def _cumsum_rows(tril, v):
    out = None
    rem = v
    for t in range(3):
        piece = rem.astype(BF16)
        part = jnp.dot(tril, piece, preferred_element_type=F32)
        out = part if out is None else out + part
        if t < 2:
            rem = rem - piece.astype(F32)
    return out


def _ssd_scan(xc, dt_raw, dt_bias, a_log, d_skip_x, tril, expand, seq):
    n_blocks = seq // ROW_BLOCK + 1
    consts = (dt_bias, a_log, d_skip_x, tril, expand)
    return pl.pallas_call(
        _ssd_kernel,
        grid=(n_blocks,),
        in_specs=[pl.BlockSpec((ROW_BLOCK, CONV_DIM), lambda i: (i, 0)),
                  pl.BlockSpec((ROW_BLOCK, LANES), lambda i: (i, 0))] + [_const_spec(c.shape) for c in consts],
        out_specs=pl.BlockSpec((ROW_BLOCK, SSM_WIDTH), lambda i: (jnp.maximum(i - 1, 0), 0)),
        out_shape=jax.ShapeDtypeStruct((seq, SSM_WIDTH), F32),
        scratch_shapes=[pltpu.VMEM((SSM_GROUPS, SSM_STATE, GROUP_WIDTH), F32)],
        compiler_params=pltpu.CompilerParams(dimension_semantics=("arbitrary",),
                                             vmem_limit_bytes=VMEM_LIMIT_BYTES),
        name="ssd_scan",
    )(xc, dt_raw, *consts)


def _out_mlp_kernel(att_ref, y_ref, z_ref, x_ref, g_ssm_ref, wout_ref, g_post_ref, g_mpre_ref,
                    wup_ref, wdown_ref, g_mpost_ref, o_ref):
    sub = att_ref.shape[0] // ROW_SPLIT
    for r in range(ROW_SPLIT):
        rows = slice(r * sub, (r + 1) * sub)
        z = z_ref[rows, :]
        gated = y_ref[rows, :] * (z * jax.nn.sigmoid(z))
        parts = []
        for g in range(SSM_GROUPS):
            gg = gated[:, g * GROUP_WIDTH:(g + 1) * GROUP_WIDTH]
            parts.append(gg * lax.rsqrt(jnp.mean(jnp.square(gg), axis=-1, keepdims=True) + EPS))
        ssm = (jnp.concatenate(parts, axis=1) * g_ssm_ref[...]).astype(BF16)
        mix = (jnp.dot(att_ref[rows, :], wout_ref[:ATT_WIDTH, :], preferred_element_type=F32)
               + jnp.dot(ssm, wout_ref[ATT_WIDTH:, :], preferred_element_type=F32))
        h1 = x_ref[rows, :] + _rms(mix, g_post_ref[...])
        hn = _rms(h1, g_mpre_ref[...]).astype(BF16)
        f = None
        for c in range(D_FF // FF_CHUNK):
            u = jnp.dot(hn, wup_ref[:, c * FF_CHUNK:(c + 1) * FF_CHUNK], preferred_element_type=F32)
            u = jnp.square(jnp.maximum(u, 0.0)).astype(BF16)
            part = jnp.dot(u, wdown_ref[c * FF_CHUNK:(c + 1) * FF_CHUNK, :], preferred_element_type=F32)
            f = part if f is None else f + part
        o_ref[rows, :] = h1 + _rms(f, g_mpost_ref[...])


def _out_mlp(att, y, z, x2d, g_ssm, wout, g_post, g_mpre, wup, wdown, g_mpost):
    seq = x2d.shape[0]
    weights = (g_ssm, wout, g_post, g_mpre, wup, wdown, g_mpost)
    row = lambda i: (i, 0)
    return pl.pallas_call(
        _out_mlp_kernel,
        grid=(seq // MLP_BLOCK,),
        in_specs=[pl.BlockSpec((MLP_BLOCK, ATT_WIDTH), row),
                  pl.BlockSpec((MLP_BLOCK, SSM_WIDTH), row),
                  pl.BlockSpec((MLP_BLOCK, SSM_WIDTH), row),
                  pl.BlockSpec((MLP_BLOCK, D_MODEL), row)] + [_const_spec(w.shape) for w in weights],
        out_specs=pl.BlockSpec((MLP_BLOCK, D_MODEL), row),
        out_shape=jax.ShapeDtypeStruct((seq, D_MODEL), F32),
        compiler_params=pltpu.CompilerParams(dimension_semantics=("arbitrary",),
                                             vmem_limit_bytes=VMEM_LIMIT_BYTES),
        name="out_proj_mlp",
    )(att, y, z, x2d, *weights)


def _rot_cols(w):
    half = QK_ROPE // 2
    return jnp.concatenate([-w[..., half:], w[..., :half]], axis=-1)


def _lane_pad(v, width=LANES):
    return jnp.pad(v, [(0, 0)] * (v.ndim - 1) + [(0, width - v.shape[-1])])


def kernel(x, meta_tokens, norm_mix_pre, w_in, q_a_norm, w_q_up, kv_a_norm, w_kv_up, conv_w, conv_b, dt_bias,
           a_log, d_skip, ssm_norm, w_out, norm_mix_post, norm_mlp_pre, w_mlp_up, w_mlp_down, norm_mlp_post):
    bsz, seq, _ = x.shape
    assert bsz == 1 and norm_mix_pre.shape[0] == 1, "single sequence, single layer"
    assert seq % (ATT_SUB * ATT_Q) == 0 and seq % ROW_BLOCK == 0 and seq % MLP_BLOCK == 0
    assert PAD == ROW_BLOCK, "the front pad is exactly the first block of the projection / SSD grids"
    x2d = x[0]
    row = lambda v: v.reshape(1, -1)

    o_ckv, o_kr, o_z, o_xbc, o_dt = Q_LORA, Q_LORA + KV_LORA, Q_LORA + KV_LORA + QK_ROPE, \
        Q_LORA + KV_LORA + QK_ROPE + SSM_WIDTH, Q_LORA + KV_LORA + QK_ROPE + SSM_WIDTH + CONV_DIM
    w_in0 = w_in[0]
    w_kr = w_in0[:, o_kr:o_z]
    wsm = jnp.concatenate([w_kr, _rot_cols(w_kr), _lane_pad(w_in0[:, o_dt:])], axis=1)
    wq3 = w_q_up[0].reshape(Q_LORA, ATT_HEADS, QK_DIM)
    wq = jnp.concatenate([wq3, _rot_cols(wq3[..., QK_NOPE:])], axis=-1).reshape(Q_LORA, ATT_HEADS * QK_PAD)

    inv_freq = ROPE_THETA ** (-jnp.arange(0, QK_ROPE, 2, dtype=F32) / QK_ROPE)
    freq = jnp.tile(inv_freq, 4)
    low = jnp.arange(LANES) < QK_ROPE
    base = (jnp.arange(seq // ROW_BLOCK + 1) * ROW_BLOCK - (PAD - N_META)).astype(F32)
    ang_b = base[:, None] * freq[None, :]
    rope_blk = jnp.stack([jnp.cos(ang_b), jnp.where(low, -jnp.sin(ang_b), jnp.sin(ang_b))], axis=1)
    rope_blk = jnp.pad(rope_blk, ((0, 0), (0, SUBLANES - 2), (0, 0)))
    ang_l = jnp.arange(ROW_BLOCK, dtype=F32)[:, None] * freq[None, :]
    rope_loc = jnp.stack([jnp.where(low, jnp.cos(ang_l), jnp.sin(ang_l)),
                          jnp.where(low, jnp.sin(ang_l), jnp.cos(ang_l))])

    meta_pad = jnp.concatenate([jnp.zeros((PAD - N_META, D_MODEL), F32), meta_tokens.astype(F32)], axis=0)

    q, k, v, z, xc, dt_raw = _input_projection(
        x2d, meta_pad, rope_blk, rope_loc, row(norm_mix_pre[0]),
        w_in0[:, :o_ckv].astype(BF16), w_in0[:, o_ckv:o_kr].astype(BF16), w_in0[:, o_z:o_xbc].astype(BF16),
        w_in0[:, o_xbc:o_dt].astype(BF16), wsm.astype(BF16),
        row(q_a_norm[0]), row(kv_a_norm[0]), wq.astype(BF16), w_kv_up[0].astype(BF16),
        conv_w[0], row(conv_b[0]))

    att = _attention(q, k, v)

    tril = jnp.tril(jnp.ones((SSD_CHUNK, SSD_CHUNK), BF16))
    expand = (jnp.arange(LANES)[:, None] == jnp.arange(SSM_WIDTH)[None, :] // SSM_HEAD_DIM).astype(BF16)
    y = _ssd_scan(xc, dt_raw, _lane_pad(row(dt_bias[0])), _lane_pad(row(a_log[0])),
                  row(jnp.repeat(d_skip[0], SSM_HEAD_DIM)), tril, expand, seq)

    out = _out_mlp(att, y, z, x2d, row(ssm_norm[0]), w_out[0].astype(BF16), row(norm_mix_post[0]),
                   row(norm_mlp_pre[0]), w_mlp_up[0].astype(BF16), w_mlp_down[0].astype(BF16),
                   row(norm_mlp_post[0]))
    return out[None]
```

```python
import functools
import math

import jax
import jax.numpy as jnp
import numpy as np
from jax import lax
from jax.experimental import pallas as pl
from jax.experimental.pallas import tpu as pltpu

D_MODEL = 1024
N_META = 16
EPS = 1e-6
ATT_HEADS = 8
Q_LORA = 384
KV_LORA = 256
QK_NOPE = 128
QK_ROPE = 64
V_HEAD = 128
QK_DIM = QK_NOPE + QK_ROPE
ROPE_THETA = 10000.0
ATT_WIDTH = ATT_HEADS * V_HEAD
SSM_HEADS = 16
SSM_HEAD_DIM = 64
SSM_WIDTH = SSM_HEADS * SSM_HEAD_DIM
SSM_GROUPS = 2
SSM_STATE = 128
GROUP_WIDTH = SSM_WIDTH // SSM_GROUPS
CONV_K = 4
CONV_DIM = SSM_WIDTH + 2 * SSM_GROUPS * SSM_STATE
D_FF = 4 * D_MODEL

LANES = 128
SUBLANES = 8
VMEM_LIMIT_BYTES = 56 * 1024 * 1024

QK_PAD = 2 * LANES
PAD = 256
ROW_BLOCK = 256
MLP_BLOCK = 512
ROW_SPLIT = 1
ATT_Q = 512
ATT_K = 512
ATT_SUB = 1
ATT_UNROLL = 8
SSD_CHUNK = 128
FF_CHUNK = 1024
ATT_SCALE_LOG2 = (QK_DIM ** -0.5) * math.log2(math.e)

BF16 = jnp.bfloat16
F32 = jnp.float32


def _const_spec(shape):
    nd = len(shape)
    return pl.BlockSpec(shape, lambda *_: (0,) * nd, pipeline_mode=pl.Buffered(1))


def _rms(x, gain):
    return x * lax.rsqrt(jnp.mean(jnp.square(x), axis=-1, keepdims=True) + EPS) * gain


def _proj_kernel(meta_ref, x_ref, rope_blk_ref, rope_loc_ref, g_pre_ref, wqa_ref, wkva_ref, wz_ref, wxbc_ref,
                 wsm_ref, gq_ref, gkv_ref, wq_ref, wkt_ref, wv_ref, convw_ref, convb_ref,
                 q_ref, k_ref, v_ref, z_ref, xbc_ref, dt_ref, stage_ref):
    i = pl.program_id(0)
    rows_b = x_ref.shape[0]

    @pl.when(i == 0)
    def _():
        stage_ref[...] = jnp.zeros_like(stage_ref)

    conv = convb_ref[...]
    for kk in range(CONV_K):
        off = SUBLANES - (CONV_K - 1) + kk
        conv = conv + convw_ref[kk:kk + 1, :] * stage_ref[off:off + rows_b, :]
    xbc_ref[...] = conv * jax.nn.sigmoid(conv)
    stage_ref[0:SUBLANES, :] = stage_ref[rows_b:rows_b + SUBLANES, :]

    xin = jnp.where(i == 0, meta_ref[...], x_ref[...])
    hn = _rms(xin, g_pre_ref[...]).astype(BF16)

    blk = rope_blk_ref[0]
    cs = blk[0:1, :] * rope_loc_ref[0] + blk[1:2, :] * rope_loc_ref[1]
    low = lax.broadcasted_iota(jnp.int32, cs.shape, 1) < QK_ROPE

    def rope(t):
        prod = t * cs
        return prod + pltpu.roll(prod, QK_ROPE, axis=1)

    z_ref[...] = jnp.dot(hn, wz_ref[...], preferred_element_type=F32)
    stage_ref[SUBLANES:, :] = jnp.dot(hn, wxbc_ref[...], preferred_element_type=F32)

    small = jnp.dot(hn, wsm_ref[...], preferred_element_type=F32)
    dt_ref[...] = small[:, LANES:]
    k_rope_t = jnp.where(low, rope(small[:, :LANES]), 0.0).T.astype(BF16)

    c_q = jnp.dot(hn, wqa_ref[...], preferred_element_type=F32)
    cqn = _rms(c_q, gq_ref[...]).astype(BF16)
    c_kv = jnp.dot(hn, wkva_ref[...], preferred_element_type=F32)
    ckvn = _rms(c_kv, gkv_ref[...])
    k_nope_t = jnp.dot(wkt_ref[...], ckvn.T.astype(BF16), preferred_element_type=F32)
    v_all = jnp.dot(ckvn.astype(BF16), wv_ref[...], preferred_element_type=F32)

    for h in range(ATT_HEADS):
        qh = jnp.dot(cqn, wq_ref[:, h * QK_PAD:(h + 1) * QK_PAD], preferred_element_type=F32)
        q_ref[h, :, :QK_NOPE] = (qh[:, :QK_NOPE] * ATT_SCALE_LOG2).astype(BF16)
        q_ref[h, :, QK_NOPE:] = (rope(qh[:, QK_NOPE:]) * ATT_SCALE_LOG2).astype(BF16)
        k_ref[h, :QK_NOPE, :] = k_nope_t[h * QK_NOPE:(h + 1) * QK_NOPE, :].astype(BF16)
        k_ref[h, QK_NOPE:, :] = k_rope_t
        v_ref[h] = v_all[:, h * V_HEAD:(h + 1) * V_HEAD].astype(BF16)


def _input_projection(x2d, meta_pad, rope_blk, rope_loc, g_pre, wqa, wkva, wz, wxbc, wsm, gq, gkv, wq, wkt, wv,
                      conv_w, conv_b):
    seq = x2d.shape[0]
    n_blocks = seq // ROW_BLOCK + 1
    rows_p = seq + PAD
    pblk = lambda i: jnp.minimum(i, n_blocks - 1)
    rblk = lambda i: jnp.maximum(pblk(i) - 1, 0)
    real = lambda i: (rblk(i), 0)
    real3 = lambda i: (0, rblk(i), 0)
    padded = lambda i: (pblk(i), 0)
    padded3 = lambda i: (0, pblk(i), 0)
    lagged = lambda i: (jnp.maximum(i - 1, 0), 0)
    weights = (g_pre, wqa, wkva, wz, wxbc, wsm, gq, gkv, wq, wkt, wv, conv_w, conv_b)
    return pl.pallas_call(
        _proj_kernel,
        grid=(n_blocks + 1,),
        in_specs=[_const_spec(meta_pad.shape),
                  pl.BlockSpec((ROW_BLOCK, D_MODEL), real),
                  pl.BlockSpec((1, SUBLANES, LANES), lambda i: (pblk(i), 0, 0)),
                  _const_spec(rope_loc.shape)] + [_const_spec(w.shape) for w in weights],
        out_specs=[pl.BlockSpec((ATT_HEADS, ROW_BLOCK, QK_PAD), real3),
                   pl.BlockSpec((ATT_HEADS, QK_PAD, ROW_BLOCK), lambda i: (0, 0, pblk(i))),
                   pl.BlockSpec((ATT_HEADS, ROW_BLOCK, V_HEAD), padded3),
                   pl.BlockSpec((ROW_BLOCK, SSM_WIDTH), real),
                   pl.BlockSpec((ROW_BLOCK, CONV_DIM), lagged),
                   pl.BlockSpec((ROW_BLOCK, LANES), padded)],
        out_shape=[jax.ShapeDtypeStruct((ATT_HEADS, seq, QK_PAD), BF16),
                   jax.ShapeDtypeStruct((ATT_HEADS, QK_PAD, rows_p), BF16),
                   jax.ShapeDtypeStruct((ATT_HEADS, rows_p, V_HEAD), BF16),
                   jax.ShapeDtypeStruct((seq, SSM_WIDTH), F32),
                   jax.ShapeDtypeStruct((rows_p, CONV_DIM), F32),
                   jax.ShapeDtypeStruct((rows_p, LANES), F32)],
        scratch_shapes=[pltpu.VMEM((SUBLANES + ROW_BLOCK, CONV_DIM), F32)],
        compiler_params=pltpu.CompilerParams(dimension_semantics=("arbitrary",),
                                             vmem_limit_bytes=VMEM_LIMIT_BYTES),
        name="input_projection",
    )(meta_pad, x2d, rope_blk, rope_loc, *weights)


def _attn_kernel(q_ref, k_ref, v_ref, o_ref, sd_ref, pd_ref, s_ref, p_ref, m_ref, l_ref, acc_ref):
    n_tiles = ATT_K // LANES
    assert ATT_Q == ATT_K, "one diagonal chunk per query block"
    assert ATT_UNROLL % 2 == 0 and PAD >= LANES

    def chunk_start(j):
        return pl.multiple_of(PAD + j * ATT_K, math.gcd(PAD, ATT_K))

    def rows_at(ref, j):
        return ref[0, pl.ds(chunk_start(j), ATT_K), :]

    def keys_at(j):
        return k_ref[0, :, pl.ds(chunk_start(j), ATT_K)]

    def query_block(i, qrows):
        def scores(buf, j):
            s_ref[buf] = jnp.dot(q_ref[0, qrows, :], keys_at(j), preferred_element_type=F32)

        def score_tiles(buf):
            return [s_ref[buf, :, t * LANES:(t + 1) * LANES] for t in range(n_tiles)]

        def consume(buf, j):
            tiles = score_tiles(buf)
            mx = tiles[0]
            for t in range(1, n_tiles):
                mx = jnp.maximum(mx, tiles[t])
            m_old = m_ref[...]
            m_new = jnp.maximum(m_old, jnp.max(mx, axis=1, keepdims=True))
            m_ref[...] = m_new
            alpha = jnp.exp2(m_old - m_new)
            tiles = score_tiles(buf)
            lsum = None
            for t in range(n_tiles):
                p = jnp.exp2(tiles[t] - m_new)
                lsum = p if lsum is None else lsum + p
                p_ref[:, t * LANES:(t + 1) * LANES] = p.astype(BF16)
            l_ref[...] = alpha * l_ref[...] + lsum
            acc_ref[...] = alpha * acc_ref[...] + jnp.dot(p_ref[...], rows_at(v_ref, j),
                                                          preferred_element_type=F32)

        def step(cur, nxt, j):
            scores(nxt, j + 1)
            consume(cur, j)

        def first_chunk():
            rows = lax.broadcasted_iota(jnp.int32, (ATT_Q, LANES), 0)
            cols = lax.broadcasted_iota(jnp.int32, (ATT_Q, LANES), 1)

            def tiles():
                out = [jnp.where(cols >= LANES - N_META, sd_ref[:, :LANES], -jnp.inf)]
                for t in range(n_tiles):
                    out.append(jnp.where(cols + t * LANES <= rows,
                                         sd_ref[:, (t + 1) * LANES:(t + 2) * LANES], -jnp.inf))
                return out

            ts = tiles()
            mx = ts[0]
            for t in range(1, n_tiles + 1):
                mx = jnp.maximum(mx, ts[t])
            m_new = jnp.broadcast_to(jnp.max(mx, axis=1, keepdims=True), (ATT_Q, LANES))
            m_ref[...] = m_new
            ts = tiles()
            lsum = None
            for t in range(n_tiles + 1):
                p = jnp.exp2(ts[t] - m_new)
                lsum = p if lsum is None else lsum + p
                pd_ref[:, t * LANES:(t + 1) * LANES] = p.astype(BF16)
            l_ref[...] = lsum
            acc_ref[...] = (
                jnp.dot(pd_ref[:, :LANES], v_ref[0, PAD - LANES:PAD, :], preferred_element_type=F32)
                + jnp.dot(pd_ref[:, LANES:], rows_at(v_ref, i), preferred_element_type=F32))

        sd_ref[:, :LANES] = jnp.dot(q_ref[0, qrows, :], k_ref[0, :, PAD - LANES:PAD],
                                    preferred_element_type=F32)
        sd_ref[:, LANES:] = jnp.dot(q_ref[0, qrows, :], keys_at(i), preferred_element_type=F32)
        scores(0, 0)
        first_chunk()

        def run(first, count):
            for u in range(count):
                step(u % 2, (u + 1) % 2, first + u)

        def trip(t, carry):
            run(ATT_UNROLL * t, ATT_UNROLL)
            return carry

        lax.fori_loop(0, i // ATT_UNROLL, trip, 0)

        done = i - i % ATT_UNROLL
        size = ATT_UNROLL // 2
        while size >= 2:
            @pl.when((i - done) >= size)
            def _():
                run(done, size)
            done = done + jnp.where((i - done) >= size, size, 0)
            size //= 2

        @pl.when(i - done >= 1)
        def _():
            run(done, 1)

        o_ref[qrows, :] = (acc_ref[...] / jnp.sum(l_ref[...], axis=1, keepdims=True)).astype(o_ref.dtype)

    for sub in range(ATT_SUB):
        query_block(ATT_SUB * pl.program_id(1) + sub, slice(sub * ATT_Q, (sub + 1) * ATT_Q))


def _attention(q, k, v):
    heads, seq, _ = q.shape
    rows_p = v.shape[1]
    step_rows = ATT_SUB * ATT_Q
    return pl.pallas_call(
        _attn_kernel,
        grid=(heads, seq // step_rows),
        in_specs=[pl.BlockSpec((1, step_rows, QK_PAD), lambda h, i: (h, i, 0)),
                  pl.BlockSpec((1, QK_PAD, rows_p), lambda h, i: (h, 0, 0)),
                  pl.BlockSpec((1, rows_p, V_HEAD), lambda h, i: (h, 0, 0))],
        out_specs=pl.BlockSpec((step_rows, V_HEAD), lambda h, i: (i, h)),
        out_shape=jax.ShapeDtypeStruct((seq, ATT_WIDTH), BF16),
        scratch_shapes=[pltpu.VMEM((ATT_Q, LANES + ATT_K), F32),
                        pltpu.VMEM((ATT_Q, LANES + ATT_K), BF16),
                        pltpu.VMEM((2, ATT_Q, ATT_K), F32),
                        pltpu.VMEM((ATT_Q, ATT_K), BF16),
                        pltpu.VMEM((ATT_Q, LANES), F32),
                        pltpu.VMEM((ATT_Q, LANES), F32),
                        pltpu.VMEM((ATT_Q, V_HEAD), F32)],
        compiler_params=pltpu.CompilerParams(dimension_semantics=("arbitrary", "arbitrary"),
                                             vmem_limit_bytes=VMEM_LIMIT_BYTES),
        name="causal_attention",
    )(q, k, v)


def _split_dot(v, mat, terms):
    out = None
    rem = v
    for t in range(terms):
        piece = rem.astype(BF16)
        part = jnp.dot(piece, mat, preferred_element_type=F32)
        out = part if out is None else out + part
        if t + 1 < terms:
            rem = rem - piece.astype(F32)
    return out


def _ssd_kernel(xc_ref, dt_ref, dtb_ref, alog_ref, dskip_ref, tril_ref, expand_ref, y_ref, state_ref):
    i = pl.program_id(0)
    rows_b = xc_ref.shape[0]

    @pl.when(i == 0)
    def _():
        state_ref[...] = jnp.zeros_like(state_ref)

    xc = xc_ref

    lane = lax.broadcasted_iota(jnp.int32, (SSD_CHUNK, LANES), 1)
    row = lax.broadcasted_iota(jnp.int32, (SSD_CHUNK, LANES), 0)
    causal = lax.broadcasted_iota(jnp.int32, (SSD_CHUNK, SSD_CHUNK), 0) >= lax.broadcasted_iota(
        jnp.int32, (SSD_CHUNK, SSD_CHUNK), 1)
    low_half = lane < SSM_HEAD_DIM
    a_row = -jnp.exp(alog_ref[...])
    tril = tril_ref[...]
    expand = expand_ref[...]
    heads_per_group = SSM_HEADS // SSM_GROUPS

    for c in range(rows_b // SSD_CHUNK):
        r0 = c * SSD_CHUNK
        xs = xc[r0:r0 + SSD_CHUNK, :SSM_WIDTH]
        valid = (lane < SSM_HEADS) & (row + (i * rows_b + r0) >= PAD - N_META)
        dt = jnp.where(valid, jax.nn.softplus(dt_ref[r0:r0 + SSD_CHUNK, :] + dtb_ref[...]), 0.0)
        da = dt * a_row
        acum = _cumsum_rows(tril, da)
        acum_t = acum.T
        dt_t = dt.T
        a_last = acum[SSD_CHUNK - 1:SSD_CHUNK, :]
        e_acum = _split_dot(jnp.exp(acum), expand, 2)
        w_end = _split_dot(jnp.exp(a_last - acum) * dt, expand, 2)
        c_decay = _split_dot(jnp.broadcast_to(jnp.exp(a_last), (SUBLANES, LANES)), expand, 2)[0:1, :]

        y_diag = []
        y_off = []
        for g in range(SSM_GROUPS):
            b_g = xc[r0:r0 + SSD_CHUNK, SSM_WIDTH + g * SSM_STATE:SSM_WIDTH + (g + 1) * SSM_STATE]
            c_g = xc[r0:r0 + SSD_CHUNK, SSM_WIDTH + (SSM_GROUPS + g) * SSM_STATE:
                     SSM_WIDTH + (SSM_GROUPS + g + 1) * SSM_STATE].astype(BF16)
            b_t = b_g.T.astype(BF16)
            cb = jnp.dot(c_g, b_t, preferred_element_type=F32)
            gcols = slice(g * GROUP_WIDTH, (g + 1) * GROUP_WIDTH)
            for jp in range(heads_per_group // 2):
                ws = []
                for r in (g * heads_per_group + 2 * jp, g * heads_per_group + 2 * jp + 1):
                    seg = acum[:, r:r + 1] - acum_t[r:r + 1, :]
                    decay = jnp.where(causal, jnp.exp(seg), 0.0)
                    ws.append((cb * decay * dt_t[r:r + 1, :]).astype(BF16))
                col = g * GROUP_WIDTH + jp * LANES
                x_pair = xs[:, col:col + LANES]
                x_cat = jnp.concatenate([jnp.where(low_half, x_pair, 0.0), jnp.where(low_half, 0.0, x_pair)],
                                        axis=0).astype(BF16)
                y_diag.append(jnp.dot(jnp.concatenate(ws, axis=1), x_cat, preferred_element_type=F32))
            h_prev = state_ref[g]
            y_off.append(jnp.dot(c_g, h_prev.astype(BF16), preferred_element_type=F32))
            x_end = (xs[:, gcols] * w_end[:, gcols]).astype(BF16)
            state_ref[g] = c_decay[:, gcols] * h_prev + jnp.dot(b_t, x_end, preferred_element_type=F32)
        y = (jnp.concatenate(y_diag, axis=1) + jnp.concatenate(y_off, axis=1) * e_acum
             + xs * dskip_ref[...])
        y_ref[r0:r0 + SSD_CHUNK, :] = y


def _cumsum_rows(tril, v):
    out = None
    rem = v
    for t in range(3):
        piece = rem.astype(BF16)
        part = jnp.dot(tril, piece, preferred_element_type=F32)
        out = part if out is None else out + part
        if t < 2:
            rem = rem - piece.astype(F32)
    return out


def _ssd_scan(xc, dt_raw, dt_bias, a_log, d_skip_x, tril, expand, seq):
    n_blocks = seq // ROW_BLOCK + 1
    consts = (dt_bias, a_log, d_skip_x, tril, expand)
    return pl.pallas_call(
        _ssd_kernel,
        grid=(n_blocks,),
        in_specs=[pl.BlockSpec((ROW_BLOCK, CONV_DIM), lambda i: (i, 0)),
                  pl.BlockSpec((ROW_BLOCK, LANES), lambda i: (i, 0))] + [_const_spec(c.shape) for c in consts],
        out_specs=pl.BlockSpec((ROW_BLOCK, SSM_WIDTH), lambda i: (jnp.maximum(i - 1, 0), 0)),
        out_shape=jax.ShapeDtypeStruct((seq, SSM_WIDTH), F32),
        scratch_shapes=[pltpu.VMEM((SSM_GROUPS, SSM_STATE, GROUP_WIDTH), F32)],
        compiler_params=pltpu.CompilerParams(dimension_semantics=("arbitrary",),
                                             vmem_limit_bytes=VMEM_LIMIT_BYTES),
        name="ssd_scan",
    )(xc, dt_raw, *consts)


def _out_mlp_kernel(att_ref, y_ref, z_ref, x_ref, g_ssm_ref, wout_ref, g_post_ref, g_mpre_ref,
                    wup_ref, wdown_ref, g_mpost_ref, o_ref):
    sub = att_ref.shape[0] // ROW_SPLIT
    for r in range(ROW_SPLIT):
        rows = slice(r * sub, (r + 1) * sub)
        z = z_ref[rows, :]
        gated = y_ref[rows, :] * (z * jax.nn.sigmoid(z))
        parts = []
        for g in range(SSM_GROUPS):
            gg = gated[:, g * GROUP_WIDTH:(g + 1) * GROUP_WIDTH]
            parts.append(gg * lax.rsqrt(jnp.mean(jnp.square(gg), axis=-1, keepdims=True) + EPS))
        ssm = (jnp.concatenate(parts, axis=1) * g_ssm_ref[...]).astype(BF16)
        mix = (jnp.dot(att_ref[rows, :], wout_ref[:ATT_WIDTH, :], preferred_element_type=F32)
               + jnp.dot(ssm, wout_ref[ATT_WIDTH:, :], preferred_element_type=F32))
        h1 = x_ref[rows, :] + _rms(mix, g_post_ref[...])
        hn = _rms(h1, g_mpre_ref[...]).astype(BF16)
        f = None
        for c in range(D_FF // FF_CHUNK):
            u = jnp.dot(hn, wup_ref[:, c * FF_CHUNK:(c + 1) * FF_CHUNK], preferred_element_type=F32)
            u = jnp.square(jnp.maximum(u, 0.0)).astype(BF16)
            part = jnp.dot(u, wdown_ref[c * FF_CHUNK:(c + 1) * FF_CHUNK, :], preferred_element_type=F32)
            f = part if f is None else f + part
        o_ref[rows, :] = h1 + _rms(f, g_mpost_ref[...])


def _out_mlp(att, y, z, x2d, g_ssm, wout, g_post, g_mpre, wup, wdown, g_mpost):
    seq = x2d.shape[0]
    weights = (g_ssm, wout, g_post, g_mpre, wup, wdown, g_mpost)
    row = lambda i: (i, 0)
    return pl.pallas_call(
        _out_mlp_kernel,
        grid=(seq // MLP_BLOCK,),
        in_specs=[pl.BlockSpec((MLP_BLOCK, ATT_WIDTH), row),
                  pl.BlockSpec((MLP_BLOCK, SSM_WIDTH), row),
                  pl.BlockSpec((MLP_BLOCK, SSM_WIDTH), row),
                  pl.BlockSpec((MLP_BLOCK, D_MODEL), row)] + [_const_spec(w.shape) for w in weights],
        out_specs=pl.BlockSpec((MLP_BLOCK, D_MODEL), row),
        out_shape=jax.ShapeDtypeStruct((seq, D_MODEL), F32),
        compiler_params=pltpu.CompilerParams(dimension_semantics=("arbitrary",),
                                             vmem_limit_bytes=VMEM_LIMIT_BYTES),
        name="out_proj_mlp",
    )(att, y, z, x2d, *weights)


def _rot_cols(w):
    half = QK_ROPE // 2
    return jnp.concatenate([-w[..., half:], w[..., :half]], axis=-1)


def _lane_pad(v, width=LANES):
    return jnp.pad(v, [(0, 0)] * (v.ndim - 1) + [(0, width - v.shape[-1])])


def kernel(x, meta_tokens, norm_mix_pre, w_in, q_a_norm, w_q_up, kv_a_norm, w_kv_up, conv_w, conv_b, dt_bias,
           a_log, d_skip, ssm_norm, w_out, norm_mix_post, norm_mlp_pre, w_mlp_up, w_mlp_down, norm_mlp_post):
    bsz, seq, _ = x.shape
    assert bsz == 1 and norm_mix_pre.shape[0] == 1, "single sequence, single layer"
    assert seq % (ATT_SUB * ATT_Q) == 0 and seq % ROW_BLOCK == 0 and seq % MLP_BLOCK == 0
    assert PAD == ROW_BLOCK, "the front pad is exactly the first block of the projection / SSD grids"
    x2d = x[0]
    row = lambda v: v.reshape(1, -1)

    o_ckv, o_kr, o_z, o_xbc, o_dt = Q_LORA, Q_LORA + KV_LORA, Q_LORA + KV_LORA + QK_ROPE, \
        Q_LORA + KV_LORA + QK_ROPE + SSM_WIDTH, Q_LORA + KV_LORA + QK_ROPE + SSM_WIDTH + CONV_DIM
    w_in0 = w_in[0]
    w_kr = w_in0[:, o_kr:o_z]
    wsm = jnp.concatenate([w_kr, _rot_cols(w_kr), _lane_pad(w_in0[:, o_dt:])], axis=1)
    wq3 = w_q_up[0].reshape(Q_LORA, ATT_HEADS, QK_DIM)
    wq = jnp.concatenate([wq3, _rot_cols(wq3[..., QK_NOPE:])], axis=-1).reshape(Q_LORA, ATT_HEADS * QK_PAD)
    wkv3 = w_kv_up[0].reshape(KV_LORA, ATT_HEADS, QK_NOPE + V_HEAD)
    wkt = wkv3[..., :QK_NOPE].reshape(KV_LORA, ATT_HEADS * QK_NOPE).T
    wv = wkv3[..., QK_NOPE:].reshape(KV_LORA, ATT_HEADS * V_HEAD)

    inv_freq = ROPE_THETA ** (-jnp.arange(0, QK_ROPE, 2, dtype=F32) / QK_ROPE)
    freq = jnp.tile(inv_freq, 4)
    low = jnp.arange(LANES) < QK_ROPE
    base = (jnp.arange(seq // ROW_BLOCK + 1) * ROW_BLOCK - (PAD - N_META)).astype(F32)
    ang_b = base[:, None] * freq[None, :]
    rope_blk = jnp.stack([jnp.cos(ang_b), jnp.where(low, -jnp.sin(ang_b), jnp.sin(ang_b))], axis=1)
    rope_blk = jnp.pad(rope_blk, ((0, 0), (0, SUBLANES - 2), (0, 0)))
    ang_l = jnp.arange(ROW_BLOCK, dtype=F32)[:, None] * freq[None, :]
    rope_loc = jnp.stack([jnp.where(low, jnp.cos(ang_l), jnp.sin(ang_l)),
                          jnp.where(low, jnp.sin(ang_l), jnp.cos(ang_l))])

    meta_pad = jnp.concatenate([jnp.zeros((PAD - N_META, D_MODEL), F32), meta_tokens.astype(F32)], axis=0)

    q, k, v, z, xc, dt_raw = _input_projection(
        x2d, meta_pad, rope_blk, rope_loc, row(norm_mix_pre[0]),
        w_in0[:, :o_ckv].astype(BF16), w_in0[:, o_ckv:o_kr].astype(BF16), w_in0[:, o_z:o_xbc].astype(BF16),
        w_in0[:, o_xbc:o_dt].astype(BF16), wsm.astype(BF16),
        row(q_a_norm[0]), row(kv_a_norm[0]), wq.astype(BF16), wkt.astype(BF16), wv.astype(BF16),
        conv_w[0], row(conv_b[0]))

    att = _attention(q, k, v)

    tril = jnp.tril(jnp.ones((SSD_CHUNK, SSD_CHUNK), BF16))
    expand = (jnp.arange(LANES)[:, None] == jnp.arange(SSM_WIDTH)[None, :] // SSM_HEAD_DIM).astype(BF16)
    y = _ssd_scan(xc, dt_raw, _lane_pad(row(dt_bias[0])), _lane_pad(row(a_log[0])),
                  row(jnp.repeat(d_skip[0], SSM_HEAD_DIM)), tril, expand, seq)

    out = _out_mlp(att, y, z, x2d, row(ssm_norm[0]), w_out[0].astype(BF16), row(norm_mix_post[0]),
                   row(norm_mlp_pre[0]), w_mlp_up[0].astype(BF16), w_mlp_down[0].astype(BF16),
                   row(norm_mlp_post[0]))
    return out[None]
```

```python
import functools
import math

import jax
import jax.numpy as jnp
import numpy as np
from jax import lax
from jax.experimental import pallas as pl
from jax.experimental.pallas import tpu as pltpu

D_MODEL = 1024
N_META = 16
EPS = 1e-6
ATT_HEADS = 8
Q_LORA = 384
KV_LORA = 256
QK_NOPE = 128
QK_ROPE = 64
V_HEAD = 128
QK_DIM = QK_NOPE + QK_ROPE
ROPE_THETA = 10000.0
ATT_WIDTH = ATT_HEADS * V_HEAD
SSM_HEADS = 16
SSM_HEAD_DIM = 64
SSM_WIDTH = SSM_HEADS * SSM_HEAD_DIM
SSM_GROUPS = 2
SSM_STATE = 128
GROUP_WIDTH = SSM_WIDTH // SSM_GROUPS
CONV_K = 4
CONV_DIM = SSM_WIDTH + 2 * SSM_GROUPS * SSM_STATE
D_FF = 4 * D_MODEL

LANES = 128
SUBLANES = 8
VMEM_LIMIT_BYTES = 56 * 1024 * 1024

QK_PAD = 2 * LANES
PAD = 256
ROW_BLOCK = 256
MLP_BLOCK = 512
ROW_SPLIT = 1
ATT_Q = 512
ATT_K = 512
ATT_UNROLL = 8
SSD_CHUNK = 128
FF_CHUNK = 1024
ATT_SCALE_LOG2 = (QK_DIM ** -0.5) * math.log2(math.e)

BF16 = jnp.bfloat16
F32 = jnp.float32


def _const_spec(shape):
    nd = len(shape)
    return pl.BlockSpec(shape, lambda *_: (0,) * nd, pipeline_mode=pl.Buffered(1))


def _rms(x, gain):
    return x * lax.rsqrt(jnp.mean(jnp.square(x), axis=-1, keepdims=True) + EPS) * gain


def _proj_kernel(meta_ref, x_ref, rope_blk_ref, rope_loc_ref, g_pre_ref, wqa_ref, wkva_ref, wz_ref, wxbc_ref,
                 wsm_ref, gq_ref, gkv_ref, wqt_ref, wk_ref, wvt_ref, convw_ref, convb_ref,
                 qt_ref, k_ref, vt_ref, z_ref, xbc_ref, dt_ref, stage_ref):
    i = pl.program_id(0)
    rows_b = x_ref.shape[0]

    @pl.when(i == 0)
    def _():
        stage_ref[...] = jnp.zeros_like(stage_ref)

    conv = convb_ref[...]
    for kk in range(CONV_K):
        off = SUBLANES - (CONV_K - 1) + kk
        conv = conv + convw_ref[kk:kk + 1, :] * stage_ref[off:off + rows_b, :]
    xbc_ref[...] = conv * jax.nn.sigmoid(conv)
    stage_ref[0:SUBLANES, :] = stage_ref[rows_b:rows_b + SUBLANES, :]

    xin = jnp.where(i == 0, meta_ref[...], x_ref[...])
    hn = _rms(xin, g_pre_ref[...]).astype(BF16)

    blk = rope_blk_ref[0]
    cs = blk[0:1, :] * rope_loc_ref[0] + blk[1:2, :] * rope_loc_ref[1]
    low = lax.broadcasted_iota(jnp.int32, cs.shape, 1) < QK_ROPE

    def rope(t):
        prod = t * cs
        return prod + pltpu.roll(prod, QK_ROPE, axis=1)

    z_ref[...] = jnp.dot(hn, wz_ref[...], preferred_element_type=F32)
    stage_ref[SUBLANES:, :] = jnp.dot(hn, wxbc_ref[...], preferred_element_type=F32)

    small = jnp.dot(hn, wsm_ref[...], preferred_element_type=F32)
    dt_ref[...] = small[:, LANES:]
    k_rope = jnp.where(low, rope(small[:, :LANES]), 0.0).astype(BF16)

    c_q = jnp.dot(hn, wqa_ref[...], preferred_element_type=F32)
    cqn_t = _rms(c_q, gq_ref[...]).T.astype(BF16)
    c_kv = jnp.dot(hn, wkva_ref[...], preferred_element_type=F32)
    ckvn = _rms(c_kv, gkv_ref[...])
    k_all = jnp.dot(ckvn.astype(BF16), wk_ref[...], preferred_element_type=F32)
    vt_all = jnp.dot(wvt_ref[...], ckvn.T.astype(BF16), preferred_element_type=F32)
    cs_t = cs.T
    zeros_t = jnp.zeros((QK_PAD - QK_DIM, rows_b), BF16)

    for h in range(ATT_HEADS):
        qt = jnp.dot(wqt_ref[h * QK_PAD:(h + 1) * QK_PAD, :], cqn_t, preferred_element_type=F32)
        prod = qt[QK_NOPE:, :] * cs_t
        qt_ref[h, :QK_NOPE, :] = (qt[:QK_NOPE, :] * ATT_SCALE_LOG2).astype(BF16)
        qt_ref[h, QK_NOPE:QK_DIM, :] = ((prod[:QK_ROPE, :] + prod[QK_ROPE:, :]) * ATT_SCALE_LOG2).astype(BF16)
        qt_ref[h, QK_DIM:, :] = zeros_t
        k_ref[h, :, :QK_NOPE] = k_all[:, h * QK_NOPE:(h + 1) * QK_NOPE].astype(BF16)
        k_ref[h, :, QK_NOPE:] = k_rope
        vt_ref[h] = vt_all[h * V_HEAD:(h + 1) * V_HEAD, :].astype(BF16)


def _input_projection(x2d, meta_pad, rope_blk, rope_loc, g_pre, wqa, wkva, wz, wxbc, wsm, gq, gkv, wqt, wk, wvt,
                      conv_w, conv_b):
    seq = x2d.shape[0]
    n_blocks = seq // ROW_BLOCK + 1
    rows_p = seq + PAD
    pblk = lambda i: jnp.minimum(i, n_blocks - 1)
    rblk = lambda i: jnp.maximum(pblk(i) - 1, 0)
    real = lambda i: (rblk(i), 0)
    real3 = lambda i: (0, rblk(i), 0)
    padded = lambda i: (pblk(i), 0)
    padded3 = lambda i: (0, pblk(i), 0)
    lagged = lambda i: (jnp.maximum(i - 1, 0), 0)
    weights = (g_pre, wqa, wkva, wz, wxbc, wsm, gq, gkv, wqt, wk, wvt, conv_w, conv_b)
    return pl.pallas_call(
        _proj_kernel,
        grid=(n_blocks + 1,),
        in_specs=[_const_spec(meta_pad.shape),
                  pl.BlockSpec((ROW_BLOCK, D_MODEL), real),
                  pl.BlockSpec((1, SUBLANES, LANES), lambda i: (pblk(i), 0, 0)),
                  _const_spec(rope_loc.shape)] + [_const_spec(w.shape) for w in weights],
        out_specs=[pl.BlockSpec((ATT_HEADS, QK_PAD, ROW_BLOCK), lambda i: (0, 0, rblk(i))),
                   pl.BlockSpec((ATT_HEADS, ROW_BLOCK, QK_PAD), padded3),
                   pl.BlockSpec((ATT_HEADS, V_HEAD, ROW_BLOCK), lambda i: (0, 0, pblk(i))),
                   pl.BlockSpec((ROW_BLOCK, SSM_WIDTH), real),
                   pl.BlockSpec((ROW_BLOCK, CONV_DIM), lagged),
                   pl.BlockSpec((ROW_BLOCK, LANES), padded)],
        out_shape=[jax.ShapeDtypeStruct((ATT_HEADS, QK_PAD, seq), BF16),
                   jax.ShapeDtypeStruct((ATT_HEADS, rows_p, QK_PAD), BF16),
                   jax.ShapeDtypeStruct((ATT_HEADS, V_HEAD, rows_p), BF16),
                   jax.ShapeDtypeStruct((seq, SSM_WIDTH), F32),
                   jax.ShapeDtypeStruct((rows_p, CONV_DIM), F32),
                   jax.ShapeDtypeStruct((rows_p, LANES), F32)],
        scratch_shapes=[pltpu.VMEM((SUBLANES + ROW_BLOCK, CONV_DIM), F32)],
        compiler_params=pltpu.CompilerParams(dimension_semantics=("arbitrary",),
                                             vmem_limit_bytes=VMEM_LIMIT_BYTES),
        name="input_projection",
    )(meta_pad, x2d, rope_blk, rope_loc, *weights)


def _attn_kernel(qt_ref, k_ref, vt_ref, o_ref, sd_ref, pd_ref, s_ref, p_ref, m_ref, l_ref, al_ref, acc_ref):
    assert ATT_Q == ATT_K, "one diagonal chunk per query block"
    assert ATT_UNROLL % 2 == 0 and PAD >= LANES
    i = pl.program_id(1)

    def chunk_start(j):
        return pl.multiple_of(PAD + j * ATT_K, math.gcd(PAD, ATT_K))

    def scores(buf, j):
        s_ref[buf] = jnp.dot(k_ref[0, pl.ds(chunk_start(j), ATT_K), :], qt_ref[0], preferred_element_type=F32)

    def consume(buf, j):
        for c in range(ATT_Q // LANES):
            col = slice(c * LANES, (c + 1) * LANES)
            row_blocks = [slice(r * LANES, (r + 1) * LANES) for r in range(ATT_K // LANES)]
            mx = None
            for rows in row_blocks:
                t = s_ref[buf, rows, col]
                mx = t if mx is None else jnp.maximum(mx, t)
            m_old = m_ref[:, col]
            m_new = jnp.maximum(m_old, jnp.max(mx, axis=0, keepdims=True))
            m_ref[:, col] = m_new
            alpha = jnp.exp2(m_old - m_new)
            al_ref[:, col] = alpha
            psum = None
            for rows in row_blocks:
                p = jnp.exp2(s_ref[buf, rows, col] - m_new)
                psum = p if psum is None else psum + p
                p_ref[rows, col] = p.astype(BF16)
            l_ref[:, col] = alpha * l_ref[:, col] + jnp.sum(psum, axis=0, keepdims=True)
        acc_ref[...] = al_ref[...] * acc_ref[...] + jnp.dot(
            vt_ref[0, :, pl.ds(chunk_start(j), ATT_K)], p_ref[...], preferred_element_type=F32)

    def step(cur, nxt, j):
        scores(nxt, j + 1)
        consume(cur, j)

    def first_chunk():
        krow_m = lax.broadcasted_iota(jnp.int32, (LANES, ATT_Q), 0)
        krow = lax.broadcasted_iota(jnp.int32, (ATT_K, ATT_Q), 0)
        qcol = lax.broadcasted_iota(jnp.int32, (ATT_K, ATT_Q), 1)

        def masked():
            return (jnp.where(krow_m >= LANES - N_META, sd_ref[:LANES, :], -jnp.inf),
                    jnp.where(krow <= qcol, sd_ref[LANES:, :], -jnp.inf))

        sm, sk = masked()
        m_new = jnp.maximum(jnp.max(sm, axis=0, keepdims=True), jnp.max(sk, axis=0, keepdims=True))
        m_ref[...] = m_new
        sm, sk = masked()
        pm = jnp.exp2(sm - m_new)
        pk = jnp.exp2(sk - m_new)
        l_ref[...] = jnp.sum(pm, axis=0, keepdims=True) + jnp.sum(pk, axis=0, keepdims=True)
        pd_ref[:LANES, :] = pm.astype(BF16)
        pd_ref[LANES:, :] = pk.astype(BF16)
        acc_ref[...] = (
            jnp.dot(vt_ref[0, :, PAD - LANES:PAD], pd_ref[:LANES, :], preferred_element_type=F32)
            + jnp.dot(vt_ref[0, :, pl.ds(chunk_start(i), ATT_K)], pd_ref[LANES:, :],
                      preferred_element_type=F32))

    sd_ref[:LANES, :] = jnp.dot(k_ref[0, PAD - LANES:PAD, :], qt_ref[0], preferred_element_type=F32)
    sd_ref[LANES:, :] = jnp.dot(k_ref[0, pl.ds(chunk_start(i), ATT_K), :], qt_ref[0],
                                preferred_element_type=F32)
    scores(0, 0)
    first_chunk()

    def run(first, count):
        for u in range(count):
            step(u % 2, (u + 1) % 2, first + u)

    def trip(t, carry):
        run(ATT_UNROLL * t, ATT_UNROLL)
        return carry

    lax.fori_loop(0, i // ATT_UNROLL, trip, 0)

    done = i - i % ATT_UNROLL
    size = ATT_UNROLL // 2
    while size >= 2:
        @pl.when((i - done) >= size)
        def _():
            run(done, size)
        done = done + jnp.where((i - done) >= size, size, 0)
        size //= 2

    @pl.when(i - done >= 1)
    def _():
        run(done, 1)

    o_ref[...] = (acc_ref[...] / l_ref[...]).T.astype(o_ref.dtype)


def _attention(qt, k, vt):
    heads, _, seq = qt.shape
    rows_p = k.shape[1]
    return pl.pallas_call(
        _attn_kernel,
        grid=(heads, seq // ATT_Q),
        in_specs=[pl.BlockSpec((1, QK_PAD, ATT_Q), lambda h, i: (h, 0, i)),
                  pl.BlockSpec((1, rows_p, QK_PAD), lambda h, i: (h, 0, 0)),
                  pl.BlockSpec((1, V_HEAD, rows_p), lambda h, i: (h, 0, 0))],
        out_specs=pl.BlockSpec((ATT_Q, V_HEAD), lambda h, i: (i, h)),
        out_shape=jax.ShapeDtypeStruct((seq, ATT_WIDTH), BF16),
        scratch_shapes=[pltpu.VMEM((LANES + ATT_K, ATT_Q), F32),
                        pltpu.VMEM((LANES + ATT_K, ATT_Q), BF16),
                        pltpu.VMEM((2, ATT_K, ATT_Q), F32),
                        pltpu.VMEM((ATT_K, ATT_Q), BF16),
                        pltpu.VMEM((1, ATT_Q), F32),
                        pltpu.VMEM((1, ATT_Q), F32),
                        pltpu.VMEM((1, ATT_Q), F32),
                        pltpu.VMEM((V_HEAD, ATT_Q), F32)],
        compiler_params=pltpu.CompilerParams(dimension_semantics=("arbitrary", "arbitrary"),
                                             vmem_limit_bytes=VMEM_LIMIT_BYTES),
        name="causal_attention",
    )(qt, k, vt)


def _split_dot(v, mat, terms):
    out = None
    rem = v
    for t in range(terms):
        piece = rem.astype(BF16)
        part = jnp.dot(piece, mat, preferred_element_type=F32)
        out = part if out is None else out + part
        if t + 1 < terms:
            rem = rem - piece.astype(F32)
    return out


def _ssd_kernel(xc_ref, dt_ref, dtb_ref, alog_ref, dskip_ref, tril_ref, expand_ref, y_ref, state_ref):
    i = pl.program_id(0)
    rows_b = xc_ref.shape[0]

    @pl.when(i == 0)
    def _():
        state_ref[...] = jnp.zeros_like(state_ref)

    xc = xc_ref

    lane = lax.broadcasted_iota(jnp.int32, (SSD_CHUNK, LANES), 1)
    row = lax.broadcasted_iota(jnp.int32, (SSD_CHUNK, LANES), 0)
    causal = lax.broadcasted_iota(jnp.int32, (SSD_CHUNK, SSD_CHUNK), 0) >= lax.broadcasted_iota(
        jnp.int32, (SSD_CHUNK, SSD_CHUNK), 1)
    low_half = lane < SSM_HEAD_DIM
    a_row = -jnp.exp(alog_ref[...])
    tril = tril_ref[...]
    expand = expand_ref[...]
    heads_per_group = SSM_HEADS // SSM_GROUPS

    for c in range(rows_b // SSD_CHUNK):
        r0 = c * SSD_CHUNK
        xs = xc[r0:r0 + SSD_CHUNK, :SSM_WIDTH]
        valid = (lane < SSM_HEADS) & (row + (i * rows_b + r0) >= PAD - N_META)
        dt = jnp.where(valid, jax.nn.softplus(dt_ref[r0:r0 + SSD_CHUNK, :] + dtb_ref[...]), 0.0)
        da = dt * a_row
        acum = _cumsum_rows(tril, da)
        acum_t = acum.T
        dt_t = dt.T
        a_last = acum[SSD_CHUNK - 1:SSD_CHUNK, :]
        e_acum = _split_dot(jnp.exp(acum), expand, 2)
        w_end = _split_dot(jnp.exp(a_last - acum) * dt, expand, 2)
        c_decay = _split_dot(jnp.broadcast_to(jnp.exp(a_last), (SUBLANES, LANES)), expand, 2)[0:1, :]

        y_diag = []
        y_off = []
        for g in range(SSM_GROUPS):
            b_g = xc[r0:r0 + SSD_CHUNK, SSM_WIDTH + g * SSM_STATE:SSM_WIDTH + (g + 1) * SSM_STATE]
            c_g = xc[r0:r0 + SSD_CHUNK, SSM_WIDTH + (SSM_GROUPS + g) * SSM_STATE:
                     SSM_WIDTH + (SSM_GROUPS + g + 1) * SSM_STATE].astype(BF16)
            b_t = b_g.T.astype(BF16)
            cb = jnp.dot(c_g, b_t, preferred_element_type=F32)
            gcols = slice(g * GROUP_WIDTH, (g + 1) * GROUP_WIDTH)
            for jp in range(heads_per_group // 2):
                ws = []
                for r in (g * heads_per_group + 2 * jp, g * heads_per_group + 2 * jp + 1):
                    seg = acum[:, r:r + 1] - acum_t[r:r + 1, :]
                    decay = jnp.where(causal, jnp.exp(seg), 0.0)
                    ws.append((cb * decay * dt_t[r:r + 1, :]).astype(BF16))
                col = g * GROUP_WIDTH + jp * LANES
                x_pair = xs[:, col:col + LANES]
                x_cat = jnp.concatenate([jnp.where(low_half, x_pair, 0.0), jnp.where(low_half, 0.0, x_pair)],
                                        axis=0).astype(BF16)
                y_diag.append(jnp.dot(jnp.concatenate(ws, axis=1), x_cat, preferred_element_type=F32))
            h_prev = state_ref[g]
            y_off.append(jnp.dot(c_g, h_prev.astype(BF16), preferred_element_type=F32))
            x_end = (xs[:, gcols] * w_end[:, gcols]).astype(BF16)
            state_ref[g] = c_decay[:, gcols] * h_prev + jnp.dot(b_t, x_end, preferred_element_type=F32)
        y = (jnp.concatenate(y_diag, axis=1) + jnp.concatenate(y_off, axis=1) * e_acum
             + xs * dskip_ref[...])
        y_ref[r0:r0 + SSD_CHUNK, :] = y


def _cumsum_rows(tril, v):
    out = None
    rem = v
    for t in range(3):
        piece = rem.astype(BF16)
        part = jnp.dot(tril, piece, preferred_element_type=F32)
        out = part if out is None else out + part
        if t < 2:
            rem = rem - piece.astype(F32)
    return out


def _ssd_scan(xc, dt_raw, dt_bias, a_log, d_skip_x, tril, expand, seq):
    n_blocks = seq // ROW_BLOCK + 1
    consts = (dt_bias, a_log, d_skip_x, tril, expand)
    return pl.pallas_call(
        _ssd_kernel,
        grid=(n_blocks,),
        in_specs=[pl.BlockSpec((ROW_BLOCK, CONV_DIM), lambda i: (i, 0)),
                  pl.BlockSpec((ROW_BLOCK, LANES), lambda i: (i, 0))] + [_const_spec(c.shape) for c in consts],
        out_specs=pl.BlockSpec((ROW_BLOCK, SSM_WIDTH), lambda i: (jnp.maximum(i - 1, 0), 0)),
        out_shape=jax.ShapeDtypeStruct((seq, SSM_WIDTH), F32),
        scratch_shapes=[pltpu.VMEM((SSM_GROUPS, SSM_STATE, GROUP_WIDTH), F32)],
        compiler_params=pltpu.CompilerParams(dimension_semantics=("arbitrary",),
                                             vmem_limit_bytes=VMEM_LIMIT_BYTES),
        name="ssd_scan",
    )(xc, dt_raw, *consts)


def _out_mlp_kernel(att_ref, y_ref, z_ref, x_ref, g_ssm_ref, wout_ref, g_post_ref, g_mpre_ref,
                    wup_ref, wdown_ref, g_mpost_ref, o_ref):
    sub = att_ref.shape[0] // ROW_SPLIT
    for r in range(ROW_SPLIT):
        rows = slice(r * sub, (r + 1) * sub)
        z = z_ref[rows, :]
        gated = y_ref[rows, :] * (z * jax.nn.sigmoid(z))
        parts = []
        for g in range(SSM_GROUPS):
            gg = gated[:, g * GROUP_WIDTH:(g + 1) * GROUP_WIDTH]
            parts.append(gg * lax.rsqrt(jnp.mean(jnp.square(gg), axis=-1, keepdims=True) + EPS))
        ssm = (jnp.concatenate(parts, axis=1) * g_ssm_ref[...]).astype(BF16)
        mix = (jnp.dot(att_ref[rows, :], wout_ref[:ATT_WIDTH, :], preferred_element_type=F32)
               + jnp.dot(ssm, wout_ref[ATT_WIDTH:, :], preferred_element_type=F32))
        h1 = x_ref[rows, :] + _rms(mix, g_post_ref[...])
        hn = _rms(h1, g_mpre_ref[...]).astype(BF16)
        f = None
        for c in range(D_FF // FF_CHUNK):
            u = jnp.dot(hn, wup_ref[:, c * FF_CHUNK:(c + 1) * FF_CHUNK], preferred_element_type=F32)
            u = jnp.square(jnp.maximum(u, 0.0)).astype(BF16)
            part = jnp.dot(u, wdown_ref[c * FF_CHUNK:(c + 1) * FF_CHUNK, :], preferred_element_type=F32)
            f = part if f is None else f + part
        o_ref[rows, :] = h1 + _rms(f, g_mpost_ref[...])


def _out_mlp(att, y, z, x2d, g_ssm, wout, g_post, g_mpre, wup, wdown, g_mpost):
    seq = x2d.shape[0]
    weights = (g_ssm, wout, g_post, g_mpre, wup, wdown, g_mpost)
    row = lambda i: (i, 0)
    return pl.pallas_call(
        _out_mlp_kernel,
        grid=(seq // MLP_BLOCK,),
        in_specs=[pl.BlockSpec((MLP_BLOCK, ATT_WIDTH), row),
                  pl.BlockSpec((MLP_BLOCK, SSM_WIDTH), row),
                  pl.BlockSpec((MLP_BLOCK, SSM_WIDTH), row),
                  pl.BlockSpec((MLP_BLOCK, D_MODEL), row)] + [_const_spec(w.shape) for w in weights],
        out_specs=pl.BlockSpec((MLP_BLOCK, D_MODEL), row),
        out_shape=jax.ShapeDtypeStruct((seq, D_MODEL), F32),
        compiler_params=pltpu.CompilerParams(dimension_semantics=("arbitrary",),
                                             vmem_limit_bytes=VMEM_LIMIT_BYTES),
        name="out_proj_mlp",
    )(att, y, z, x2d, *weights)


def _rot_cols(w):
    half = QK_ROPE // 2
    return jnp.concatenate([-w[..., half:], w[..., :half]], axis=-1)


def _lane_pad(v, width=LANES):
    return jnp.pad(v, [(0, 0)] * (v.ndim - 1) + [(0, width - v.shape[-1])])


def kernel(x, meta_tokens, norm_mix_pre, w_in, q_a_norm, w_q_up, kv_a_norm, w_kv_up, conv_w, conv_b, dt_bias,
           a_log, d_skip, ssm_norm, w_out, norm_mix_post, norm_mlp_pre, w_mlp_up, w_mlp_down, norm_mlp_post):
    bsz, seq, _ = x.shape
    assert bsz == 1 and norm_mix_pre.shape[0] == 1, "single sequence, single layer"
    assert seq % ATT_Q == 0 and seq % ROW_BLOCK == 0 and seq % MLP_BLOCK == 0
    assert PAD == ROW_BLOCK, "the front pad is exactly the first block of the projection / SSD grids"
    x2d = x[0]
    row = lambda v: v.reshape(1, -1)

    o_ckv, o_kr, o_z, o_xbc, o_dt = Q_LORA, Q_LORA + KV_LORA, Q_LORA + KV_LORA + QK_ROPE, \
        Q_LORA + KV_LORA + QK_ROPE + SSM_WIDTH, Q_LORA + KV_LORA + QK_ROPE + SSM_WIDTH + CONV_DIM
    w_in0 = w_in[0]
    w_kr = w_in0[:, o_kr:o_z]
    wsm = jnp.concatenate([w_kr, _rot_cols(w_kr), _lane_pad(w_in0[:, o_dt:])], axis=1)
    wq3 = w_q_up[0].reshape(Q_LORA, ATT_HEADS, QK_DIM)
    wq = jnp.concatenate([wq3, _rot_cols(wq3[..., QK_NOPE:])], axis=-1).reshape(Q_LORA, ATT_HEADS * QK_PAD)
    wkv3 = w_kv_up[0].reshape(KV_LORA, ATT_HEADS, QK_NOPE + V_HEAD)
    wk = wkv3[..., :QK_NOPE].reshape(KV_LORA, ATT_HEADS * QK_NOPE)
    wvt = wkv3[..., QK_NOPE:].reshape(KV_LORA, ATT_HEADS * V_HEAD).T

    inv_freq = ROPE_THETA ** (-jnp.arange(0, QK_ROPE, 2, dtype=F32) / QK_ROPE)
    freq = jnp.tile(inv_freq, 4)
    low = jnp.arange(LANES) < QK_ROPE
    base = (jnp.arange(seq // ROW_BLOCK + 1) * ROW_BLOCK - (PAD - N_META)).astype(F32)
    ang_b = base[:, None] * freq[None, :]
    rope_blk = jnp.stack([jnp.cos(ang_b), jnp.where(low, -jnp.sin(ang_b), jnp.sin(ang_b))], axis=1)
    rope_blk = jnp.pad(rope_blk, ((0, 0), (0, SUBLANES - 2), (0, 0)))
    ang_l = jnp.arange(ROW_BLOCK, dtype=F32)[:, None] * freq[None, :]
    rope_loc = jnp.stack([jnp.where(low, jnp.cos(ang_l), jnp.sin(ang_l)),
                          jnp.where(low, jnp.sin(ang_l), jnp.cos(ang_l))])

    meta_pad = jnp.concatenate([jnp.zeros((PAD - N_META, D_MODEL), F32), meta_tokens.astype(F32)], axis=0)

    qt, k, vt, z, xc, dt_raw = _input_projection(
        x2d, meta_pad, rope_blk, rope_loc, row(norm_mix_pre[0]),
        w_in0[:, :o_ckv].astype(BF16), w_in0[:, o_ckv:o_kr].astype(BF16), w_in0[:, o_z:o_xbc].astype(BF16),
        w_in0[:, o_xbc:o_dt].astype(BF16), wsm.astype(BF16),
        row(q_a_norm[0]), row(kv_a_norm[0]), wq.T.astype(BF16), wk.astype(BF16), wvt.astype(BF16),
        conv_w[0], row(conv_b[0]))

    att = _attention(qt, k, vt)

    tril = jnp.tril(jnp.ones((SSD_CHUNK, SSD_CHUNK), BF16))
    expand = (jnp.arange(LANES)[:, None] == jnp.arange(SSM_WIDTH)[None, :] // SSM_HEAD_DIM).astype(BF16)
    y = _ssd_scan(xc, dt_raw, _lane_pad(row(dt_bias[0])), _lane_pad(row(a_log[0])),
                  row(jnp.repeat(d_skip[0], SSM_HEAD_DIM)), tril, expand, seq)

    out = _out_mlp(att, y, z, x2d, row(ssm_norm[0]), w_out[0].astype(BF16), row(norm_mix_post[0]),
                   row(norm_mlp_pre[0]), w_mlp_up[0].astype(BF16), w_mlp_down[0].astype(BF16),
                   row(norm_mlp_post[0]))
    return out[None]
```

```python
import math

import jax
import jax.numpy as jnp
from jax import lax
from jax.experimental import pallas as pl
from jax.experimental.pallas import tpu as pltpu

D_MODEL = 1024
N_META = 16
EPS = 1e-6
ATT_HEADS = 8
Q_LORA = 384
KV_LORA = 256
QK_NOPE = 128
QK_ROPE = 64
V_HEAD = 128
QK_DIM = QK_NOPE + QK_ROPE
ROPE_THETA = 10000.0
ATT_WIDTH = ATT_HEADS * V_HEAD
SSM_HEADS = 16
SSM_HEAD_DIM = 64
SSM_WIDTH = SSM_HEADS * SSM_HEAD_DIM
SSM_GROUPS = 2
SSM_STATE = 128
GROUP_WIDTH = SSM_WIDTH // SSM_GROUPS
CONV_K = 4
CONV_DIM = SSM_WIDTH + 2 * SSM_GROUPS * SSM_STATE
D_FF = 4 * D_MODEL

LANES = 128
SUBLANES = 8
VMEM_LIMIT_BYTES = 56 * 1024 * 1024

QK_PAD = 2 * LANES
PAD = 256
ROW_BLOCK = 256
MLP_BLOCK = 512
ATT_Q = 512
ATT_K = 512
ATT_UNROLL = 8
SSD_CHUNK = 128
FF_CHUNK = 1024
ATT_SCALE_LOG2 = (QK_DIM ** -0.5) * math.log2(math.e)

BF16 = jnp.bfloat16
F32 = jnp.float32


def _const_spec(shape):
    nd = len(shape)
    return pl.BlockSpec(shape, lambda *_: (0,) * nd, pipeline_mode=pl.Buffered(1))


def _rms(x, gain):
    return x * lax.rsqrt(jnp.mean(jnp.square(x), axis=-1, keepdims=True) + EPS) * gain


def _proj_kernel(meta_ref, x_ref, rope_blk_ref, rope_loc_ref, g_pre_ref, wqa_ref, wkva_ref, wz_ref, wxbc_ref,
                 wsm_ref, gq_ref, gkv_ref, wq_ref, wkt_ref, wv_ref, convw_ref, convb_ref,
                 q_ref, k_ref, v_ref, z_ref, xbc_ref, dt_ref, stage_ref):
    i = pl.program_id(0)
    rows_b = x_ref.shape[0]

    @pl.when(i == 0)
    def _():
        stage_ref[...] = jnp.zeros_like(stage_ref)

    conv = convb_ref[...]
    for kk in range(CONV_K):
        off = SUBLANES - (CONV_K - 1) + kk
        conv = conv + convw_ref[kk:kk + 1, :] * stage_ref[off:off + rows_b, :]
    xbc_ref[...] = conv * jax.nn.sigmoid(conv)
    stage_ref[0:SUBLANES, :] = stage_ref[rows_b:rows_b + SUBLANES, :]

    xin = jnp.where(i == 0, meta_ref[...], x_ref[...])
    hn = _rms(xin, g_pre_ref[...]).astype(BF16)

    blk = rope_blk_ref[0]
    cs = blk[0:1, :] * rope_loc_ref[0] + blk[1:2, :] * rope_loc_ref[1]
    low = lax.broadcasted_iota(jnp.int32, cs.shape, 1) < QK_ROPE

    def rope(t):
        prod = t * cs
        return prod + pltpu.roll(prod, QK_ROPE, axis=1)

    z_ref[...] = jnp.dot(hn, wz_ref[...], preferred_element_type=F32)
    stage_ref[SUBLANES:, :] = jnp.dot(hn, wxbc_ref[...], preferred_element_type=F32)

    small = jnp.dot(hn, wsm_ref[...], preferred_element_type=F32)
    dt_ref[...] = small[:, LANES:]
    k_rope_t = jnp.where(low, rope(small[:, :LANES]), 0.0).T.astype(BF16)

    c_q = jnp.dot(hn, wqa_ref[...], preferred_element_type=F32)
    cqn = _rms(c_q, gq_ref[...]).astype(BF16)
    c_kv = jnp.dot(hn, wkva_ref[...], preferred_element_type=F32)
    ckvn = _rms(c_kv, gkv_ref[...])
    k_nope_t = jnp.dot(wkt_ref[...], ckvn.T.astype(BF16), preferred_element_type=F32)
    v_all = jnp.dot(ckvn.astype(BF16), wv_ref[...], preferred_element_type=F32)

    for h in range(ATT_HEADS):
        qh = jnp.dot(cqn, wq_ref[:, h * QK_PAD:(h + 1) * QK_PAD], preferred_element_type=F32)
        q_ref[h, :, :QK_NOPE] = (qh[:, :QK_NOPE] * ATT_SCALE_LOG2).astype(BF16)
        q_ref[h, :, QK_NOPE:] = (rope(qh[:, QK_NOPE:]) * ATT_SCALE_LOG2).astype(BF16)
        k_ref[h, :QK_NOPE, :] = k_nope_t[h * QK_NOPE:(h + 1) * QK_NOPE, :].astype(BF16)
        k_ref[h, QK_NOPE:, :] = k_rope_t
        v_ref[h] = v_all[:, h * V_HEAD:(h + 1) * V_HEAD].astype(BF16)


def _input_projection(x2d, meta_pad, rope_blk, rope_loc, g_pre, wqa, wkva, wz, wxbc, wsm, gq, gkv, wq, wkt, wv,
                      conv_w, conv_b):
    seq = x2d.shape[0]
    n_blocks = seq // ROW_BLOCK + 1
    rows_p = seq + PAD
    pblk = lambda i: jnp.minimum(i, n_blocks - 1)
    rblk = lambda i: jnp.maximum(pblk(i) - 1, 0)
    real = lambda i: (rblk(i), 0)
    real3 = lambda i: (0, rblk(i), 0)
    padded = lambda i: (pblk(i), 0)
    padded3 = lambda i: (0, pblk(i), 0)
    lagged = lambda i: (jnp.maximum(i - 1, 0), 0)
    weights = (g_pre, wqa, wkva, wz, wxbc, wsm, gq, gkv, wq, wkt, wv, conv_w, conv_b)
    return pl.pallas_call(
        _proj_kernel,
        grid=(n_blocks + 1,),
        in_specs=[_const_spec(meta_pad.shape),
                  pl.BlockSpec((ROW_BLOCK, D_MODEL), real),
                  pl.BlockSpec((1, SUBLANES, LANES), lambda i: (pblk(i), 0, 0)),
                  _const_spec(rope_loc.shape)] + [_const_spec(w.shape) for w in weights],
        out_specs=[pl.BlockSpec((ATT_HEADS, ROW_BLOCK, QK_PAD), real3),
                   pl.BlockSpec((ATT_HEADS, QK_PAD, ROW_BLOCK), lambda i: (0, 0, pblk(i))),
                   pl.BlockSpec((ATT_HEADS, ROW_BLOCK, V_HEAD), padded3),
                   pl.BlockSpec((ROW_BLOCK, SSM_WIDTH), real),
                   pl.BlockSpec((ROW_BLOCK, CONV_DIM), lagged),
                   pl.BlockSpec((ROW_BLOCK, LANES), padded)],
        out_shape=[jax.ShapeDtypeStruct((ATT_HEADS, seq, QK_PAD), BF16),
                   jax.ShapeDtypeStruct((ATT_HEADS, QK_PAD, rows_p), BF16),
                   jax.ShapeDtypeStruct((ATT_HEADS, rows_p, V_HEAD), BF16),
                   jax.ShapeDtypeStruct((seq, SSM_WIDTH), F32),
                   jax.ShapeDtypeStruct((rows_p, CONV_DIM), F32),
                   jax.ShapeDtypeStruct((rows_p, LANES), F32)],
        scratch_shapes=[pltpu.VMEM((SUBLANES + ROW_BLOCK, CONV_DIM), F32)],
        compiler_params=pltpu.CompilerParams(dimension_semantics=("arbitrary",),
                                             vmem_limit_bytes=VMEM_LIMIT_BYTES),
        name="input_projection",
    )(meta_pad, x2d, rope_blk, rope_loc, *weights)


def _attn_kernel(q_ref, k_ref, v_ref, o_ref, sm_ref, pm_ref, s_ref, p_ref, m_ref, l_ref, acc_ref):
    i = pl.program_id(1)
    n_tiles = ATT_K // LANES
    assert ATT_Q == ATT_K, "one diagonal chunk per query block"
    assert ATT_UNROLL % 2 == 0 and PAD >= LANES

    def chunk_start(j):
        return pl.multiple_of(PAD + j * ATT_K, math.gcd(PAD, ATT_K))

    def scores(buf, j):
        s_ref[buf] = jnp.dot(q_ref[0], k_ref[0, :, pl.ds(chunk_start(j), ATT_K)], preferred_element_type=F32)

    def score_tiles(buf, last):
        tiles = [s_ref[buf, :, t * LANES:(t + 1) * LANES] for t in range(n_tiles)]
        if last:
            rows = lax.broadcasted_iota(jnp.int32, (ATT_Q, LANES), 0)
            cols = lax.broadcasted_iota(jnp.int32, (ATT_Q, LANES), 1)
            tiles = [jnp.where(cols + t * LANES <= rows, tiles[t], -jnp.inf) for t in range(n_tiles)]
            tiles = [jnp.where(cols >= LANES - N_META, sm_ref[...], -jnp.inf)] + tiles
        return tiles

    def consume(buf, j, last=False):
        tiles = score_tiles(buf, last)
        mx = tiles[0]
        for t in range(1, len(tiles)):
            mx = jnp.maximum(mx, tiles[t])
        m_old = m_ref[...]
        m_new = jnp.maximum(m_old, jnp.max(mx, axis=1, keepdims=True))
        m_ref[...] = m_new
        alpha = jnp.exp2(m_old - m_new)
        tiles = score_tiles(buf, last)
        probs = [jnp.exp2(t - m_new) for t in tiles]
        lsum = probs[0]
        for p in probs[1:]:
            lsum = lsum + p
        l_ref[...] = alpha * l_ref[...] + lsum
        pv = None
        if last:
            pm_ref[...] = probs.pop(0).astype(BF16)
            pv = jnp.dot(pm_ref[...], v_ref[0, PAD - LANES:PAD, :], preferred_element_type=F32)
        for t, p in enumerate(probs):
            p_ref[:, t * LANES:(t + 1) * LANES] = p.astype(BF16)
        pv_chunk = jnp.dot(p_ref[...], v_ref[0, pl.ds(chunk_start(j), ATT_K), :], preferred_element_type=F32)
        acc_ref[...] = alpha * acc_ref[...] + (pv_chunk if pv is None else pv + pv_chunk)

    def run(first, count):
        for u in range(count):
            scores((u + 1) % 2, first + u + 1)
            consume(u % 2, first + u)

    m_ref[...] = jnp.full((ATT_Q, LANES), -jnp.inf, F32)
    l_ref[...] = jnp.zeros((ATT_Q, LANES), F32)
    acc_ref[...] = jnp.zeros((ATT_Q, V_HEAD), F32)
    sm_ref[...] = jnp.dot(q_ref[0], k_ref[0, :, PAD - LANES:PAD], preferred_element_type=F32)
    scores(0, 0)

    def trip(t, carry):
        run(ATT_UNROLL * t, ATT_UNROLL)
        return carry

    lax.fori_loop(0, i // ATT_UNROLL, trip, 0)

    done = i - i % ATT_UNROLL
    size = ATT_UNROLL // 2
    while size >= 2:
        @pl.when((i - done) >= size)
        def _():
            run(done, size)
        done = done + jnp.where((i - done) >= size, size, 0)
        size //= 2

    @pl.when(i - done >= 1)
    def _():
        run(done, 1)

    for parity in range(2):
        @pl.when(i % 2 == parity)
        def _():
            consume(parity, i, last=True)
            o_ref[...] = (acc_ref[...] / jnp.sum(l_ref[...], axis=1, keepdims=True)).astype(o_ref.dtype)


def _attention(q, k, v):
    heads, seq, _ = q.shape
    rows_p = v.shape[1]
    n_q = seq // ATT_Q
    return pl.pallas_call(
        _attn_kernel,
        grid=(heads, n_q),
        in_specs=[pl.BlockSpec((1, ATT_Q, QK_PAD), lambda h, i: (h, i, 0)),
                  pl.BlockSpec((1, QK_PAD, rows_p), lambda h, i: (h, 0, 0)),
                  pl.BlockSpec((1, rows_p, V_HEAD), lambda h, i: (h, 0, 0))],
        out_specs=pl.BlockSpec((ATT_Q, V_HEAD), lambda h, i: (i, h)),
        out_shape=jax.ShapeDtypeStruct((seq, ATT_WIDTH), BF16),
        scratch_shapes=[pltpu.VMEM((ATT_Q, LANES), F32),
                        pltpu.VMEM((ATT_Q, LANES), BF16),
                        pltpu.VMEM((2, ATT_Q, ATT_K), F32),
                        pltpu.VMEM((ATT_Q, ATT_K), BF16),
                        pltpu.VMEM((ATT_Q, LANES), F32),
                        pltpu.VMEM((ATT_Q, LANES), F32),
                        pltpu.VMEM((ATT_Q, V_HEAD), F32)],
        compiler_params=pltpu.CompilerParams(dimension_semantics=("arbitrary", "arbitrary"),
                                             vmem_limit_bytes=VMEM_LIMIT_BYTES),
        name="causal_attention",
    )(q, k, v)


def _split_dot(v, mat, terms):
    out = None
    rem = v
    for t in range(terms):
        piece = rem.astype(BF16)
        part = jnp.dot(piece, mat, preferred_element_type=F32)
        out = part if out is None else out + part
        if t + 1 < terms:
            rem = rem - piece.astype(F32)
    return out


def _ssd_kernel(xc_ref, dt_ref, dtb_ref, alog_ref, dskip_ref, tril_ref, expand_ref, y_ref, state_ref):
    i = pl.program_id(0)
    rows_b = xc_ref.shape[0]

    @pl.when(i == 0)
    def _():
        state_ref[...] = jnp.zeros_like(state_ref)

    xc = xc_ref

    lane = lax.broadcasted_iota(jnp.int32, (SSD_CHUNK, LANES), 1)
    row = lax.broadcasted_iota(jnp.int32, (SSD_CHUNK, LANES), 0)
    causal = lax.broadcasted_iota(jnp.int32, (SSD_CHUNK, SSD_CHUNK), 0) >= lax.broadcasted_iota(
        jnp.int32, (SSD_CHUNK, SSD_CHUNK), 1)
    low_half = lane < SSM_HEAD_DIM
    a_row = -jnp.exp(alog_ref[...])
    tril = tril_ref[...]
    expand = expand_ref[...]
    heads_per_group = SSM_HEADS // SSM_GROUPS

    for c in range(rows_b // SSD_CHUNK):
        r0 = c * SSD_CHUNK
        xs = xc[r0:r0 + SSD_CHUNK, :SSM_WIDTH]
        valid = (lane < SSM_HEADS) & (row + (i * rows_b + r0) >= PAD - N_META)
        dt = jnp.where(valid, jax.nn.softplus(dt_ref[r0:r0 + SSD_CHUNK, :] + dtb_ref[...]), 0.0)
        da = dt * a_row
        acum = _cumsum_rows(tril, da)
        acum_t = acum.T
        dt_t = dt.T
        a_last = acum[SSD_CHUNK - 1:SSD_CHUNK, :]
        e_acum = _split_dot(jnp.exp(acum), expand, 2)
        w_end = _split_dot(jnp.exp(a_last - acum) * dt, expand, 2)
        c_decay = _split_dot(jnp.broadcast_to(jnp.exp(a_last), (SUBLANES, LANES)), expand, 2)[0:1, :]

        y_diag = []
        y_off = []
        for g in range(SSM_GROUPS):
            b_g = xc[r0:r0 + SSD_CHUNK, SSM_WIDTH + g * SSM_STATE:SSM_WIDTH + (g + 1) * SSM_STATE]
            c_g = xc[r0:r0 + SSD_CHUNK, SSM_WIDTH + (SSM_GROUPS + g) * SSM_STATE:
                     SSM_WIDTH + (SSM_GROUPS + g + 1) * SSM_STATE].astype(BF16)
            b_t = b_g.T.astype(BF16)
            cb = jnp.dot(c_g, b_t, preferred_element_type=F32)
            gcols = slice(g * GROUP_WIDTH, (g + 1) * GROUP_WIDTH)
            for jp in range(heads_per_group // 2):
                ws = []
                for r in (g * heads_per_group + 2 * jp, g * heads_per_group + 2 * jp + 1):
                    seg = acum[:, r:r + 1] - acum_t[r:r + 1, :]
                    decay = jnp.where(causal, jnp.exp(seg), 0.0)
                    ws.append((cb * decay * dt_t[r:r + 1, :]).astype(BF16))
                col = g * GROUP_WIDTH + jp * LANES
                x_pair = xs[:, col:col + LANES]
                x_cat = jnp.concatenate([jnp.where(low_half, x_pair, 0.0), jnp.where(low_half, 0.0, x_pair)],
                                        axis=0).astype(BF16)
                y_diag.append(jnp.dot(jnp.concatenate(ws, axis=1), x_cat, preferred_element_type=F32))
            h_prev = state_ref[g]
            y_off.append(jnp.dot(c_g, h_prev.astype(BF16), preferred_element_type=F32))
            x_end = (xs[:, gcols] * w_end[:, gcols]).astype(BF16)
            state_ref[g] = c_decay[:, gcols] * h_prev + jnp.dot(b_t, x_end, preferred_element_type=F32)
        y = (jnp.concatenate(y_diag, axis=1) + jnp.concatenate(y_off, axis=1) * e_acum
             + xs * dskip_ref[...])
        y_ref[r0:r0 + SSD_CHUNK, :] = y


def _cumsum_rows(tril, v):
    out = None
    rem = v
    for t in range(3):
        piece = rem.astype(BF16)
        part = jnp.dot(tril, piece, preferred_element_type=F32)
        out = part if out is None else out + part
        if t < 2:
            rem = rem - piece.astype(F32)
    return out


def _ssd_scan(xc, dt_raw, dt_bias, a_log, d_skip_x, tril, expand, seq):
    n_blocks = seq // ROW_BLOCK + 1
    consts = (dt_bias, a_log, d_skip_x, tril, expand)
    return pl.pallas_call(
        _ssd_kernel,
        grid=(n_blocks,),
        in_specs=[pl.BlockSpec((ROW_BLOCK, CONV_DIM), lambda i: (i, 0)),
                  pl.BlockSpec((ROW_BLOCK, LANES), lambda i: (i, 0))] + [_const_spec(c.shape) for c in consts],
        out_specs=pl.BlockSpec((ROW_BLOCK, SSM_WIDTH), lambda i: (jnp.maximum(i - 1, 0), 0)),
        out_shape=jax.ShapeDtypeStruct((seq, SSM_WIDTH), F32),
        scratch_shapes=[pltpu.VMEM((SSM_GROUPS, SSM_STATE, GROUP_WIDTH), F32)],
        compiler_params=pltpu.CompilerParams(dimension_semantics=("arbitrary",),
                                             vmem_limit_bytes=VMEM_LIMIT_BYTES),
        name="ssd_scan",
    )(xc, dt_raw, *consts)


def _out_mlp_kernel(att_ref, y_ref, z_ref, x_ref, g_ssm_ref, wout_ref, g_post_ref, g_mpre_ref,
                    wup_ref, wdown_ref, g_mpost_ref, o_ref):
    z = z_ref[...]
    gated = y_ref[...] * (z * jax.nn.sigmoid(z))
    parts = []
    for g in range(SSM_GROUPS):
        gg = gated[:, g * GROUP_WIDTH:(g + 1) * GROUP_WIDTH]
        parts.append(gg * lax.rsqrt(jnp.mean(jnp.square(gg), axis=-1, keepdims=True) + EPS))
    ssm = (jnp.concatenate(parts, axis=1) * g_ssm_ref[...]).astype(BF16)
    mix = (jnp.dot(att_ref[...], wout_ref[:ATT_WIDTH, :], preferred_element_type=F32)
           + jnp.dot(ssm, wout_ref[ATT_WIDTH:, :], preferred_element_type=F32))
    h1 = x_ref[...] + _rms(mix, g_post_ref[...])
    hn = _rms(h1, g_mpre_ref[...]).astype(BF16)
    f = None
    for c in range(D_FF // FF_CHUNK):
        u = jnp.dot(hn, wup_ref[:, c * FF_CHUNK:(c + 1) * FF_CHUNK], preferred_element_type=F32)
        u = jnp.square(jnp.maximum(u, 0.0)).astype(BF16)
        part = jnp.dot(u, wdown_ref[c * FF_CHUNK:(c + 1) * FF_CHUNK, :], preferred_element_type=F32)
        f = part if f is None else f + part
    o_ref[...] = h1 + _rms(f, g_mpost_ref[...])


def _out_mlp(att, y, z, x2d, g_ssm, wout, g_post, g_mpre, wup, wdown, g_mpost):
    seq = x2d.shape[0]
    weights = (g_ssm, wout, g_post, g_mpre, wup, wdown, g_mpost)
    row = lambda i: (i, 0)
    return pl.pallas_call(
        _out_mlp_kernel,
        grid=(seq // MLP_BLOCK,),
        in_specs=[pl.BlockSpec((MLP_BLOCK, ATT_WIDTH), row),
                  pl.BlockSpec((MLP_BLOCK, SSM_WIDTH), row),
                  pl.BlockSpec((MLP_BLOCK, SSM_WIDTH), row),
                  pl.BlockSpec((MLP_BLOCK, D_MODEL), row)] + [_const_spec(w.shape) for w in weights],
        out_specs=pl.BlockSpec((MLP_BLOCK, D_MODEL), row),
        out_shape=jax.ShapeDtypeStruct((seq, D_MODEL), F32),
        compiler_params=pltpu.CompilerParams(dimension_semantics=("arbitrary",),
                                             vmem_limit_bytes=VMEM_LIMIT_BYTES),
        name="out_proj_mlp",
    )(att, y, z, x2d, *weights)


def _rot_cols(w):
    half = QK_ROPE // 2
    return jnp.concatenate([-w[..., half:], w[..., :half]], axis=-1)


def _lane_pad(v, width=LANES):
    return jnp.pad(v, [(0, 0)] * (v.ndim - 1) + [(0, width - v.shape[-1])])


def kernel(x, meta_tokens, norm_mix_pre, w_in, q_a_norm, w_q_up, kv_a_norm, w_kv_up, conv_w, conv_b, dt_bias,
           a_log, d_skip, ssm_norm, w_out, norm_mix_post, norm_mlp_pre, w_mlp_up, w_mlp_down, norm_mlp_post):
    bsz, seq, _ = x.shape
    assert bsz == 1 and norm_mix_pre.shape[0] == 1, "single sequence, single layer"
    assert seq % ATT_Q == 0 and seq % ROW_BLOCK == 0 and seq % MLP_BLOCK == 0
    assert PAD == ROW_BLOCK, "the front pad is exactly the first block of the projection / SSD grids"
    x2d = x[0]
    row = lambda v: v.reshape(1, -1)

    o_ckv, o_kr, o_z, o_xbc, o_dt = Q_LORA, Q_LORA + KV_LORA, Q_LORA + KV_LORA + QK_ROPE, \
        Q_LORA + KV_LORA + QK_ROPE + SSM_WIDTH, Q_LORA + KV_LORA + QK_ROPE + SSM_WIDTH + CONV_DIM
    w_in0 = w_in[0]
    w_kr = w_in0[:, o_kr:o_z]
    wsm = jnp.concatenate([w_kr, _rot_cols(w_kr), _lane_pad(w_in0[:, o_dt:])], axis=1)
    wq3 = w_q_up[0].reshape(Q_LORA, ATT_HEADS, QK_DIM)
    wq = jnp.concatenate([wq3, _rot_cols(wq3[..., QK_NOPE:])], axis=-1).reshape(Q_LORA, ATT_HEADS * QK_PAD)
    wkv3 = w_kv_up[0].reshape(KV_LORA, ATT_HEADS, QK_NOPE + V_HEAD)
    wkt = wkv3[..., :QK_NOPE].reshape(KV_LORA, ATT_HEADS * QK_NOPE).T
    wv = wkv3[..., QK_NOPE:].reshape(KV_LORA, ATT_HEADS * V_HEAD)

    inv_freq = ROPE_THETA ** (-jnp.arange(0, QK_ROPE, 2, dtype=F32) / QK_ROPE)
    freq = jnp.tile(inv_freq, 4)
    low = jnp.arange(LANES) < QK_ROPE
    base = (jnp.arange(seq // ROW_BLOCK + 1) * ROW_BLOCK - (PAD - N_META)).astype(F32)
    ang_b = base[:, None] * freq[None, :]
    rope_blk = jnp.stack([jnp.cos(ang_b), jnp.where(low, -jnp.sin(ang_b), jnp.sin(ang_b))], axis=1)
    rope_blk = jnp.pad(rope_blk, ((0, 0), (0, SUBLANES - 2), (0, 0)))
    ang_l = jnp.arange(ROW_BLOCK, dtype=F32)[:, None] * freq[None, :]
    rope_loc = jnp.stack([jnp.where(low, jnp.cos(ang_l), jnp.sin(ang_l)),
                          jnp.where(low, jnp.sin(ang_l), jnp.cos(ang_l))])

    meta_pad = jnp.concatenate([jnp.zeros((PAD - N_META, D_MODEL), F32), meta_tokens.astype(F32)], axis=0)

    q, k, v, z, xc, dt_raw = _input_projection(
        x2d, meta_pad, rope_blk, rope_loc, row(norm_mix_pre[0]),
        w_in0[:, :o_ckv].astype(BF16), w_in0[:, o_ckv:o_kr].astype(BF16), w_in0[:, o_z:o_xbc].astype(BF16),
        w_in0[:, o_xbc:o_dt].astype(BF16), wsm.astype(BF16),
        row(q_a_norm[0]), row(kv_a_norm[0]), wq.astype(BF16), wkt.astype(BF16), wv.astype(BF16),
        conv_w[0], row(conv_b[0]))

    att = _attention(q, k, v)

    tril = jnp.tril(jnp.ones((SSD_CHUNK, SSD_CHUNK), BF16))
    expand = (jnp.arange(LANES)[:, None] == jnp.arange(SSM_WIDTH)[None, :] // SSM_HEAD_DIM).astype(BF16)
    y = _ssd_scan(xc, dt_raw, _lane_pad(row(dt_bias[0])), _lane_pad(row(a_log[0])),
                  row(jnp.repeat(d_skip[0], SSM_HEAD_DIM)), tril, expand, seq)

    out = _out_mlp(att, y, z, x2d, row(ssm_norm[0]), w_out[0].astype(BF16), row(norm_mix_post[0]),
                   row(norm_mlp_pre[0]), w_mlp_up[0].astype(BF16), w_mlp_down[0].astype(BF16),
                   row(norm_mlp_post[0]))
    return out[None]
```

```python
import math

import jax
import jax.numpy as jnp
from jax import lax
from jax.experimental import pallas as pl
from jax.experimental.pallas import tpu as pltpu

D_MODEL = 1024
N_META = 16
EPS = 1e-6
ATT_HEADS = 8
Q_LORA = 384
KV_LORA = 256
QK_NOPE = 128
QK_ROPE = 64
V_HEAD = 128
QK_DIM = QK_NOPE + QK_ROPE
ROPE_THETA = 10000.0
ATT_WIDTH = ATT_HEADS * V_HEAD
SSM_HEADS = 16
SSM_HEAD_DIM = 64
SSM_WIDTH = SSM_HEADS * SSM_HEAD_DIM
SSM_GROUPS = 2
SSM_STATE = 128
GROUP_WIDTH = SSM_WIDTH // SSM_GROUPS
CONV_K = 4
CONV_DIM = SSM_WIDTH + 2 * SSM_GROUPS * SSM_STATE
D_FF = 4 * D_MODEL

LANES = 128
SUBLANES = 8
VMEM_LIMIT_BYTES = 56 * 1024 * 1024

QK_PAD = 2 * LANES
V_EXT = 2 * LANES
PAD = 256
ROW_BLOCK = 256
MLP_BLOCK = 512
ATT_Q = 512
ATT_K = 512
ATT_UNROLL = 8
SSD_CHUNK = 128
FF_CHUNK = 1024
ATT_SCALE_LOG2 = (QK_DIM ** -0.5) * math.log2(math.e)

BF16 = jnp.bfloat16
F32 = jnp.float32


def _const_spec(shape):
    nd = len(shape)
    return pl.BlockSpec(shape, lambda *_: (0,) * nd, pipeline_mode=pl.Buffered(1))


def _rms(x, gain):
    return x * lax.rsqrt(jnp.mean(jnp.square(x), axis=-1, keepdims=True) + EPS) * gain


def _proj_kernel(meta_ref, x_ref, rope_blk_ref, rope_loc_ref, g_pre_ref, wqa_ref, wkva_ref, wz_ref, wxbc_ref,
                 wsm_ref, gq_ref, gkv_ref, wq_ref, wkt_ref, wv_ref, convw_ref, convb_ref,
                 q_ref, k_ref, v_ref, z_ref, xbc_ref, dt_ref, stage_ref):
    i = pl.program_id(0)
    rows_b = x_ref.shape[0]

    @pl.when(i == 0)
    def _():
        stage_ref[...] = jnp.zeros_like(stage_ref)

    conv = convb_ref[...]
    for kk in range(CONV_K):
        off = SUBLANES - (CONV_K - 1) + kk
        conv = conv + convw_ref[kk:kk + 1, :] * stage_ref[off:off + rows_b, :]
    xbc_ref[...] = conv * jax.nn.sigmoid(conv)
    stage_ref[0:SUBLANES, :] = stage_ref[rows_b:rows_b + SUBLANES, :]

    xin = jnp.where(i == 0, meta_ref[...], x_ref[...])
    hn = _rms(xin, g_pre_ref[...]).astype(BF16)

    blk = rope_blk_ref[0]
    cs = blk[0:1, :] * rope_loc_ref[0] + blk[1:2, :] * rope_loc_ref[1]
    low = lax.broadcasted_iota(jnp.int32, cs.shape, 1) < QK_ROPE

    def rope(t):
        prod = t * cs
        return prod + pltpu.roll(prod, QK_ROPE, axis=1)

    z_ref[...] = jnp.dot(hn, wz_ref[...], preferred_element_type=F32)
    stage_ref[SUBLANES:, :] = jnp.dot(hn, wxbc_ref[...], preferred_element_type=F32)

    small = jnp.dot(hn, wsm_ref[...], preferred_element_type=F32)
    dt_ref[...] = small[:, LANES:]
    k_rope_t = jnp.where(low, rope(small[:, :LANES]), 0.0).T.astype(BF16)

    c_q = jnp.dot(hn, wqa_ref[...], preferred_element_type=F32)
    cqn = _rms(c_q, gq_ref[...]).astype(BF16)
    c_kv = jnp.dot(hn, wkva_ref[...], preferred_element_type=F32)
    ckvn = _rms(c_kv, gkv_ref[...])
    k_nope_t = jnp.dot(wkt_ref[...], ckvn.T.astype(BF16), preferred_element_type=F32)
    v_all = jnp.dot(ckvn.astype(BF16), wv_ref[...], preferred_element_type=F32)
    ones_col = jnp.where(lax.broadcasted_iota(jnp.int32, cs.shape, 1) == 0, 1.0, 0.0).astype(BF16)

    for h in range(ATT_HEADS):
        qh = jnp.dot(cqn, wq_ref[:, h * QK_PAD:(h + 1) * QK_PAD], preferred_element_type=F32)
        q_ref[h, :, :QK_NOPE] = (qh[:, :QK_NOPE] * ATT_SCALE_LOG2).astype(BF16)
        q_ref[h, :, QK_NOPE:] = (rope(qh[:, QK_NOPE:]) * ATT_SCALE_LOG2).astype(BF16)
        k_ref[h, :QK_NOPE, :] = k_nope_t[h * QK_NOPE:(h + 1) * QK_NOPE, :].astype(BF16)
        k_ref[h, QK_NOPE:, :] = k_rope_t
        v_ref[h, :, :V_HEAD] = v_all[:, h * V_HEAD:(h + 1) * V_HEAD].astype(BF16)
        v_ref[h, :, V_HEAD:] = ones_col


def _input_projection(x2d, meta_pad, rope_blk, rope_loc, g_pre, wqa, wkva, wz, wxbc, wsm, gq, gkv, wq, wkt, wv,
                      conv_w, conv_b):
    seq = x2d.shape[0]
    n_blocks = seq // ROW_BLOCK + 1
    rows_p = seq + PAD
    pblk = lambda i: jnp.minimum(i, n_blocks - 1)
    rblk = lambda i: jnp.maximum(pblk(i) - 1, 0)
    real = lambda i: (rblk(i), 0)
    real3 = lambda i: (0, rblk(i), 0)
    padded = lambda i: (pblk(i), 0)
    padded3 = lambda i: (0, pblk(i), 0)
    lagged = lambda i: (jnp.maximum(i - 1, 0), 0)
    weights = (g_pre, wqa, wkva, wz, wxbc, wsm, gq, gkv, wq, wkt, wv, conv_w, conv_b)
    return pl.pallas_call(
        _proj_kernel,
        grid=(n_blocks + 1,),
        in_specs=[_const_spec(meta_pad.shape),
                  pl.BlockSpec((ROW_BLOCK, D_MODEL), real),
                  pl.BlockSpec((1, SUBLANES, LANES), lambda i: (pblk(i), 0, 0)),
                  _const_spec(rope_loc.shape)] + [_const_spec(w.shape) for w in weights],
        out_specs=[pl.BlockSpec((ATT_HEADS, ROW_BLOCK, QK_PAD), real3),
                   pl.BlockSpec((ATT_HEADS, QK_PAD, ROW_BLOCK), lambda i: (0, 0, pblk(i))),
                   pl.BlockSpec((ATT_HEADS, ROW_BLOCK, V_EXT), padded3),
                   pl.BlockSpec((ROW_BLOCK, SSM_WIDTH), real),
                   pl.BlockSpec((ROW_BLOCK, CONV_DIM), lagged),
                   pl.BlockSpec((ROW_BLOCK, LANES), padded)],
        out_shape=[jax.ShapeDtypeStruct((ATT_HEADS, seq, QK_PAD), BF16),
                   jax.ShapeDtypeStruct((ATT_HEADS, QK_PAD, rows_p), BF16),
                   jax.ShapeDtypeStruct((ATT_HEADS, rows_p, V_EXT), BF16),
                   jax.ShapeDtypeStruct((seq, SSM_WIDTH), F32),
                   jax.ShapeDtypeStruct((rows_p, CONV_DIM), F32),
                   jax.ShapeDtypeStruct((rows_p, LANES), F32)],
        scratch_shapes=[pltpu.VMEM((SUBLANES + ROW_BLOCK, CONV_DIM), F32)],
        compiler_params=pltpu.CompilerParams(dimension_semantics=("arbitrary",),
                                             vmem_limit_bytes=VMEM_LIMIT_BYTES),
        name="input_projection",
    )(meta_pad, x2d, rope_blk, rope_loc, *weights)


def _attn_kernel(q_ref, k_ref, v_ref, o_ref, sm_ref, pm_ref, s_ref, p_ref, m_ref, acc_ref):
    i = pl.program_id(1)
    n_tiles = ATT_K // LANES
    assert ATT_Q == ATT_K, "one diagonal chunk per query block"
    assert ATT_UNROLL % 2 == 0 and PAD >= LANES

    def chunk_start(j):
        return pl.multiple_of(PAD + j * ATT_K, math.gcd(PAD, ATT_K))

    def scores(buf, j):
        s_ref[buf] = jnp.dot(q_ref[0], k_ref[0, :, pl.ds(chunk_start(j), ATT_K)], preferred_element_type=F32)

    def score_tiles(buf, last):
        tiles = [s_ref[buf, :, t * LANES:(t + 1) * LANES] for t in range(n_tiles)]
        if last:
            rows = lax.broadcasted_iota(jnp.int32, (ATT_Q, LANES), 0)
            cols = lax.broadcasted_iota(jnp.int32, (ATT_Q, LANES), 1)
            tiles = [jnp.where(cols + t * LANES <= rows, tiles[t], -jnp.inf) for t in range(n_tiles)]
            tiles = [jnp.where(cols >= LANES - N_META, sm_ref[...], -jnp.inf)] + tiles
        return tiles

    def consume(buf, j, last=False):
        tiles = score_tiles(buf, last)
        mx = tiles[0]
        for t in range(1, len(tiles)):
            mx = jnp.maximum(mx, tiles[t])
        m_old = m_ref[...]
        m_new = jnp.maximum(m_old, jnp.max(mx, axis=1, keepdims=True))
        m_ref[...] = m_new
        alpha = jnp.exp2(m_old - m_new)
        tiles = score_tiles(buf, last)
        probs = [jnp.exp2((t - m_new).astype(BF16)) for t in tiles]
        pv = None
        if last:
            pm_ref[...] = probs.pop(0)
            pv = jnp.dot(pm_ref[...], v_ref[0, PAD - LANES:PAD, :], preferred_element_type=F32)
        for t, p in enumerate(probs):
            p_ref[:, t * LANES:(t + 1) * LANES] = p
        pv_chunk = jnp.dot(p_ref[...], v_ref[0, pl.ds(chunk_start(j), ATT_K), :], preferred_element_type=F32)
        alpha2 = jnp.concatenate([alpha, alpha], axis=1)
        acc_ref[...] = alpha2 * acc_ref[...] + (pv_chunk if pv is None else pv + pv_chunk)

    def run(first, count):
        for u in range(count):
            scores((u + 1) % 2, first + u + 1)
            consume(u % 2, first + u)

    m_ref[...] = jnp.full((ATT_Q, LANES), -jnp.inf, F32)
    acc_ref[...] = jnp.zeros((ATT_Q, V_EXT), F32)
    sm_ref[...] = jnp.dot(q_ref[0], k_ref[0, :, PAD - LANES:PAD], preferred_element_type=F32)
    scores(0, 0)

    def trip(t, carry):
        run(ATT_UNROLL * t, ATT_UNROLL)
        return carry

    lax.fori_loop(0, i // ATT_UNROLL, trip, 0)

    done = i - i % ATT_UNROLL
    size = ATT_UNROLL // 2
    while size >= 2:
        @pl.when((i - done) >= size)
        def _():
            run(done, size)
        done = done + jnp.where((i - done) >= size, size, 0)
        size //= 2

    @pl.when(i - done >= 1)
    def _():
        run(done, 1)

    for parity in range(2):
        @pl.when(i % 2 == parity)
        def _():
            consume(parity, i, last=True)
            o_ref[...] = (acc_ref[:, :V_HEAD] / acc_ref[:, V_HEAD:V_HEAD + 1]).astype(o_ref.dtype)


def _attention(q, k, v):
    heads, seq, _ = q.shape
    rows_p = v.shape[1]
    n_q = seq // ATT_Q
    return pl.pallas_call(
        _attn_kernel,
        grid=(heads, n_q),
        in_specs=[pl.BlockSpec((1, ATT_Q, QK_PAD), lambda h, i: (h, i, 0)),
                  pl.BlockSpec((1, QK_PAD, rows_p), lambda h, i: (h, 0, 0)),
                  pl.BlockSpec((1, rows_p, V_EXT), lambda h, i: (h, 0, 0))],
        out_specs=pl.BlockSpec((ATT_Q, V_HEAD), lambda h, i: (i, h)),
        out_shape=jax.ShapeDtypeStruct((seq, ATT_WIDTH), BF16),
        scratch_shapes=[pltpu.VMEM((ATT_Q, LANES), F32),
                        pltpu.VMEM((ATT_Q, LANES), BF16),
                        pltpu.VMEM((2, ATT_Q, ATT_K), F32),
                        pltpu.VMEM((ATT_Q, ATT_K), BF16),
                        pltpu.VMEM((ATT_Q, LANES), F32),
                        pltpu.VMEM((ATT_Q, V_EXT), F32)],
        compiler_params=pltpu.CompilerParams(dimension_semantics=("arbitrary", "arbitrary"),
                                             vmem_limit_bytes=VMEM_LIMIT_BYTES),
        name="causal_attention",
    )(q, k, v)


def _split_dot(v, mat, terms):
    out = None
    rem = v
    for t in range(terms):
        piece = rem.astype(BF16)
        part = jnp.dot(piece, mat, preferred_element_type=F32)
        out = part if out is None else out + part
        if t + 1 < terms:
            rem = rem - piece.astype(F32)
    return out


def _ssd_kernel(xc_ref, dt_ref, dtb_ref, alog_ref, dskip_ref, tril_ref, expand_ref, y_ref, state_ref):
    i = pl.program_id(0)
    rows_b = xc_ref.shape[0]

    @pl.when(i == 0)
    def _():
        state_ref[...] = jnp.zeros_like(state_ref)

    xc = xc_ref

    lane = lax.broadcasted_iota(jnp.int32, (SSD_CHUNK, LANES), 1)
    row = lax.broadcasted_iota(jnp.int32, (SSD_CHUNK, LANES), 0)
    causal = lax.broadcasted_iota(jnp.int32, (SSD_CHUNK, SSD_CHUNK), 0) >= lax.broadcasted_iota(
        jnp.int32, (SSD_CHUNK, SSD_CHUNK), 1)
    low_half = lane < SSM_HEAD_DIM
    a_row = -jnp.exp(alog_ref[...])
    tril = tril_ref[...]
    expand = expand_ref[...]
    heads_per_group = SSM_HEADS // SSM_GROUPS

    for c in range(rows_b // SSD_CHUNK):
        r0 = c * SSD_CHUNK
        xs = xc[r0:r0 + SSD_CHUNK, :SSM_WIDTH]
        valid = (lane < SSM_HEADS) & (row + (i * rows_b + r0) >= PAD - N_META)
        dt = jnp.where(valid, jax.nn.softplus(dt_ref[r0:r0 + SSD_CHUNK, :] + dtb_ref[...]), 0.0)
        da = dt * a_row
        acum = _cumsum_rows(tril, da)
        acum_t = acum.T
        dt_t = dt.T
        a_last = acum[SSD_CHUNK - 1:SSD_CHUNK, :]
        e_acum = _split_dot(jnp.exp(acum), expand, 2)
        w_end = _split_dot(jnp.exp(a_last - acum) * dt, expand, 2)
        c_decay = _split_dot(jnp.broadcast_to(jnp.exp(a_last), (SUBLANES, LANES)), expand, 2)[0:1, :]

        y_diag = []
        y_off = []
        for g in range(SSM_GROUPS):
            b_g = xc[r0:r0 + SSD_CHUNK, SSM_WIDTH + g * SSM_STATE:SSM_WIDTH + (g + 1) * SSM_STATE]
            c_g = xc[r0:r0 + SSD_CHUNK, SSM_WIDTH + (SSM_GROUPS + g) * SSM_STATE:
                     SSM_WIDTH + (SSM_GROUPS + g + 1) * SSM_STATE].astype(BF16)
            b_t = b_g.T.astype(BF16)
            cb = jnp.dot(c_g, b_t, preferred_element_type=F32)
            gcols = slice(g * GROUP_WIDTH, (g + 1) * GROUP_WIDTH)
            for jp in range(heads_per_group // 2):
                ws = []
                for r in (g * heads_per_group + 2 * jp, g * heads_per_group + 2 * jp + 1):
                    seg = acum[:, r:r + 1] - acum_t[r:r + 1, :]
                    decay = jnp.where(causal, jnp.exp(seg), 0.0)
                    ws.append((cb * decay * dt_t[r:r + 1, :]).astype(BF16))
                col = g * GROUP_WIDTH + jp * LANES
                x_pair = xs[:, col:col + LANES]
                x_cat = jnp.concatenate([jnp.where(low_half, x_pair, 0.0), jnp.where(low_half, 0.0, x_pair)],
                                        axis=0).astype(BF16)
                y_diag.append(jnp.dot(jnp.concatenate(ws, axis=1), x_cat, preferred_element_type=F32))
            h_prev = state_ref[g]
            y_off.append(jnp.dot(c_g, h_prev.astype(BF16), preferred_element_type=F32))
            x_end = (xs[:, gcols] * w_end[:, gcols]).astype(BF16)
            state_ref[g] = c_decay[:, gcols] * h_prev + jnp.dot(b_t, x_end, preferred_element_type=F32)
        y = (jnp.concatenate(y_diag, axis=1) + jnp.concatenate(y_off, axis=1) * e_acum
             + xs * dskip_ref[...])
        y_ref[r0:r0 + SSD_CHUNK, :] = y


def _cumsum_rows(tril, v):
    out = None
    rem = v
    for t in range(3):
        piece = rem.astype(BF16)
        part = jnp.dot(tril, piece, preferred_element_type=F32)
        out = part if out is None else out + part
        if t < 2:
            rem = rem - piece.astype(F32)
    return out


def _ssd_scan(xc, dt_raw, dt_bias, a_log, d_skip_x, tril, expand, seq):
    n_blocks = seq // ROW_BLOCK + 1
    consts = (dt_bias, a_log, d_skip_x, tril, expand)
    return pl.pallas_call(
        _ssd_kernel,
        grid=(n_blocks,),
        in_specs=[pl.BlockSpec((ROW_BLOCK, CONV_DIM), lambda i: (i, 0)),
                  pl.BlockSpec((ROW_BLOCK, LANES), lambda i: (i, 0))] + [_const_spec(c.shape) for c in consts],
        out_specs=pl.BlockSpec((ROW_BLOCK, SSM_WIDTH), lambda i: (jnp.maximum(i - 1, 0), 0)),
        out_shape=jax.ShapeDtypeStruct((seq, SSM_WIDTH), F32),
        scratch_shapes=[pltpu.VMEM((SSM_GROUPS, SSM_STATE, GROUP_WIDTH), F32)],
        compiler_params=pltpu.CompilerParams(dimension_semantics=("arbitrary",),
                                             vmem_limit_bytes=VMEM_LIMIT_BYTES),
        name="ssd_scan",
    )(xc, dt_raw, *consts)


def _out_mlp_kernel(att_ref, y_ref, z_ref, x_ref, g_ssm_ref, wout_ref, g_post_ref, g_mpre_ref,
                    wup_ref, wdown_ref, g_mpost_ref, o_ref):
    z = z_ref[...]
    gated = y_ref[...] * (z * jax.nn.sigmoid(z))
    parts = []
    for g in range(SSM_GROUPS):
        gg = gated[:, g * GROUP_WIDTH:(g + 1) * GROUP_WIDTH]
        parts.append(gg * lax.rsqrt(jnp.mean(jnp.square(gg), axis=-1, keepdims=True) + EPS))
    ssm = (jnp.concatenate(parts, axis=1) * g_ssm_ref[...]).astype(BF16)
    mix = (jnp.dot(att_ref[...], wout_ref[:ATT_WIDTH, :], preferred_element_type=F32)
           + jnp.dot(ssm, wout_ref[ATT_WIDTH:, :], preferred_element_type=F32))
    h1 = x_ref[...] + _rms(mix, g_post_ref[...])
    hn = _rms(h1, g_mpre_ref[...]).astype(BF16)
    f = None
    for c in range(D_FF // FF_CHUNK):
        u = jnp.dot(hn, wup_ref[:, c * FF_CHUNK:(c + 1) * FF_CHUNK], preferred_element_type=F32)
        u = jnp.square(jnp.maximum(u, 0.0)).astype(BF16)
        part = jnp.dot(u, wdown_ref[c * FF_CHUNK:(c + 1) * FF_CHUNK, :], preferred_element_type=F32)
        f = part if f is None else f + part
    o_ref[...] = h1 + _rms(f, g_mpost_ref[...])


def _out_mlp(att, y, z, x2d, g_ssm, wout, g_post, g_mpre, wup, wdown, g_mpost):
    seq = x2d.shape[0]
    weights = (g_ssm, wout, g_post, g_mpre, wup, wdown, g_mpost)
    row = lambda i: (i, 0)
    return pl.pallas_call(
        _out_mlp_kernel,
        grid=(seq // MLP_BLOCK,),
        in_specs=[pl.BlockSpec((MLP_BLOCK, ATT_WIDTH), row),
                  pl.BlockSpec((MLP_BLOCK, SSM_WIDTH), row),
                  pl.BlockSpec((MLP_BLOCK, SSM_WIDTH), row),
                  pl.BlockSpec((MLP_BLOCK, D_MODEL), row)] + [_const_spec(w.shape) for w in weights],
        out_specs=pl.BlockSpec((MLP_BLOCK, D_MODEL), row),
        out_shape=jax.ShapeDtypeStruct((seq, D_MODEL), F32),
        compiler_params=pltpu.CompilerParams(dimension_semantics=("arbitrary",),
                                             vmem_limit_bytes=VMEM_LIMIT_BYTES),
        name="out_proj_mlp",
    )(att, y, z, x2d, *weights)


def _rot_cols(w):
    half = QK_ROPE // 2
    return jnp.concatenate([-w[..., half:], w[..., :half]], axis=-1)


def _lane_pad(v, width=LANES):
    return jnp.pad(v, [(0, 0)] * (v.ndim - 1) + [(0, width - v.shape[-1])])


def kernel(x, meta_tokens, norm_mix_pre, w_in, q_a_norm, w_q_up, kv_a_norm, w_kv_up, conv_w, conv_b, dt_bias,
           a_log, d_skip, ssm_norm, w_out, norm_mix_post, norm_mlp_pre, w_mlp_up, w_mlp_down, norm_mlp_post):
    bsz, seq, _ = x.shape
    assert bsz == 1 and norm_mix_pre.shape[0] == 1, "single sequence, single layer"
    assert seq % ATT_Q == 0 and seq % ROW_BLOCK == 0 and seq % MLP_BLOCK == 0
    assert PAD == ROW_BLOCK, "the front pad is exactly the first block of the projection / SSD grids"
    x2d = x[0]
    row = lambda v: v.reshape(1, -1)

    o_ckv, o_kr, o_z, o_xbc, o_dt = Q_LORA, Q_LORA + KV_LORA, Q_LORA + KV_LORA + QK_ROPE, \
        Q_LORA + KV_LORA + QK_ROPE + SSM_WIDTH, Q_LORA + KV_LORA + QK_ROPE + SSM_WIDTH + CONV_DIM
    w_in0 = w_in[0]
    w_kr = w_in0[:, o_kr:o_z]
    wsm = jnp.concatenate([w_kr, _rot_cols(w_kr), _lane_pad(w_in0[:, o_dt:])], axis=1)
    wq3 = w_q_up[0].reshape(Q_LORA, ATT_HEADS, QK_DIM)
    wq = jnp.concatenate([wq3, _rot_cols(wq3[..., QK_NOPE:])], axis=-1).reshape(Q_LORA, ATT_HEADS * QK_PAD)
    wkv3 = w_kv_up[0].reshape(KV_LORA, ATT_HEADS, QK_NOPE + V_HEAD)
    wkt = wkv3[..., :QK_NOPE].reshape(KV_LORA, ATT_HEADS * QK_NOPE).T
    wv = wkv3[..., QK_NOPE:].reshape(KV_LORA, ATT_HEADS * V_HEAD)

    inv_freq = ROPE_THETA ** (-jnp.arange(0, QK_ROPE, 2, dtype=F32) / QK_ROPE)
    freq = jnp.tile(inv_freq, 4)
    low = jnp.arange(LANES) < QK_ROPE
    base = (jnp.arange(seq // ROW_BLOCK + 1) * ROW_BLOCK - (PAD - N_META)).astype(F32)
    ang_b = base[:, None] * freq[None, :]
    rope_blk = jnp.stack([jnp.cos(ang_b), jnp.where(low, -jnp.sin(ang_b), jnp.sin(ang_b))], axis=1)
    rope_blk = jnp.pad(rope_blk, ((0, 0), (0, SUBLANES - 2), (0, 0)))
    ang_l = jnp.arange(ROW_BLOCK, dtype=F32)[:, None] * freq[None, :]
    rope_loc = jnp.stack([jnp.where(low, jnp.cos(ang_l), jnp.sin(ang_l)),
                          jnp.where(low, jnp.sin(ang_l), jnp.cos(ang_l))])

    meta_pad = jnp.concatenate([jnp.zeros((PAD - N_META, D_MODEL), F32), meta_tokens.astype(F32)], axis=0)

    q, k, v, z, xc, dt_raw = _input_projection(
        x2d, meta_pad, rope_blk, rope_loc, row(norm_mix_pre[0]),
        w_in0[:, :o_ckv].astype(BF16), w_in0[:, o_ckv:o_kr].astype(BF16), w_in0[:, o_z:o_xbc].astype(BF16),
        w_in0[:, o_xbc:o_dt].astype(BF16), wsm.astype(BF16),
        row(q_a_norm[0]), row(kv_a_norm[0]), wq.astype(BF16), wkt.astype(BF16), wv.astype(BF16),
        conv_w[0], row(conv_b[0]))

    att = _attention(q, k, v)

    tril = jnp.tril(jnp.ones((SSD_CHUNK, SSD_CHUNK), BF16))
    expand = (jnp.arange(LANES)[:, None] == jnp.arange(SSM_WIDTH)[None, :] // SSM_HEAD_DIM).astype(BF16)
    y = _ssd_scan(xc, dt_raw, _lane_pad(row(dt_bias[0])), _lane_pad(row(a_log[0])),
                  row(jnp.repeat(d_skip[0], SSM_HEAD_DIM)), tril, expand, seq)

    out = _out_mlp(att, y, z, x2d, row(ssm_norm[0]), w_out[0].astype(BF16), row(norm_mix_post[0]),
                   row(norm_mlp_pre[0]), w_mlp_up[0].astype(BF16), w_mlp_down[0].astype(BF16),
                   row(norm_mlp_post[0]))
    return out[None]
```

```python
import math

import jax
import jax.numpy as jnp
from jax import lax
from jax.experimental import pallas as pl
from jax.experimental.pallas import tpu as pltpu

D_MODEL = 1024
N_META = 16
EPS = 1e-6
ATT_HEADS = 8
Q_LORA = 384
KV_LORA = 256
QK_NOPE = 128
QK_ROPE = 64
V_HEAD = 128
QK_DIM = QK_NOPE + QK_ROPE
ROPE_THETA = 10000.0
ATT_WIDTH = ATT_HEADS * V_HEAD
SSM_HEADS = 16
SSM_HEAD_DIM = 64
SSM_WIDTH = SSM_HEADS * SSM_HEAD_DIM
SSM_GROUPS = 2
SSM_STATE = 128
GROUP_WIDTH = SSM_WIDTH // SSM_GROUPS
CONV_K = 4
CONV_DIM = SSM_WIDTH + 2 * SSM_GROUPS * SSM_STATE
D_FF = 4 * D_MODEL

LANES = 128
SUBLANES = 8
VMEM_LIMIT_BYTES = 56 * 1024 * 1024

QK_PAD = 2 * LANES
PAD = 256
ROW_BLOCK = 256
MLP_BLOCK = 512
ATT_Q = 512
ATT_K = 512
ATT_UNROLL = 8
SSD_CHUNK = 128
FF_CHUNK = 1024
ATT_SCALE_LOG2 = (QK_DIM ** -0.5) * math.log2(math.e)

BF16 = jnp.bfloat16
F32 = jnp.float32


def _const_spec(shape):
    nd = len(shape)
    return pl.BlockSpec(shape, lambda *_: (0,) * nd, pipeline_mode=pl.Buffered(1))


def _rms(x, gain):
    return x * lax.rsqrt(jnp.mean(jnp.square(x), axis=-1, keepdims=True) + EPS) * gain


def _proj_kernel(meta_ref, x_ref, rope_blk_ref, rope_loc_ref, g_pre_ref, wqa_ref, wkva_ref, wz_ref, wxbc_ref,
                 wsm_ref, gq_ref, gkv_ref, wq_ref, wkt_ref, wv_ref, convw_ref, convb_ref,
                 q_ref, k_ref, v_ref, z_ref, xbc_ref, dt_ref, stage_ref):
    i = pl.program_id(0)
    rows_b = x_ref.shape[0]

    @pl.when(i == 0)
    def _():
        stage_ref[...] = jnp.zeros_like(stage_ref)

    u = stage_ref[...]
    u1 = pltpu.roll(u, 1, axis=0)
    older = pltpu.roll(convw_ref[1:2, :] * u + convw_ref[0:1, :] * u1, 2, axis=0)
    conv = (convw_ref[3:4, :] * u + convw_ref[2:3, :] * u1 + older + convb_ref[...])[SUBLANES:, :]
    xbc_ref[...] = conv * jax.nn.sigmoid(conv)
    stage_ref[0:SUBLANES, :] = stage_ref[rows_b:rows_b + SUBLANES, :]

    xin = jnp.where(i == 0, meta_ref[...], x_ref[...])
    hn = _rms(xin, g_pre_ref[...]).astype(BF16)

    blk = rope_blk_ref[0]
    cs = blk[0:1, :] * rope_loc_ref[0] + blk[1:2, :] * rope_loc_ref[1]
    low = lax.broadcasted_iota(jnp.int32, cs.shape, 1) < QK_ROPE

    def rope(t):
        prod = t * cs
        return prod + pltpu.roll(prod, QK_ROPE, axis=1)

    z_ref[...] = jnp.dot(hn, wz_ref[...], preferred_element_type=F32)
    stage_ref[SUBLANES:, :] = jnp.dot(hn, wxbc_ref[...], preferred_element_type=F32)

    small = jnp.dot(hn, wsm_ref[...], preferred_element_type=F32)
    dt_ref[...] = small[:, LANES:]
    k_rope_t = jnp.where(low, rope(small[:, :LANES]), 0.0).T.astype(BF16)

    c_q = jnp.dot(hn, wqa_ref[...], preferred_element_type=F32)
    cqn = _rms(c_q, gq_ref[...]).astype(BF16)
    c_kv = jnp.dot(hn, wkva_ref[...], preferred_element_type=F32)
    ckvn = _rms(c_kv, gkv_ref[...])
    k_nope_t = jnp.dot(wkt_ref[...], ckvn.T.astype(BF16), preferred_element_type=F32)
    v_all = jnp.dot(ckvn.astype(BF16), wv_ref[...], preferred_element_type=F32)

    for h in range(ATT_HEADS):
        qh = jnp.dot(cqn, wq_ref[:, h * QK_PAD:(h + 1) * QK_PAD], preferred_element_type=F32)
        q_ref[h, :, :QK_NOPE] = (qh[:, :QK_NOPE] * ATT_SCALE_LOG2).astype(BF16)
        q_ref[h, :, QK_NOPE:] = (rope(qh[:, QK_NOPE:]) * ATT_SCALE_LOG2).astype(BF16)
        k_ref[h, :QK_NOPE, :] = k_nope_t[h * QK_NOPE:(h + 1) * QK_NOPE, :].astype(BF16)
        k_ref[h, QK_NOPE:, :] = k_rope_t
        v_ref[h] = v_all[:, h * V_HEAD:(h + 1) * V_HEAD].astype(BF16)


def _input_projection(x2d, meta_pad, rope_blk, rope_loc, g_pre, wqa, wkva, wz, wxbc, wsm, gq, gkv, wq, wkt, wv,
                      conv_w, conv_b):
    seq = x2d.shape[0]
    n_blocks = seq // ROW_BLOCK + 1
    rows_p = seq + PAD
    pblk = lambda i: jnp.minimum(i, n_blocks - 1)
    rblk = lambda i: jnp.maximum(pblk(i) - 1, 0)
    real = lambda i: (rblk(i), 0)
    real3 = lambda i: (0, rblk(i), 0)
    padded = lambda i: (pblk(i), 0)
    padded3 = lambda i: (0, pblk(i), 0)
    lagged = lambda i: (jnp.maximum(i - 1, 0), 0)
    weights = (g_pre, wqa, wkva, wz, wxbc, wsm, gq, gkv, wq, wkt, wv, conv_w, conv_b)
    return pl.pallas_call(
        _proj_kernel,
        grid=(n_blocks + 1,),
        in_specs=[_const_spec(meta_pad.shape),
                  pl.BlockSpec((ROW_BLOCK, D_MODEL), real),
                  pl.BlockSpec((1, SUBLANES, LANES), lambda i: (pblk(i), 0, 0)),
                  _const_spec(rope_loc.shape)] + [_const_spec(w.shape) for w in weights],
        out_specs=[pl.BlockSpec((ATT_HEADS, ROW_BLOCK, QK_PAD), real3),
                   pl.BlockSpec((ATT_HEADS, QK_PAD, ROW_BLOCK), lambda i: (0, 0, pblk(i))),
                   pl.BlockSpec((ATT_HEADS, ROW_BLOCK, V_HEAD), padded3),
                   pl.BlockSpec((ROW_BLOCK, SSM_WIDTH), real),
                   pl.BlockSpec((ROW_BLOCK, CONV_DIM), lagged),
                   pl.BlockSpec((ROW_BLOCK, LANES), padded)],
        out_shape=[jax.ShapeDtypeStruct((ATT_HEADS, seq, QK_PAD), BF16),
                   jax.ShapeDtypeStruct((ATT_HEADS, QK_PAD, rows_p), BF16),
                   jax.ShapeDtypeStruct((ATT_HEADS, rows_p, V_HEAD), BF16),
                   jax.ShapeDtypeStruct((seq, SSM_WIDTH), F32),
                   jax.ShapeDtypeStruct((rows_p, CONV_DIM), F32),
                   jax.ShapeDtypeStruct((rows_p, LANES), F32)],
        scratch_shapes=[pltpu.VMEM((SUBLANES + ROW_BLOCK, CONV_DIM), F32)],
        compiler_params=pltpu.CompilerParams(dimension_semantics=("arbitrary",),
                                             vmem_limit_bytes=VMEM_LIMIT_BYTES),
        name="input_projection",
    )(meta_pad, x2d, rope_blk, rope_loc, *weights)


def _attn_kernel(q_ref, k_ref, v_ref, o_ref, sm_ref, pm_ref, s_ref, p_ref, m_ref, l_ref, acc_ref):
    i = pl.program_id(1)
    n_tiles = ATT_K // LANES
    assert ATT_Q == ATT_K, "one diagonal chunk per query block"
    assert ATT_UNROLL % 2 == 0 and PAD >= LANES

    def chunk_start(j):
        return pl.multiple_of(PAD + j * ATT_K, math.gcd(PAD, ATT_K))

    def scores(buf, j):
        s_ref[buf] = jnp.dot(q_ref[0], k_ref[0, :, pl.ds(chunk_start(j), ATT_K)], preferred_element_type=F32)

    def score_tiles(buf, last):
        tiles = [s_ref[buf, :, t * LANES:(t + 1) * LANES] for t in range(n_tiles)]
        if last:
            rows = lax.broadcasted_iota(jnp.int32, (ATT_Q, LANES), 0)
            cols = lax.broadcasted_iota(jnp.int32, (ATT_Q, LANES), 1)
            tiles = [jnp.where(cols + t * LANES <= rows, tiles[t], -jnp.inf) for t in range(n_tiles)]
            tiles = [jnp.where(cols >= LANES - N_META, sm_ref[...], -jnp.inf)] + tiles
        return tiles

    def consume(buf, j, last=False):
        tiles = score_tiles(buf, last)
        mx = tiles[0]
        for t in range(1, len(tiles)):
            mx = jnp.maximum(mx, tiles[t])
        m_old = m_ref[...]
        m_new = jnp.maximum(m_old, jnp.max(mx, axis=1, keepdims=True))
        m_ref[...] = m_new
        alpha = jnp.exp2(m_old - m_new)
        tiles = score_tiles(buf, last)
        probs = [jnp.exp2(t - m_new) for t in tiles]
        lsum = probs[0]
        for p in probs[1:]:
            lsum = lsum + p
        l_ref[...] = alpha * l_ref[...] + lsum
        pv = None
        if last:
            pm_ref[...] = probs.pop(0).astype(BF16)
            pv = jnp.dot(pm_ref[...], v_ref[0, PAD - LANES:PAD, :], preferred_element_type=F32)
        for t, p in enumerate(probs):
            p_ref[:, t * LANES:(t + 1) * LANES] = p.astype(BF16)
        pv_chunk = jnp.dot(p_ref[...], v_ref[0, pl.ds(chunk_start(j), ATT_K), :], preferred_element_type=F32)
        acc_ref[...] = alpha * acc_ref[...] + (pv_chunk if pv is None else pv + pv_chunk)

    def run(first, count):
        for u in range(count):
            scores((u + 1) % 2, first + u + 1)
            consume(u % 2, first + u)

    m_ref[...] = jnp.full((ATT_Q, LANES), -jnp.inf, F32)
    l_ref[...] = jnp.zeros((ATT_Q, LANES), F32)
    acc_ref[...] = jnp.zeros((ATT_Q, V_HEAD), F32)
    sm_ref[...] = jnp.dot(q_ref[0], k_ref[0, :, PAD - LANES:PAD], preferred_element_type=F32)
    scores(0, 0)

    def trip(t, carry):
        run(ATT_UNROLL * t, ATT_UNROLL)
        return carry

    lax.fori_loop(0, i // ATT_UNROLL, trip, 0)

    done = i - i % ATT_UNROLL
    size = ATT_UNROLL // 2
    while size >= 2:
        @pl.when((i - done) >= size)
        def _():
            run(done, size)
        done = done + jnp.where((i - done) >= size, size, 0)
        size //= 2

    @pl.when(i - done >= 1)
    def _():
        run(done, 1)

    for parity in range(2):
        @pl.when(i % 2 == parity)
        def _():
            consume(parity, i, last=True)
            o_ref[...] = (acc_ref[...] / jnp.sum(l_ref[...], axis=1, keepdims=True)).astype(o_ref.dtype)


def _attention(q, k, v):
    heads, seq, _ = q.shape
    rows_p = v.shape[1]
    n_q = seq // ATT_Q
    return pl.pallas_call(
        _attn_kernel,
        grid=(heads, n_q),
        in_specs=[pl.BlockSpec((1, ATT_Q, QK_PAD), lambda h, i: (h, i, 0)),
                  pl.BlockSpec((1, QK_PAD, rows_p), lambda h, i: (h, 0, 0)),
                  pl.BlockSpec((1, rows_p, V_HEAD), lambda h, i: (h, 0, 0))],
        out_specs=pl.BlockSpec((ATT_Q, V_HEAD), lambda h, i: (i, h)),
        out_shape=jax.ShapeDtypeStruct((seq, ATT_WIDTH), BF16),
        scratch_shapes=[pltpu.VMEM((ATT_Q, LANES), F32),
                        pltpu.VMEM((ATT_Q, LANES), BF16),
                        pltpu.VMEM((2, ATT_Q, ATT_K), F32),
                        pltpu.VMEM((ATT_Q, ATT_K), BF16),
                        pltpu.VMEM((ATT_Q, LANES), F32),
                        pltpu.VMEM((ATT_Q, LANES), F32),
                        pltpu.VMEM((ATT_Q, V_HEAD), F32)],
        compiler_params=pltpu.CompilerParams(dimension_semantics=("arbitrary", "arbitrary"),
                                             vmem_limit_bytes=VMEM_LIMIT_BYTES),
        name="causal_attention",
    )(q, k, v)


def _split_dot(v, mat, terms):
    out = None
    rem = v
    for t in range(terms):
        piece = rem.astype(BF16)
        part = jnp.dot(piece, mat, preferred_element_type=F32)
        out = part if out is None else out + part
        if t + 1 < terms:
            rem = rem - piece.astype(F32)
    return out


def _ssd_kernel(xc_ref, dt_ref, dtb_ref, alog_ref, dskip_ref, tril_ref, expand_ref, y_ref, state_ref):
    i = pl.program_id(0)
    rows_b = xc_ref.shape[0]

    @pl.when(i == 0)
    def _():
        state_ref[...] = jnp.zeros_like(state_ref)

    xc = xc_ref

    lane = lax.broadcasted_iota(jnp.int32, (SSD_CHUNK, LANES), 1)
    row = lax.broadcasted_iota(jnp.int32, (SSD_CHUNK, LANES), 0)
    causal = lax.broadcasted_iota(jnp.int32, (SSD_CHUNK, SSD_CHUNK), 0) >= lax.broadcasted_iota(
        jnp.int32, (SSD_CHUNK, SSD_CHUNK), 1)
    low_half = lane < SSM_HEAD_DIM
    a_row = -jnp.exp(alog_ref[...])
    tril = tril_ref[...]
    expand = expand_ref[...]
    heads_per_group = SSM_HEADS // SSM_GROUPS

    for c in range(rows_b // SSD_CHUNK):
        r0 = c * SSD_CHUNK
        xs = xc[r0:r0 + SSD_CHUNK, :SSM_WIDTH]
        valid = (lane < SSM_HEADS) & (row + (i * rows_b + r0) >= PAD - N_META)
        dt = jnp.where(valid, jax.nn.softplus(dt_ref[r0:r0 + SSD_CHUNK, :] + dtb_ref[...]), 0.0)
        da = dt * a_row
        acum = _cumsum_rows(tril, da)
        acum_t = acum.T
        dt_t = dt.T
        a_last = acum[SSD_CHUNK - 1:SSD_CHUNK, :]
        e_acum = _split_dot(jnp.exp(acum), expand, 2)
        w_end = _split_dot(jnp.exp(a_last - acum) * dt, expand, 2)
        c_decay = _split_dot(jnp.broadcast_to(jnp.exp(a_last), (SUBLANES, LANES)), expand, 2)[0:1, :]

        y_diag = []
        y_off = []
        for g in range(SSM_GROUPS):
            b_g = xc[r0:r0 + SSD_CHUNK, SSM_WIDTH + g * SSM_STATE:SSM_WIDTH + (g + 1) * SSM_STATE]
            c_g = xc[r0:r0 + SSD_CHUNK, SSM_WIDTH + (SSM_GROUPS + g) * SSM_STATE:
                     SSM_WIDTH + (SSM_GROUPS + g + 1) * SSM_STATE].astype(BF16)
            b_t = b_g.T.astype(BF16)
            cb = jnp.dot(c_g, b_t, preferred_element_type=F32)
            gcols = slice(g * GROUP_WIDTH, (g + 1) * GROUP_WIDTH)
            for jp in range(heads_per_group // 2):
                ws = []
                for r in (g * heads_per_group + 2 * jp, g * heads_per_group + 2 * jp + 1):
                    seg = acum[:, r:r + 1] - acum_t[r:r + 1, :]
                    decay = jnp.where(causal, jnp.exp(seg), 0.0)
                    ws.append((cb * decay * dt_t[r:r + 1, :]).astype(BF16))
                col = g * GROUP_WIDTH + jp * LANES
                x_pair = xs[:, col:col + LANES]
                x_cat = jnp.concatenate([jnp.where(low_half, x_pair, 0.0), jnp.where(low_half, 0.0, x_pair)],
                                        axis=0).astype(BF16)
                y_diag.append(jnp.dot(jnp.concatenate(ws, axis=1), x_cat, preferred_element_type=F32))
            h_prev = state_ref[g]
            y_off.append(jnp.dot(c_g, h_prev.astype(BF16), preferred_element_type=F32))
            x_end = (xs[:, gcols] * w_end[:, gcols]).astype(BF16)
            state_ref[g] = c_decay[:, gcols] * h_prev + jnp.dot(b_t, x_end, preferred_element_type=F32)
        y = (jnp.concatenate(y_diag, axis=1) + jnp.concatenate(y_off, axis=1) * e_acum
             + xs * dskip_ref[...])
        y_ref[r0:r0 + SSD_CHUNK, :] = y


def _cumsum_rows(tril, v):
    out = None
    rem = v
    for t in range(3):
        piece = rem.astype(BF16)
        part = jnp.dot(tril, piece, preferred_element_type=F32)
        out = part if out is None else out + part
        if t < 2:
            rem = rem - piece.astype(F32)
    return out


def _ssd_scan(xc, dt_raw, dt_bias, a_log, d_skip_x, tril, expand, seq):
    n_blocks = seq // ROW_BLOCK + 1
    consts = (dt_bias, a_log, d_skip_x, tril, expand)
    return pl.pallas_call(
        _ssd_kernel,
        grid=(n_blocks,),
        in_specs=[pl.BlockSpec((ROW_BLOCK, CONV_DIM), lambda i: (i, 0)),
                  pl.BlockSpec((ROW_BLOCK, LANES), lambda i: (i, 0))] + [_const_spec(c.shape) for c in consts],
        out_specs=pl.BlockSpec((ROW_BLOCK, SSM_WIDTH), lambda i: (jnp.maximum(i - 1, 0), 0)),
        out_shape=jax.ShapeDtypeStruct((seq, SSM_WIDTH), F32),
        scratch_shapes=[pltpu.VMEM((SSM_GROUPS, SSM_STATE, GROUP_WIDTH), F32)],
        compiler_params=pltpu.CompilerParams(dimension_semantics=("arbitrary",),
                                             vmem_limit_bytes=VMEM_LIMIT_BYTES),
        name="ssd_scan",
    )(xc, dt_raw, *consts)


def _out_mlp_kernel(att_ref, y_ref, z_ref, x_ref, g_ssm_ref, wout_ref, g_post_ref, g_mpre_ref,
                    wup_ref, wdown_ref, g_mpost_ref, o_ref):
    z = z_ref[...]
    gated = y_ref[...] * (z * jax.nn.sigmoid(z))
    parts = []
    for g in range(SSM_GROUPS):
        gg = gated[:, g * GROUP_WIDTH:(g + 1) * GROUP_WIDTH]
        parts.append(gg * lax.rsqrt(jnp.mean(jnp.square(gg), axis=-1, keepdims=True) + EPS))
    ssm = (jnp.concatenate(parts, axis=1) * g_ssm_ref[...]).astype(BF16)
    mix = (jnp.dot(att_ref[...], wout_ref[:ATT_WIDTH, :], preferred_element_type=F32)
           + jnp.dot(ssm, wout_ref[ATT_WIDTH:, :], preferred_element_type=F32))
    h1 = x_ref[...] + _rms(mix, g_post_ref[...])
    hn = _rms(h1, g_mpre_ref[...]).astype(BF16)
    f = None
    for c in range(D_FF // FF_CHUNK):
        u = jnp.dot(hn, wup_ref[:, c * FF_CHUNK:(c + 1) * FF_CHUNK], preferred_element_type=F32)
        u = jnp.square(jnp.maximum(u, 0.0)).astype(BF16)
        part = jnp.dot(u, wdown_ref[c * FF_CHUNK:(c + 1) * FF_CHUNK, :], preferred_element_type=F32)
        f = part if f is None else f + part
    o_ref[...] = h1 + _rms(f, g_mpost_ref[...])


def _out_mlp(att, y, z, x2d, g_ssm, wout, g_post, g_mpre, wup, wdown, g_mpost):
    seq = x2d.shape[0]
    weights = (g_ssm, wout, g_post, g_mpre, wup, wdown, g_mpost)
    row = lambda i: (i, 0)
    return pl.pallas_call(
        _out_mlp_kernel,
        grid=(seq // MLP_BLOCK,),
        in_specs=[pl.BlockSpec((MLP_BLOCK, ATT_WIDTH), row),
                  pl.BlockSpec((MLP_BLOCK, SSM_WIDTH), row),
                  pl.BlockSpec((MLP_BLOCK, SSM_WIDTH), row),
                  pl.BlockSpec((MLP_BLOCK, D_MODEL), row)] + [_const_spec(w.shape) for w in weights],
        out_specs=pl.BlockSpec((MLP_BLOCK, D_MODEL), row),
        out_shape=jax.ShapeDtypeStruct((seq, D_MODEL), F32),
        compiler_params=pltpu.CompilerParams(dimension_semantics=("arbitrary",),
                                             vmem_limit_bytes=VMEM_LIMIT_BYTES),
        name="out_proj_mlp",
    )(att, y, z, x2d, *weights)


def _rot_cols(w):
    half = QK_ROPE // 2
    return jnp.concatenate([-w[..., half:], w[..., :half]], axis=-1)


def _lane_pad(v, width=LANES):
    return jnp.pad(v, [(0, 0)] * (v.ndim - 1) + [(0, width - v.shape[-1])])


def kernel(x, meta_tokens, norm_mix_pre, w_in, q_a_norm, w_q_up, kv_a_norm, w_kv_up, conv_w, conv_b, dt_bias,
           a_log, d_skip, ssm_norm, w_out, norm_mix_post, norm_mlp_pre, w_mlp_up, w_mlp_down, norm_mlp_post):
    bsz, seq, _ = x.shape
    assert bsz == 1 and norm_mix_pre.shape[0] == 1, "single sequence, single layer"
    assert seq % ATT_Q == 0 and seq % ROW_BLOCK == 0 and seq % MLP_BLOCK == 0
    assert PAD == ROW_BLOCK, "the front pad is exactly the first block of the projection / SSD grids"
    x2d = x[0]
    row = lambda v: v.reshape(1, -1)

    o_ckv, o_kr, o_z, o_xbc, o_dt = Q_LORA, Q_LORA + KV_LORA, Q_LORA + KV_LORA + QK_ROPE, \
        Q_LORA + KV_LORA + QK_ROPE + SSM_WIDTH, Q_LORA + KV_LORA + QK_ROPE + SSM_WIDTH + CONV_DIM
    w_in0 = w_in[0]
    w_kr = w_in0[:, o_kr:o_z]
    wsm = jnp.concatenate([w_kr, _rot_cols(w_kr), _lane_pad(w_in0[:, o_dt:])], axis=1)
    wq3 = w_q_up[0].reshape(Q_LORA, ATT_HEADS, QK_DIM)
    wq = jnp.concatenate([wq3, _rot_cols(wq3[..., QK_NOPE:])], axis=-1).reshape(Q_LORA, ATT_HEADS * QK_PAD)
    wkv3 = w_kv_up[0].reshape(KV_LORA, ATT_HEADS, QK_NOPE + V_HEAD)
    wkt = wkv3[..., :QK_NOPE].reshape(KV_LORA, ATT_HEADS * QK_NOPE).T
    wv = wkv3[..., QK_NOPE:].reshape(KV_LORA, ATT_HEADS * V_HEAD)

    inv_freq = ROPE_THETA ** (-jnp.arange(0, QK_ROPE, 2, dtype=F32) / QK_ROPE)
    freq = jnp.tile(inv_freq, 4)
    low = jnp.arange(LANES) < QK_ROPE
    base = (jnp.arange(seq // ROW_BLOCK + 1) * ROW_BLOCK - (PAD - N_META)).astype(F32)
    ang_b = base[:, None] * freq[None, :]
    rope_blk = jnp.stack([jnp.cos(ang_b), jnp.where(low, -jnp.sin(ang_b), jnp.sin(ang_b))], axis=1)
    rope_blk = jnp.pad(rope_blk, ((0, 0), (0, SUBLANES - 2), (0, 0)))
    ang_l = jnp.arange(ROW_BLOCK, dtype=F32)[:, None] * freq[None, :]
    rope_loc = jnp.stack([jnp.where(low, jnp.cos(ang_l), jnp.sin(ang_l)),
                          jnp.where(low, jnp.sin(ang_l), jnp.cos(ang_l))])

    meta_pad = jnp.concatenate([jnp.zeros((PAD - N_META, D_MODEL), F32), meta_tokens.astype(F32)], axis=0)

    q, k, v, z, xc, dt_raw = _input_projection(
        x2d, meta_pad, rope_blk, rope_loc, row(norm_mix_pre[0]),
        w_in0[:, :o_ckv].astype(BF16), w_in0[:, o_ckv:o_kr].astype(BF16), w_in0[:, o_z:o_xbc].astype(BF16),
        w_in0[:, o_xbc:o_dt].astype(BF16), wsm.astype(BF16),
        row(q_a_norm[0]), row(kv_a_norm[0]), wq.astype(BF16), wkt.astype(BF16), wv.astype(BF16),
        conv_w[0], row(conv_b[0]))

    att = _attention(q, k, v)

    tril = jnp.tril(jnp.ones((SSD_CHUNK, SSD_CHUNK), BF16))
    expand = (jnp.arange(LANES)[:, None] == jnp.arange(SSM_WIDTH)[None, :] // SSM_HEAD_DIM).astype(BF16)
    y = _ssd_scan(xc, dt_raw, _lane_pad(row(dt_bias[0])), _lane_pad(row(a_log[0])),
                  row(jnp.repeat(d_skip[0], SSM_HEAD_DIM)), tril, expand, seq)

    out = _out_mlp(att, y, z, x2d, row(ssm_norm[0]), w_out[0].astype(BF16), row(norm_mix_post[0]),
                   row(norm_mlp_pre[0]), w_mlp_up[0].astype(BF16), w_mlp_down[0].astype(BF16),
                   row(norm_mlp_post[0]))
    return out[None]
```

```python
import math

import jax
import jax.numpy as jnp
from jax import lax
from jax.experimental import pallas as pl
from jax.experimental.pallas import tpu as pltpu

D_MODEL = 1024
N_META = 16
EPS = 1e-6
ATT_HEADS = 8
Q_LORA = 384
KV_LORA = 256
QK_NOPE = 128
QK_ROPE = 64
V_HEAD = 128
QK_DIM = QK_NOPE + QK_ROPE
ROPE_THETA = 10000.0
ATT_WIDTH = ATT_HEADS * V_HEAD
SSM_HEADS = 16
SSM_HEAD_DIM = 64
SSM_WIDTH = SSM_HEADS * SSM_HEAD_DIM
SSM_GROUPS = 2
SSM_STATE = 128
GROUP_WIDTH = SSM_WIDTH // SSM_GROUPS
CONV_K = 4
CONV_DIM = SSM_WIDTH + 2 * SSM_GROUPS * SSM_STATE
D_FF = 4 * D_MODEL

LANES = 128
SUBLANES = 8
VMEM_LIMIT_BYTES = 56 * 1024 * 1024

QK_PAD = 2 * LANES
PAD = 512
ROW_BLOCK = 512
MLP_BLOCK = 512
ATT_Q = 512
ATT_K = 512
ATT_UNROLL = 8
SSD_CHUNK = 128
FF_CHUNK = 1024
ATT_SCALE_LOG2 = (QK_DIM ** -0.5) * math.log2(math.e)

BF16 = jnp.bfloat16
F32 = jnp.float32


def _const_spec(shape):
    nd = len(shape)
    return pl.BlockSpec(shape, lambda *_: (0,) * nd, pipeline_mode=pl.Buffered(1))


def _rms(x, gain):
    return x * lax.rsqrt(jnp.mean(jnp.square(x), axis=-1, keepdims=True) + EPS) * gain


def _proj_kernel(meta_ref, x_ref, rope_blk_ref, rope_loc_ref, g_pre_ref, wqa_ref, wkva_ref, wz_ref, wxbc_ref,
                 wsm_ref, gq_ref, gkv_ref, wq_ref, wkt_ref, wv_ref, convw_ref, convb_ref,
                 q_ref, k_ref, v_ref, z_ref, xbc_ref, dt_ref, stage_ref):
    i = pl.program_id(0)
    rows_b = x_ref.shape[0]

    @pl.when(i == 0)
    def _():
        stage_ref[...] = jnp.zeros_like(stage_ref)

    u = stage_ref[...]
    u1 = pltpu.roll(u, 1, axis=0)
    older = pltpu.roll(convw_ref[1:2, :] * u + convw_ref[0:1, :] * u1, 2, axis=0)
    conv = (convw_ref[3:4, :] * u + convw_ref[2:3, :] * u1 + older + convb_ref[...])[SUBLANES:, :]
    xbc_ref[...] = conv * jax.nn.sigmoid(conv)
    stage_ref[0:SUBLANES, :] = stage_ref[rows_b:rows_b + SUBLANES, :]

    xin = jnp.where(i == 0, meta_ref[...], x_ref[...])
    hn = _rms(xin, g_pre_ref[...]).astype(BF16)

    blk = rope_blk_ref[0]
    cs = blk[0:1, :] * rope_loc_ref[0] + blk[1:2, :] * rope_loc_ref[1]
    low = lax.broadcasted_iota(jnp.int32, cs.shape, 1) < QK_ROPE

    def rope(t):
        prod = t * cs
        return prod + pltpu.roll(prod, QK_ROPE, axis=1)

    z_ref[...] = jnp.dot(hn, wz_ref[...], preferred_element_type=F32)
    stage_ref[SUBLANES:, :] = jnp.dot(hn, wxbc_ref[...], preferred_element_type=F32)

    small = jnp.dot(hn, wsm_ref[...], preferred_element_type=F32)
    dt_ref[...] = small[:, LANES:]
    k_rope_t = jnp.where(low, rope(small[:, :LANES]), 0.0).T.astype(BF16)

    c_q = jnp.dot(hn, wqa_ref[...], preferred_element_type=F32)
    cqn = _rms(c_q, gq_ref[...]).astype(BF16)
    c_kv = jnp.dot(hn, wkva_ref[...], preferred_element_type=F32)
    ckvn = _rms(c_kv, gkv_ref[...])
    k_nope_t = jnp.dot(wkt_ref[...], ckvn.T.astype(BF16), preferred_element_type=F32)
    v_all = jnp.dot(ckvn.astype(BF16), wv_ref[...], preferred_element_type=F32)

    for h in range(ATT_HEADS):
        qh = jnp.dot(cqn, wq_ref[:, h * QK_PAD:(h + 1) * QK_PAD], preferred_element_type=F32)
        q_ref[h, :, :QK_NOPE] = (qh[:, :QK_NOPE] * ATT_SCALE_LOG2).astype(BF16)
        q_ref[h, :, QK_NOPE:] = (rope(qh[:, QK_NOPE:]) * ATT_SCALE_LOG2).astype(BF16)
        k_ref[h, :QK_NOPE, :] = k_nope_t[h * QK_NOPE:(h + 1) * QK_NOPE, :].astype(BF16)
        k_ref[h, QK_NOPE:, :] = k_rope_t
        v_ref[h] = v_all[:, h * V_HEAD:(h + 1) * V_HEAD].astype(BF16)


def _input_projection(x2d, meta_pad, rope_blk, rope_loc, g_pre, wqa, wkva, wz, wxbc, wsm, gq, gkv, wq, wkt, wv,
                      conv_w, conv_b):
    seq = x2d.shape[0]
    n_blocks = seq // ROW_BLOCK + 1
    rows_p = seq + PAD
    pblk = lambda i: jnp.minimum(i, n_blocks - 1)
    rblk = lambda i: jnp.maximum(pblk(i) - 1, 0)
    real = lambda i: (rblk(i), 0)
    real3 = lambda i: (0, rblk(i), 0)
    padded = lambda i: (pblk(i), 0)
    padded3 = lambda i: (0, pblk(i), 0)
    lagged = lambda i: (jnp.maximum(i - 1, 0), 0)
    weights = (g_pre, wqa, wkva, wz, wxbc, wsm, gq, gkv, wq, wkt, wv, conv_w, conv_b)
    return pl.pallas_call(
        _proj_kernel,
        grid=(n_blocks + 1,),
        in_specs=[_const_spec(meta_pad.shape),
                  pl.BlockSpec((ROW_BLOCK, D_MODEL), real),
                  pl.BlockSpec((1, SUBLANES, LANES), lambda i: (pblk(i), 0, 0)),
                  _const_spec(rope_loc.shape)] + [_const_spec(w.shape) for w in weights],
        out_specs=[pl.BlockSpec((ATT_HEADS, ROW_BLOCK, QK_PAD), real3),
                   pl.BlockSpec((ATT_HEADS, QK_PAD, ROW_BLOCK), lambda i: (0, 0, pblk(i))),
                   pl.BlockSpec((ATT_HEADS, ROW_BLOCK, V_HEAD), padded3),
                   pl.BlockSpec((ROW_BLOCK, SSM_WIDTH), real),
                   pl.BlockSpec((ROW_BLOCK, CONV_DIM), lagged),
                   pl.BlockSpec((ROW_BLOCK, LANES), padded)],
        out_shape=[jax.ShapeDtypeStruct((ATT_HEADS, seq, QK_PAD), BF16),
                   jax.ShapeDtypeStruct((ATT_HEADS, QK_PAD, rows_p), BF16),
                   jax.ShapeDtypeStruct((ATT_HEADS, rows_p, V_HEAD), BF16),
                   jax.ShapeDtypeStruct((seq, SSM_WIDTH), F32),
                   jax.ShapeDtypeStruct((rows_p, CONV_DIM), F32),
                   jax.ShapeDtypeStruct((rows_p, LANES), F32)],
        scratch_shapes=[pltpu.VMEM((SUBLANES + ROW_BLOCK, CONV_DIM), F32)],
        compiler_params=pltpu.CompilerParams(dimension_semantics=("arbitrary",),
                                             vmem_limit_bytes=VMEM_LIMIT_BYTES),
        name="input_projection",
    )(meta_pad, x2d, rope_blk, rope_loc, *weights)


def _attn_kernel(q_ref, k_ref, v_ref, o_ref, sm_ref, pm_ref, s_ref, p_ref, m_ref, l_ref, acc_ref):
    i = pl.program_id(1)
    n_tiles = ATT_K // LANES
    assert ATT_Q == ATT_K, "one diagonal chunk per query block"
    assert ATT_UNROLL % 2 == 0 and PAD >= LANES

    def chunk_start(j):
        return pl.multiple_of(PAD + j * ATT_K, math.gcd(PAD, ATT_K))

    def scores(buf, j):
        s_ref[buf] = jnp.dot(q_ref[0], k_ref[0, :, pl.ds(chunk_start(j), ATT_K)], preferred_element_type=F32)

    def score_tiles(buf, last):
        tiles = [s_ref[buf, :, t * LANES:(t + 1) * LANES] for t in range(n_tiles)]
        if last:
            rows = lax.broadcasted_iota(jnp.int32, (ATT_Q, LANES), 0)
            cols = lax.broadcasted_iota(jnp.int32, (ATT_Q, LANES), 1)
            tiles = [jnp.where(cols + t * LANES <= rows, tiles[t], -jnp.inf) for t in range(n_tiles)]
            tiles = [jnp.where(cols >= LANES - N_META, sm_ref[...], -jnp.inf)] + tiles
        return tiles

    def consume(buf, j, last=False):
        tiles = score_tiles(buf, last)
        mx = tiles[0]
        for t in range(1, len(tiles)):
            mx = jnp.maximum(mx, tiles[t])
        m_old = m_ref[...]
        m_new = jnp.maximum(m_old, jnp.max(mx, axis=1, keepdims=True))
        m_ref[...] = m_new
        alpha = jnp.exp2(m_old - m_new)
        tiles = score_tiles(buf, last)
        probs = [jnp.exp2(t - m_new) for t in tiles]
        lsum = probs[0]
        for p in probs[1:]:
            lsum = lsum + p
        l_ref[...] = alpha * l_ref[...] + lsum
        pv = None
        if last:
            pm_ref[...] = probs.pop(0).astype(BF16)
            pv = jnp.dot(pm_ref[...], v_ref[0, PAD - LANES:PAD, :], preferred_element_type=F32)
        for t, p in enumerate(probs):
            p_ref[:, t * LANES:(t + 1) * LANES] = p.astype(BF16)
        pv_chunk = jnp.dot(p_ref[...], v_ref[0, pl.ds(chunk_start(j), ATT_K), :], preferred_element_type=F32)
        acc_ref[...] = alpha * acc_ref[...] + (pv_chunk if pv is None else pv + pv_chunk)

    def run(first, count):
        for u in range(count):
            scores((u + 1) % 2, first + u + 1)
            consume(u % 2, first + u)

    m_ref[...] = jnp.full((ATT_Q, LANES), -jnp.inf, F32)
    l_ref[...] = jnp.zeros((ATT_Q, LANES), F32)
    acc_ref[...] = jnp.zeros((ATT_Q, V_HEAD), F32)
    sm_ref[...] = jnp.dot(q_ref[0], k_ref[0, :, PAD - LANES:PAD], preferred_element_type=F32)
    scores(0, 0)

    def trip(t, carry):
        run(ATT_UNROLL * t, ATT_UNROLL)
        return carry

    lax.fori_loop(0, i // ATT_UNROLL, trip, 0)

    done = i - i % ATT_UNROLL
    size = ATT_UNROLL // 2
    while size >= 2:
        @pl.when((i - done) >= size)
        def _():
            run(done, size)
        done = done + jnp.where((i - done) >= size, size, 0)
        size //= 2

    @pl.when(i - done >= 1)
    def _():
        run(done, 1)

    for parity in range(2):
        @pl.when(i % 2 == parity)
        def _():
            consume(parity, i, last=True)
            o_ref[...] = (acc_ref[...] / jnp.sum(l_ref[...], axis=1, keepdims=True)).astype(o_ref.dtype)


def _attention(q, k, v):
    heads, seq, _ = q.shape
    rows_p = v.shape[1]
    n_q = seq // ATT_Q
    return pl.pallas_call(
        _attn_kernel,
        grid=(heads, n_q),
        in_specs=[pl.BlockSpec((1, ATT_Q, QK_PAD), lambda h, i: (h, i, 0)),
                  pl.BlockSpec((1, QK_PAD, rows_p), lambda h, i: (h, 0, 0)),
                  pl.BlockSpec((1, rows_p, V_HEAD), lambda h, i: (h, 0, 0))],
        out_specs=pl.BlockSpec((ATT_Q, V_HEAD), lambda h, i: (i, h)),
        out_shape=jax.ShapeDtypeStruct((seq, ATT_WIDTH), BF16),
        scratch_shapes=[pltpu.VMEM((ATT_Q, LANES), F32),
                        pltpu.VMEM((ATT_Q, LANES), BF16),
                        pltpu.VMEM((2, ATT_Q, ATT_K), F32),
                        pltpu.VMEM((ATT_Q, ATT_K), BF16),
                        pltpu.VMEM((ATT_Q, LANES), F32),
                        pltpu.VMEM((ATT_Q, LANES), F32),
                        pltpu.VMEM((ATT_Q, V_HEAD), F32)],
        compiler_params=pltpu.CompilerParams(dimension_semantics=("arbitrary", "arbitrary"),
                                             vmem_limit_bytes=VMEM_LIMIT_BYTES),
        name="causal_attention",
    )(q, k, v)


def _split_dot(v, mat, terms):
    out = None
    rem = v
    for t in range(terms):
        piece = rem.astype(BF16)
        part = jnp.dot(piece, mat, preferred_element_type=F32)
        out = part if out is None else out + part
        if t + 1 < terms:
            rem = rem - piece.astype(F32)
    return out


def _ssd_kernel(xc_ref, dt_ref, dtb_ref, alog_ref, dskip_ref, tril_ref, expand_ref, y_ref, state_ref):
    i = pl.program_id(0)
    rows_b = xc_ref.shape[0]

    @pl.when(i == 0)
    def _():
        state_ref[...] = jnp.zeros_like(state_ref)

    xc = xc_ref

    lane = lax.broadcasted_iota(jnp.int32, (SSD_CHUNK, LANES), 1)
    row = lax.broadcasted_iota(jnp.int32, (SSD_CHUNK, LANES), 0)
    causal = lax.broadcasted_iota(jnp.int32, (SSD_CHUNK, SSD_CHUNK), 0) >= lax.broadcasted_iota(
        jnp.int32, (SSD_CHUNK, SSD_CHUNK), 1)
    low_half = lane < SSM_HEAD_DIM
    a_row = -jnp.exp(alog_ref[...])
    tril = tril_ref[...]
    expand = expand_ref[...]
    heads_per_group = SSM_HEADS // SSM_GROUPS

    for c in range(rows_b // SSD_CHUNK):
        r0 = c * SSD_CHUNK
        xs = xc[r0:r0 + SSD_CHUNK, :SSM_WIDTH]
        valid = (lane < SSM_HEADS) & (row + (i * rows_b + r0) >= PAD - N_META)
        dt = jnp.where(valid, jax.nn.softplus(dt_ref[r0:r0 + SSD_CHUNK, :] + dtb_ref[...]), 0.0)
        da = dt * a_row
        acum = _cumsum_rows(tril, da)
        acum_t = acum.T
        dt_t = dt.T
        a_last = acum[SSD_CHUNK - 1:SSD_CHUNK, :]
        e_acum = _split_dot(jnp.exp(acum), expand, 2)
        w_end = _split_dot(jnp.exp(a_last - acum) * dt, expand, 2)
        c_decay = _split_dot(jnp.broadcast_to(jnp.exp(a_last), (SUBLANES, LANES)), expand, 2)[0:1, :]

        y_diag = []
        y_off = []
        for g in range(SSM_GROUPS):
            b_g = xc[r0:r0 + SSD_CHUNK, SSM_WIDTH + g * SSM_STATE:SSM_WIDTH + (g + 1) * SSM_STATE]
            c_g = xc[r0:r0 + SSD_CHUNK, SSM_WIDTH + (SSM_GROUPS + g) * SSM_STATE:
                     SSM_WIDTH + (SSM_GROUPS + g + 1) * SSM_STATE].astype(BF16)
            b_t = b_g.T.astype(BF16)
            cb = jnp.dot(c_g, b_t, preferred_element_type=F32)
            gcols = slice(g * GROUP_WIDTH, (g + 1) * GROUP_WIDTH)
            for jp in range(heads_per_group // 2):
                ws = []
                for r in (g * heads_per_group + 2 * jp, g * heads_per_group + 2 * jp + 1):
                    seg = acum[:, r:r + 1] - acum_t[r:r + 1, :]
                    decay = jnp.where(causal, jnp.exp(seg), 0.0)
                    ws.append((cb * decay * dt_t[r:r + 1, :]).astype(BF16))
                col = g * GROUP_WIDTH + jp * LANES
                x_pair = xs[:, col:col + LANES]
                x_cat = jnp.concatenate([jnp.where(low_half, x_pair, 0.0), jnp.where(low_half, 0.0, x_pair)],
                                        axis=0).astype(BF16)
                y_diag.append(jnp.dot(jnp.concatenate(ws, axis=1), x_cat, preferred_element_type=F32))
            h_prev = state_ref[g]
            y_off.append(jnp.dot(c_g, h_prev.astype(BF16), preferred_element_type=F32))
            x_end = (xs[:, gcols] * w_end[:, gcols]).astype(BF16)
            state_ref[g] = c_decay[:, gcols] * h_prev + jnp.dot(b_t, x_end, preferred_element_type=F32)
        y = (jnp.concatenate(y_diag, axis=1) + jnp.concatenate(y_off, axis=1) * e_acum
             + xs * dskip_ref[...])
        y_ref[r0:r0 + SSD_CHUNK, :] = y


def _cumsum_rows(tril, v):
    out = None
    rem = v
    for t in range(3):
        piece = rem.astype(BF16)
        part = jnp.dot(tril, piece, preferred_element_type=F32)
        out = part if out is None else out + part
        if t < 2:
            rem = rem - piece.astype(F32)
    return out


def _ssd_scan(xc, dt_raw, dt_bias, a_log, d_skip_x, tril, expand, seq):
    n_blocks = seq // ROW_BLOCK + 1
    consts = (dt_bias, a_log, d_skip_x, tril, expand)
    return pl.pallas_call(
        _ssd_kernel,
        grid=(n_blocks,),
        in_specs=[pl.BlockSpec((ROW_BLOCK, CONV_DIM), lambda i: (i, 0)),
                  pl.BlockSpec((ROW_BLOCK, LANES), lambda i: (i, 0))] + [_const_spec(c.shape) for c in consts],
        out_specs=pl.BlockSpec((ROW_BLOCK, SSM_WIDTH), lambda i: (jnp.maximum(i - 1, 0), 0)),
        out_shape=jax.ShapeDtypeStruct((seq, SSM_WIDTH), F32),
        scratch_shapes=[pltpu.VMEM((SSM_GROUPS, SSM_STATE, GROUP_WIDTH), F32)],
        compiler_params=pltpu.CompilerParams(dimension_semantics=("arbitrary",),
                                             vmem_limit_bytes=VMEM_LIMIT_BYTES),
        name="ssd_scan",
    )(xc, dt_raw, *consts)


def _out_mlp_kernel(att_ref, y_ref, z_ref, x_ref, g_ssm_ref, wout_ref, g_post_ref, g_mpre_ref,
                    wup_ref, wdown_ref, g_mpost_ref, o_ref):
    z = z_ref[...]
    gated = y_ref[...] * (z * jax.nn.sigmoid(z))
    parts = []
    for g in range(SSM_GROUPS):
        gg = gated[:, g * GROUP_WIDTH:(g + 1) * GROUP_WIDTH]
        parts.append(gg * lax.rsqrt(jnp.mean(jnp.square(gg), axis=-1, keepdims=True) + EPS))
    ssm = (jnp.concatenate(parts, axis=1) * g_ssm_ref[...]).astype(BF16)
    mix = (jnp.dot(att_ref[...], wout_ref[:ATT_WIDTH, :], preferred_element_type=F32)
           + jnp.dot(ssm, wout_ref[ATT_WIDTH:, :], preferred_element_type=F32))
    h1 = x_ref[...] + _rms(mix, g_post_ref[...])
    hn = _rms(h1, g_mpre_ref[...]).astype(BF16)
    f = None
    for c in range(D_FF // FF_CHUNK):
        u = jnp.dot(hn, wup_ref[:, c * FF_CHUNK:(c + 1) * FF_CHUNK], preferred_element_type=F32)
        u = jnp.square(jnp.maximum(u, 0.0)).astype(BF16)
        part = jnp.dot(u, wdown_ref[c * FF_CHUNK:(c + 1) * FF_CHUNK, :], preferred_element_type=F32)
        f = part if f is None else f + part
    o_ref[...] = h1 + _rms(f, g_mpost_ref[...])


def _out_mlp(att, y, z, x2d, g_ssm, wout, g_post, g_mpre, wup, wdown, g_mpost):
    seq = x2d.shape[0]
    weights = (g_ssm, wout, g_post, g_mpre, wup, wdown, g_mpost)
    row = lambda i: (i, 0)
    return pl.pallas_call(
        _out_mlp_kernel,
        grid=(seq // MLP_BLOCK,),
        in_specs=[pl.BlockSpec((MLP_BLOCK, ATT_WIDTH), row),
                  pl.BlockSpec((MLP_BLOCK, SSM_WIDTH), row),
                  pl.BlockSpec((MLP_BLOCK, SSM_WIDTH), row),
                  pl.BlockSpec((MLP_BLOCK, D_MODEL), row)] + [_const_spec(w.shape) for w in weights],
        out_specs=pl.BlockSpec((MLP_BLOCK, D_MODEL), row),
        out_shape=jax.ShapeDtypeStruct((seq, D_MODEL), F32),
        compiler_params=pltpu.CompilerParams(dimension_semantics=("arbitrary",),
                                             vmem_limit_bytes=VMEM_LIMIT_BYTES),
        name="out_proj_mlp",
    )(att, y, z, x2d, *weights)


def _rot_cols(w):
    half = QK_ROPE // 2
    return jnp.concatenate([-w[..., half:], w[..., :half]], axis=-1)


def _lane_pad(v, width=LANES):
    return jnp.pad(v, [(0, 0)] * (v.ndim - 1) + [(0, width - v.shape[-1])])


def kernel(x, meta_tokens, norm_mix_pre, w_in, q_a_norm, w_q_up, kv_a_norm, w_kv_up, conv_w, conv_b, dt_bias,
           a_log, d_skip, ssm_norm, w_out, norm_mix_post, norm_mlp_pre, w_mlp_up, w_mlp_down, norm_mlp_post):
    bsz, seq, _ = x.shape
    assert bsz == 1 and norm_mix_pre.shape[0] == 1, "single sequence, single layer"
    assert seq % ATT_Q == 0 and seq % ROW_BLOCK == 0 and seq % MLP_BLOCK == 0
    assert PAD == ROW_BLOCK, "the front pad is exactly the first block of the projection / SSD grids"
    x2d = x[0]
    row = lambda v: v.reshape(1, -1)

    o_ckv, o_kr, o_z, o_xbc, o_dt = Q_LORA, Q_LORA + KV_LORA, Q_LORA + KV_LORA + QK_ROPE, \
        Q_LORA + KV_LORA + QK_ROPE + SSM_WIDTH, Q_LORA + KV_LORA + QK_ROPE + SSM_WIDTH + CONV_DIM
    w_in0 = w_in[0]
    w_kr = w_in0[:, o_kr:o_z]
    wsm = jnp.concatenate([w_kr, _rot_cols(w_kr), _lane_pad(w_in0[:, o_dt:])], axis=1)
    wq3 = w_q_up[0].reshape(Q_LORA, ATT_HEADS, QK_DIM)
    wq = jnp.concatenate([wq3, _rot_cols(wq3[..., QK_NOPE:])], axis=-1).reshape(Q_LORA, ATT_HEADS * QK_PAD)
    wkv3 = w_kv_up[0].reshape(KV_LORA, ATT_HEADS, QK_NOPE + V_HEAD)
    wkt = wkv3[..., :QK_NOPE].reshape(KV_LORA, ATT_HEADS * QK_NOPE).T
    wv = wkv3[..., QK_NOPE:].reshape(KV_LORA, ATT_HEADS * V_HEAD)

    inv_freq = ROPE_THETA ** (-jnp.arange(0, QK_ROPE, 2, dtype=F32) / QK_ROPE)
    freq = jnp.tile(inv_freq, 4)
    low = jnp.arange(LANES) < QK_ROPE
    base = (jnp.arange(seq // ROW_BLOCK + 1) * ROW_BLOCK - (PAD - N_META)).astype(F32)
    ang_b = base[:, None] * freq[None, :]
    rope_blk = jnp.stack([jnp.cos(ang_b), jnp.where(low, -jnp.sin(ang_b), jnp.sin(ang_b))], axis=1)
    rope_blk = jnp.pad(rope_blk, ((0, 0), (0, SUBLANES - 2), (0, 0)))
    ang_l = jnp.arange(ROW_BLOCK, dtype=F32)[:, None] * freq[None, :]
    rope_loc = jnp.stack([jnp.where(low, jnp.cos(ang_l), jnp.sin(ang_l)),
                          jnp.where(low, jnp.sin(ang_l), jnp.cos(ang_l))])

    meta_pad = jnp.concatenate([jnp.zeros((PAD - N_META, D_MODEL), F32), meta_tokens.astype(F32)], axis=0)

    q, k, v, z, xc, dt_raw = _input_projection(
        x2d, meta_pad, rope_blk, rope_loc, row(norm_mix_pre[0]),
        w_in0[:, :o_ckv].astype(BF16), w_in0[:, o_ckv:o_kr].astype(BF16), w_in0[:, o_z:o_xbc].astype(BF16),
        w_in0[:, o_xbc:o_dt].astype(BF16), wsm.astype(BF16),
        row(q_a_norm[0]), row(kv_a_norm[0]), wq.astype(BF16), wkt.astype(BF16), wv.astype(BF16),
        conv_w[0], row(conv_b[0]))

    att = _attention(q, k, v)

    tril = jnp.tril(jnp.ones((SSD_CHUNK, SSD_CHUNK), BF16))
    expand = (jnp.arange(LANES)[:, None] == jnp.arange(SSM_WIDTH)[None, :] // SSM_HEAD_DIM).astype(BF16)
    y = _ssd_scan(xc, dt_raw, _lane_pad(row(dt_bias[0])), _lane_pad(row(a_log[0])),
                  row(jnp.repeat(d_skip[0], SSM_HEAD_DIM)), tril, expand, seq)

    out = _out_mlp(att, y, z, x2d, row(ssm_norm[0]), w_out[0].astype(BF16), row(norm_mix_post[0]),
                   row(norm_mlp_pre[0]), w_mlp_up[0].astype(BF16), w_mlp_down[0].astype(BF16),
                   row(norm_mlp_post[0]))
    return out[None]
```

```python
import math

import jax
import jax.numpy as jnp
from jax import lax
from jax.experimental import pallas as pl
from jax.experimental.pallas import tpu as pltpu

D_MODEL = 1024
N_META = 16
EPS = 1e-6
ATT_HEADS = 8
Q_LORA = 384
KV_LORA = 256
QK_NOPE = 128
QK_ROPE = 64
V_HEAD = 128
QK_DIM = QK_NOPE + QK_ROPE
ROPE_THETA = 10000.0
ATT_WIDTH = ATT_HEADS * V_HEAD
SSM_HEADS = 16
SSM_HEAD_DIM = 64
SSM_WIDTH = SSM_HEADS * SSM_HEAD_DIM
SSM_GROUPS = 2
SSM_STATE = 128
GROUP_WIDTH = SSM_WIDTH // SSM_GROUPS
CONV_K = 4
CONV_DIM = SSM_WIDTH + 2 * SSM_GROUPS * SSM_STATE
D_FF = 4 * D_MODEL

LANES = 128
SUBLANES = 8
VMEM_LIMIT_BYTES = 56 * 1024 * 1024

QK_PAD = 2 * LANES
PAD = 512
ROW_BLOCK = 512
MLP_BLOCK = 512
ATT_Q = 512
ATT_K = 512
ATT_UNROLL = 8
SSD_CHUNK = 128
FF_CHUNK = 1024
ATT_SCALE_LOG2 = (QK_DIM ** -0.5) * math.log2(math.e)

BF16 = jnp.bfloat16
F32 = jnp.float32


def _const_spec(shape):
    nd = len(shape)
    block = (None,) + tuple(shape[1:]) if nd == 3 and shape[0] == 1 else tuple(shape)
    return pl.BlockSpec(block, lambda *_: (0,) * nd, pipeline_mode=pl.Buffered(1))


def _rms(x, gain):
    return x * lax.rsqrt(jnp.mean(jnp.square(x), axis=-1, keepdims=True) + EPS) * gain


def _proj_kernel(meta_ref, x_ref, rope_blk_ref, rope_loc_ref, g_pre_ref, wqa_ref, wkva_ref, wz_ref, wxbc_ref,
                 wsm_ref, gq_ref, gkv_ref, wq_ref, wkt_ref, wv_ref, convw_ref, convb_ref,
                 q_ref, k_ref, v_ref, z_ref, xbc_ref, dt_ref, stage_ref):
    i = pl.program_id(0)
    rows_b = x_ref.shape[0]

    @pl.when(i == 0)
    def _():
        stage_ref[...] = jnp.zeros_like(stage_ref)

    u = stage_ref[...]
    u1 = pltpu.roll(u, 1, axis=0)
    older = pltpu.roll(convw_ref[1:2, :] * u + convw_ref[0:1, :] * u1, 2, axis=0)
    conv = (convw_ref[3:4, :] * u + convw_ref[2:3, :] * u1 + older + convb_ref[...])[SUBLANES:, :]
    xbc_ref[...] = conv * jax.nn.sigmoid(conv)
    stage_ref[0:SUBLANES, :] = stage_ref[rows_b:rows_b + SUBLANES, :]

    xin = jnp.where(i == 0, meta_ref[...], x_ref[...])
    hn = _rms(xin, g_pre_ref[...]).astype(BF16)

    blk = rope_blk_ref[0]
    cs = blk[0:1, :] * rope_loc_ref[0] + blk[1:2, :] * rope_loc_ref[1]
    low = lax.broadcasted_iota(jnp.int32, cs.shape, 1) < QK_ROPE

    def rope(t):
        prod = t * cs
        return prod + pltpu.roll(prod, QK_ROPE, axis=1)

    z_ref[...] = jnp.dot(hn, wz_ref[...], preferred_element_type=F32)
    stage_ref[SUBLANES:, :] = jnp.dot(hn, wxbc_ref[...], preferred_element_type=F32)

    small = jnp.dot(hn, wsm_ref[...], preferred_element_type=F32)
    dt_ref[...] = small[:, LANES:]
    k_rope_t = jnp.where(low, rope(small[:, :LANES]), 0.0).T.astype(BF16)

    c_q = jnp.dot(hn, wqa_ref[...], preferred_element_type=F32)
    cqn = _rms(c_q, gq_ref[...]).astype(BF16)
    c_kv = jnp.dot(hn, wkva_ref[...], preferred_element_type=F32)
    ckvn = _rms(c_kv, gkv_ref[...])
    k_nope_t = jnp.dot(wkt_ref[...], ckvn.T.astype(BF16), preferred_element_type=F32)
    v_all = jnp.dot(ckvn.astype(BF16), wv_ref[...], preferred_element_type=F32)

    for h in range(ATT_HEADS):
        qh = jnp.dot(cqn, wq_ref[:, h * QK_PAD:(h + 1) * QK_PAD], preferred_element_type=F32)
        q_ref[h, :, :QK_NOPE] = (qh[:, :QK_NOPE] * ATT_SCALE_LOG2).astype(BF16)
        q_ref[h, :, QK_NOPE:] = (rope(qh[:, QK_NOPE:]) * ATT_SCALE_LOG2).astype(BF16)
        k_ref[h, :QK_NOPE, :] = k_nope_t[h * QK_NOPE:(h + 1) * QK_NOPE, :].astype(BF16)
        k_ref[h, QK_NOPE:, :] = k_rope_t
        v_ref[h] = v_all[:, h * V_HEAD:(h + 1) * V_HEAD].astype(BF16)


def _input_projection(x2d, meta_pad, rope_blk, rope_loc, g_pre, wqa, wkva, wz, wxbc, wsm, gq, gkv, wq, wkt, wv,
                      conv_w, conv_b):
    seq = x2d.shape[0]
    n_blocks = seq // ROW_BLOCK + 1
    rows_p = seq + PAD
    pblk = lambda i: jnp.minimum(i, n_blocks - 1)
    rblk = lambda i: jnp.maximum(pblk(i) - 1, 0)
    real = lambda i: (rblk(i), 0)
    real3 = lambda i: (0, rblk(i), 0)
    padded = lambda i: (pblk(i), 0)
    padded3 = lambda i: (0, pblk(i), 0)
    lagged = lambda i: (jnp.maximum(i - 1, 0), 0)
    weights = (g_pre, wqa, wkva, wz, wxbc, wsm, gq, gkv, wq, wkt, wv, conv_w, conv_b)
    return pl.pallas_call(
        _proj_kernel,
        grid=(n_blocks + 1,),
        in_specs=[_const_spec(meta_pad.shape),
                  pl.BlockSpec((ROW_BLOCK, D_MODEL), real),
                  pl.BlockSpec((1, SUBLANES, LANES), lambda i: (pblk(i), 0, 0)),
                  _const_spec(rope_loc.shape)] + [_const_spec(w.shape) for w in weights],
        out_specs=[pl.BlockSpec((ATT_HEADS, ROW_BLOCK, QK_PAD), real3),
                   pl.BlockSpec((ATT_HEADS, QK_PAD, ROW_BLOCK), lambda i: (0, 0, pblk(i))),
                   pl.BlockSpec((ATT_HEADS, ROW_BLOCK, V_HEAD), padded3),
                   pl.BlockSpec((ROW_BLOCK, SSM_WIDTH), real),
                   pl.BlockSpec((ROW_BLOCK, CONV_DIM), lagged),
                   pl.BlockSpec((ROW_BLOCK, LANES), padded)],
        out_shape=[jax.ShapeDtypeStruct((ATT_HEADS, seq, QK_PAD), BF16),
                   jax.ShapeDtypeStruct((ATT_HEADS, QK_PAD, rows_p), BF16),
                   jax.ShapeDtypeStruct((ATT_HEADS, rows_p, V_HEAD), BF16),
                   jax.ShapeDtypeStruct((seq, SSM_WIDTH), F32),
                   jax.ShapeDtypeStruct((rows_p, CONV_DIM), F32),
                   jax.ShapeDtypeStruct((rows_p, LANES), F32)],
        scratch_shapes=[pltpu.VMEM((SUBLANES + ROW_BLOCK, CONV_DIM), F32)],
        compiler_params=pltpu.CompilerParams(dimension_semantics=("arbitrary",),
                                             vmem_limit_bytes=VMEM_LIMIT_BYTES),
        name="input_projection",
    )(meta_pad, x2d, rope_blk, rope_loc, *weights)


def _attn_kernel(q_ref, k_ref, v_ref, o_ref, sm_ref, pm_ref, s_ref, p_ref, m_ref, l_ref, acc_ref):
    i = pl.program_id(1)
    n_tiles = ATT_K // LANES
    assert ATT_Q == ATT_K, "one diagonal chunk per query block"
    assert ATT_UNROLL % 2 == 0 and PAD >= LANES

    def chunk_start(j):
        return pl.multiple_of(PAD + j * ATT_K, math.gcd(PAD, ATT_K))

    def scores(buf, j):
        s_ref[buf] = jnp.dot(q_ref[0], k_ref[0, :, pl.ds(chunk_start(j), ATT_K)], preferred_element_type=F32)

    def score_tiles(buf, last):
        tiles = [s_ref[buf, :, t * LANES:(t + 1) * LANES] for t in range(n_tiles)]
        if last:
            rows = lax.broadcasted_iota(jnp.int32, (ATT_Q, LANES), 0)
            cols = lax.broadcasted_iota(jnp.int32, (ATT_Q, LANES), 1)
            tiles = [jnp.where(cols + t * LANES <= rows, tiles[t], -jnp.inf) for t in range(n_tiles)]
            tiles = [jnp.where(cols >= LANES - N_META, sm_ref[...], -jnp.inf)] + tiles
        return tiles

    def consume(buf, j, last=False):
        tiles = score_tiles(buf, last)
        mx = tiles[0]
        for t in range(1, len(tiles)):
            mx = jnp.maximum(mx, tiles[t])
        m_old = m_ref[...]
        m_new = jnp.maximum(m_old, jnp.max(mx, axis=1, keepdims=True))
        m_ref[...] = m_new
        alpha = jnp.exp2(m_old - m_new)
        tiles = score_tiles(buf, last)
        probs = [jnp.exp2(t - m_new) for t in tiles]
        lsum = probs[0]
        for p in probs[1:]:
            lsum = lsum + p
        l_ref[...] = alpha * l_ref[...] + lsum
        pv = None
        if last:
            pm_ref[...] = probs.pop(0).astype(BF16)
            pv = jnp.dot(pm_ref[...], v_ref[0, PAD - LANES:PAD, :], preferred_element_type=F32)
        for t, p in enumerate(probs):
            p_ref[:, t * LANES:(t + 1) * LANES] = p.astype(BF16)
        pv_chunk = jnp.dot(p_ref[...], v_ref[0, pl.ds(chunk_start(j), ATT_K), :], preferred_element_type=F32)
        acc_ref[...] = alpha * acc_ref[...] + (pv_chunk if pv is None else pv + pv_chunk)

    def run(first, count):
        for u in range(count):
            scores((u + 1) % 2, first + u + 1)
            consume(u % 2, first + u)

    m_ref[...] = jnp.full((ATT_Q, LANES), -jnp.inf, F32)
    l_ref[...] = jnp.zeros((ATT_Q, LANES), F32)
    acc_ref[...] = jnp.zeros((ATT_Q, V_HEAD), F32)
    sm_ref[...] = jnp.dot(q_ref[0], k_ref[0, :, PAD - LANES:PAD], preferred_element_type=F32)
    scores(0, 0)

    def trip(t, carry):
        run(ATT_UNROLL * t, ATT_UNROLL)
        return carry

    lax.fori_loop(0, i // ATT_UNROLL, trip, 0)

    done = i - i % ATT_UNROLL
    size = ATT_UNROLL // 2
    while size >= 2:
        @pl.when((i - done) >= size)
        def _():
            run(done, size)
        done = done + jnp.where((i - done) >= size, size, 0)
        size //= 2

    @pl.when(i - done >= 1)
    def _():
        run(done, 1)

    for parity in range(2):
        @pl.when(i % 2 == parity)
        def _():
            consume(parity, i, last=True)
            o_ref[...] = (acc_ref[...] / jnp.sum(l_ref[...], axis=1, keepdims=True)).astype(o_ref.dtype)


def _attention(q, k, v):
    heads, seq, _ = q.shape
    rows_p = v.shape[1]
    n_q = seq // ATT_Q
    return pl.pallas_call(
        _attn_kernel,
        grid=(heads, n_q),
        in_specs=[pl.BlockSpec((1, ATT_Q, QK_PAD), lambda h, i: (h, i, 0)),
                  pl.BlockSpec((1, QK_PAD, rows_p), lambda h, i: (h, 0, 0)),
                  pl.BlockSpec((1, rows_p, V_HEAD), lambda h, i: (h, 0, 0))],
        out_specs=pl.BlockSpec((ATT_Q, V_HEAD), lambda h, i: (i, h)),
        out_shape=jax.ShapeDtypeStruct((seq, ATT_WIDTH), BF16),
        scratch_shapes=[pltpu.VMEM((ATT_Q, LANES), F32),
                        pltpu.VMEM((ATT_Q, LANES), BF16),
                        pltpu.VMEM((2, ATT_Q, ATT_K), F32),
                        pltpu.VMEM((ATT_Q, ATT_K), BF16),
                        pltpu.VMEM((ATT_Q, LANES), F32),
                        pltpu.VMEM((ATT_Q, LANES), F32),
                        pltpu.VMEM((ATT_Q, V_HEAD), F32)],
        compiler_params=pltpu.CompilerParams(dimension_semantics=("arbitrary", "arbitrary"),
                                             vmem_limit_bytes=VMEM_LIMIT_BYTES),
        name="causal_attention",
    )(q, k, v)


def _split_dot(v, mat, terms):
    out = None
    rem = v
    for t in range(terms):
        piece = rem.astype(BF16)
        part = jnp.dot(piece, mat, preferred_element_type=F32)
        out = part if out is None else out + part
        if t + 1 < terms:
            rem = rem - piece.astype(F32)
    return out


def _ssd_kernel(xc_ref, dt_ref, dtb_ref, alog_ref, dskip_ref, tril_ref, expand_ref, y_ref, state_ref):
    i = pl.program_id(0)
    rows_b = xc_ref.shape[0]

    @pl.when(i == 0)
    def _():
        state_ref[...] = jnp.zeros_like(state_ref)

    xc = xc_ref

    lane = lax.broadcasted_iota(jnp.int32, (SSD_CHUNK, LANES), 1)
    row = lax.broadcasted_iota(jnp.int32, (SSD_CHUNK, LANES), 0)
    causal = lax.broadcasted_iota(jnp.int32, (SSD_CHUNK, SSD_CHUNK), 0) >= lax.broadcasted_iota(
        jnp.int32, (SSD_CHUNK, SSD_CHUNK), 1)
    low_half = lane < SSM_HEAD_DIM
    a_row = -jnp.exp(alog_ref[...])
    tril = tril_ref[...]
    expand = expand_ref[...]
    heads_per_group = SSM_HEADS // SSM_GROUPS

    for c in range(rows_b // SSD_CHUNK):
        r0 = c * SSD_CHUNK
        xs = xc[r0:r0 + SSD_CHUNK, :SSM_WIDTH]
        valid = (lane < SSM_HEADS) & (row + (i * rows_b + r0) >= PAD - N_META)
        dt = jnp.where(valid, jax.nn.softplus(dt_ref[r0:r0 + SSD_CHUNK, :] + dtb_ref[...]), 0.0)
        da = dt * a_row
        acum = _cumsum_rows(tril, da)
        acum_t = acum.T
        dt_t = dt.T
        a_last = acum[SSD_CHUNK - 1:SSD_CHUNK, :]
        e_acum = _split_dot(jnp.exp(acum), expand, 2)
        w_end = _split_dot(jnp.exp(a_last - acum) * dt, expand, 2)
        c_decay = _split_dot(jnp.broadcast_to(jnp.exp(a_last), (SUBLANES, LANES)), expand, 2)[0:1, :]

        y_diag = []
        y_off = []
        for g in range(SSM_GROUPS):
            b_g = xc[r0:r0 + SSD_CHUNK, SSM_WIDTH + g * SSM_STATE:SSM_WIDTH + (g + 1) * SSM_STATE]
            c_g = xc[r0:r0 + SSD_CHUNK, SSM_WIDTH + (SSM_GROUPS + g) * SSM_STATE:
                     SSM_WIDTH + (SSM_GROUPS + g + 1) * SSM_STATE].astype(BF16)
            b_t = b_g.T.astype(BF16)
            cb = jnp.dot(c_g, b_t, preferred_element_type=F32)
            gcols = slice(g * GROUP_WIDTH, (g + 1) * GROUP_WIDTH)
            for jp in range(heads_per_group // 2):
                ws = []
                for r in (g * heads_per_group + 2 * jp, g * heads_per_group + 2 * jp + 1):
                    seg = acum[:, r:r + 1] - acum_t[r:r + 1, :]
                    decay = jnp.where(causal, jnp.exp(seg), 0.0)
                    ws.append((cb * decay * dt_t[r:r + 1, :]).astype(BF16))
                col = g * GROUP_WIDTH + jp * LANES
                x_pair = xs[:, col:col + LANES]
                x_cat = jnp.concatenate([jnp.where(low_half, x_pair, 0.0), jnp.where(low_half, 0.0, x_pair)],
                                        axis=0).astype(BF16)
                y_diag.append(jnp.dot(jnp.concatenate(ws, axis=1), x_cat, preferred_element_type=F32))
            h_prev = state_ref[g]
            y_off.append(jnp.dot(c_g, h_prev.astype(BF16), preferred_element_type=F32))
            x_end = (xs[:, gcols] * w_end[:, gcols]).astype(BF16)
            state_ref[g] = c_decay[:, gcols] * h_prev + jnp.dot(b_t, x_end, preferred_element_type=F32)
        y = (jnp.concatenate(y_diag, axis=1) + jnp.concatenate(y_off, axis=1) * e_acum
             + xs * dskip_ref[...])
        y_ref[r0:r0 + SSD_CHUNK, :] = y


def _cumsum_rows(tril, v):
    out = None
    rem = v
    for t in range(3):
        piece = rem.astype(BF16)
        part = jnp.dot(tril, piece, preferred_element_type=F32)
        out = part if out is None else out + part
        if t < 2:
            rem = rem - piece.astype(F32)
    return out


def _ssd_scan(xc, dt_raw, dt_bias, a_log, d_skip_x, tril, expand, seq):
    n_blocks = seq // ROW_BLOCK + 1
    consts = (dt_bias, a_log, d_skip_x, tril, expand)
    return pl.pallas_call(
        _ssd_kernel,
        grid=(n_blocks,),
        in_specs=[pl.BlockSpec((ROW_BLOCK, CONV_DIM), lambda i: (i, 0)),
                  pl.BlockSpec((ROW_BLOCK, LANES), lambda i: (i, 0))] + [_const_spec(c.shape) for c in consts],
        out_specs=pl.BlockSpec((ROW_BLOCK, SSM_WIDTH), lambda i: (jnp.maximum(i - 1, 0), 0)),
        out_shape=jax.ShapeDtypeStruct((seq, SSM_WIDTH), F32),
        scratch_shapes=[pltpu.VMEM((SSM_GROUPS, SSM_STATE, GROUP_WIDTH), F32)],
        compiler_params=pltpu.CompilerParams(dimension_semantics=("arbitrary",),
                                             vmem_limit_bytes=VMEM_LIMIT_BYTES),
        name="ssd_scan",
    )(xc, dt_raw, *consts)


def _out_mlp_kernel(att_ref, y_ref, z_ref, x_ref, g_ssm_ref, wout_ref, g_post_ref, g_mpre_ref,
                    wup_ref, wdown_ref, g_mpost_ref, o_ref):
    z = z_ref[...]
    gated = y_ref[...] * (z * jax.nn.sigmoid(z))
    parts = []
    for g in range(SSM_GROUPS):
        gg = gated[:, g * GROUP_WIDTH:(g + 1) * GROUP_WIDTH]
        parts.append(gg * lax.rsqrt(jnp.mean(jnp.square(gg), axis=-1, keepdims=True) + EPS))
    ssm = (jnp.concatenate(parts, axis=1) * g_ssm_ref[...]).astype(BF16)
    mix = (jnp.dot(att_ref[...], wout_ref[:ATT_WIDTH, :], preferred_element_type=F32)
           + jnp.dot(ssm, wout_ref[ATT_WIDTH:, :], preferred_element_type=F32))
    h1 = x_ref[...] + _rms(mix, g_post_ref[...])
    hn = _rms(h1, g_mpre_ref[...]).astype(BF16)
    f = None
    for c in range(D_FF // FF_CHUNK):
        u = jnp.dot(hn, wup_ref[:, c * FF_CHUNK:(c + 1) * FF_CHUNK], preferred_element_type=F32)
        u = jnp.square(jnp.maximum(u, 0.0)).astype(BF16)
        part = jnp.dot(u, wdown_ref[c * FF_CHUNK:(c + 1) * FF_CHUNK, :], preferred_element_type=F32)
        f = part if f is None else f + part
    o_ref[...] = h1 + _rms(f, g_mpost_ref[...])


def _out_mlp(att, y, z, x2d, g_ssm, wout, g_post, g_mpre, wup, wdown, g_mpost):
    seq = x2d.shape[0]
    weights = (g_ssm, wout, g_post, g_mpre, wup, wdown, g_mpost)
    row = lambda i: (i, 0)
    return pl.pallas_call(
        _out_mlp_kernel,
        grid=(seq // MLP_BLOCK,),
        in_specs=[pl.BlockSpec((MLP_BLOCK, ATT_WIDTH), row),
                  pl.BlockSpec((MLP_BLOCK, SSM_WIDTH), row),
                  pl.BlockSpec((MLP_BLOCK, SSM_WIDTH), row),
                  pl.BlockSpec((MLP_BLOCK, D_MODEL), row)] + [_const_spec(w.shape) for w in weights],
        out_specs=pl.BlockSpec((MLP_BLOCK, D_MODEL), row),
        out_shape=jax.ShapeDtypeStruct((seq, D_MODEL), F32),
        compiler_params=pltpu.CompilerParams(dimension_semantics=("arbitrary",),
                                             vmem_limit_bytes=VMEM_LIMIT_BYTES),
        name="out_proj_mlp",
    )(att, y, z, x2d, *weights)


def _rot_cols(w):
    half = QK_ROPE // 2
    return jnp.concatenate([-w[..., half:], w[..., :half]], axis=-1)


def _lane_pad(v, width=LANES):
    return jnp.pad(v, [(0, 0)] * (v.ndim - 1) + [(0, width - v.shape[-1])])


def kernel(x, meta_tokens, norm_mix_pre, w_in, q_a_norm, w_q_up, kv_a_norm, w_kv_up, conv_w, conv_b, dt_bias,
           a_log, d_skip, ssm_norm, w_out, norm_mix_post, norm_mlp_pre, w_mlp_up, w_mlp_down, norm_mlp_post):
    bsz, seq, _ = x.shape
    assert bsz == 1 and norm_mix_pre.shape[0] == 1, "single sequence, single layer"
    assert seq % ATT_Q == 0 and seq % ROW_BLOCK == 0 and seq % MLP_BLOCK == 0
    assert PAD == ROW_BLOCK, "the front pad is exactly the first block of the projection / SSD grids"
    x2d = x[0]
    row = lambda v: v.reshape(1, -1)

    o_ckv, o_kr, o_z, o_xbc, o_dt = Q_LORA, Q_LORA + KV_LORA, Q_LORA + KV_LORA + QK_ROPE, \
        Q_LORA + KV_LORA + QK_ROPE + SSM_WIDTH, Q_LORA + KV_LORA + QK_ROPE + SSM_WIDTH + CONV_DIM
    w_in0 = w_in.astype(BF16)[0]
    w_kr = w_in0[:, o_kr:o_z]
    wsm = jnp.concatenate([w_kr, _rot_cols(w_kr), _lane_pad(w_in0[:, o_dt:])], axis=1)
    wq3 = w_q_up[0].reshape(Q_LORA, ATT_HEADS, QK_DIM)
    wq = jnp.concatenate([wq3, _rot_cols(wq3[..., QK_NOPE:])], axis=-1).reshape(Q_LORA, ATT_HEADS * QK_PAD)
    wkv3 = w_kv_up[0].reshape(KV_LORA, ATT_HEADS, QK_NOPE + V_HEAD)
    wkt = wkv3[..., :QK_NOPE].reshape(KV_LORA, ATT_HEADS * QK_NOPE).T
    wv = wkv3[..., QK_NOPE:].reshape(KV_LORA, ATT_HEADS * V_HEAD)

    inv_freq = ROPE_THETA ** (-jnp.arange(0, QK_ROPE, 2, dtype=F32) / QK_ROPE)
    freq = jnp.tile(inv_freq, 4)
    low = jnp.arange(LANES) < QK_ROPE
    base = (jnp.arange(seq // ROW_BLOCK + 1) * ROW_BLOCK - (PAD - N_META)).astype(F32)
    ang_b = base[:, None] * freq[None, :]
    rope_blk = jnp.stack([jnp.cos(ang_b), jnp.where(low, -jnp.sin(ang_b), jnp.sin(ang_b))], axis=1)
    rope_blk = jnp.pad(rope_blk, ((0, 0), (0, SUBLANES - 2), (0, 0)))
    ang_l = jnp.arange(ROW_BLOCK, dtype=F32)[:, None] * freq[None, :]
    rope_loc = jnp.stack([jnp.where(low, jnp.cos(ang_l), jnp.sin(ang_l)),
                          jnp.where(low, jnp.sin(ang_l), jnp.cos(ang_l))])

    meta_pad = jnp.concatenate([jnp.zeros((PAD - N_META, D_MODEL), F32), meta_tokens.astype(F32)], axis=0)

    q, k, v, z, xc, dt_raw = _input_projection(
        x2d, meta_pad, rope_blk, rope_loc, row(norm_mix_pre[0]),
        w_in0[:, :o_ckv].astype(BF16), w_in0[:, o_ckv:o_kr].astype(BF16), w_in0[:, o_z:o_xbc].astype(BF16),
        w_in0[:, o_xbc:o_dt].astype(BF16), wsm.astype(BF16),
        row(q_a_norm[0]), row(kv_a_norm[0]), wq.astype(BF16), wkt.astype(BF16), wv.astype(BF16),
        conv_w[0], row(conv_b[0]))

    att = _attention(q, k, v)

    tril = jnp.tril(jnp.ones((SSD_CHUNK, SSD_CHUNK), BF16))
    expand = (jnp.arange(LANES)[:, None] == jnp.arange(SSM_WIDTH)[None, :] // SSM_HEAD_DIM).astype(BF16)
    y = _ssd_scan(xc, dt_raw, _lane_pad(row(dt_bias[0])), _lane_pad(row(a_log[0])),
                  row(jnp.repeat(d_skip[0], SSM_HEAD_DIM)), tril, expand, seq)

    out = _out_mlp(att, y, z, x2d, row(ssm_norm[0]), w_out.astype(BF16), row(norm_mix_post[0]),
                   row(norm_mlp_pre[0]), w_mlp_up.astype(BF16), w_mlp_down.astype(BF16),
                   row(norm_mlp_post[0]))
    return out[None]
```

```python
import math

import jax
import jax.numpy as jnp
from jax import lax
from jax.experimental import pallas as pl
from jax.experimental.pallas import tpu as pltpu

D_MODEL = 1024
N_META = 16
EPS = 1e-6
ATT_HEADS = 8
Q_LORA = 384
KV_LORA = 256
QK_NOPE = 128
QK_ROPE = 64
V_HEAD = 128
QK_DIM = QK_NOPE + QK_ROPE
ROPE_THETA = 10000.0
ATT_WIDTH = ATT_HEADS * V_HEAD
SSM_HEADS = 16
SSM_HEAD_DIM = 64
SSM_WIDTH = SSM_HEADS * SSM_HEAD_DIM
SSM_GROUPS = 2
SSM_STATE = 128
GROUP_WIDTH = SSM_WIDTH // SSM_GROUPS
CONV_K = 4
CONV_DIM = SSM_WIDTH + 2 * SSM_GROUPS * SSM_STATE
D_FF = 4 * D_MODEL

LANES = 128
SUBLANES = 8
VMEM_LIMIT_BYTES = 56 * 1024 * 1024

QK_PAD = 2 * LANES
PAD = 512
ROW_BLOCK = 512
MLP_BLOCK = 512
ATT_Q = 512
ATT_K = 512
ATT_UNROLL = 8
SSD_CHUNK = 128
FF_CHUNK = 1024
ATT_SCALE_LOG2 = (QK_DIM ** -0.5) * math.log2(math.e)

BF16 = jnp.bfloat16
F32 = jnp.float32


def _const_spec(shape):
    nd = len(shape)
    block = (None,) + tuple(shape[1:]) if nd == 3 and shape[0] == 1 else tuple(shape)
    return pl.BlockSpec(block, lambda *_: (0,) * nd, pipeline_mode=pl.Buffered(1))


def _rms(x, gain):
    return x * lax.rsqrt(jnp.mean(jnp.square(x), axis=-1, keepdims=True) + EPS) * gain


def _proj_kernel(meta_ref, x_ref, rope_blk_ref, rope_loc_ref, g_pre_ref, wqa_ref, wkva_ref, wz_ref, wxbc_ref,
                 wsm_ref, gq_ref, gkv_ref, wq_ref, wkt_ref, wv_ref, convw_ref, convb_ref,
                 q_ref, k_ref, v_ref, z_ref, xbc_ref, dt_ref, stage_ref):
    i = pl.program_id(0)
    rows_b = x_ref.shape[0]

    @pl.when(i == 0)
    def _():
        stage_ref[...] = jnp.zeros_like(stage_ref)

    u = stage_ref[...]
    u1 = pltpu.roll(u, 1, axis=0)
    older = pltpu.roll(convw_ref[1:2, :] * u + convw_ref[0:1, :] * u1, 2, axis=0)
    conv = (convw_ref[3:4, :] * u + convw_ref[2:3, :] * u1 + older + convb_ref[...])[SUBLANES:, :]
    xbc_ref[...] = conv * jax.nn.sigmoid(conv)
    stage_ref[0:SUBLANES, :] = stage_ref[rows_b:rows_b + SUBLANES, :]

    xin = jnp.where(i == 0, meta_ref[...], x_ref[...])
    hn = _rms(xin, g_pre_ref[...]).astype(BF16)

    blk = rope_blk_ref[0]
    cs = blk[0:1, :] * rope_loc_ref[0] + blk[1:2, :] * rope_loc_ref[1]
    low = lax.broadcasted_iota(jnp.int32, cs.shape, 1) < QK_ROPE

    def rope(t):
        prod = t * cs
        return prod + pltpu.roll(prod, QK_ROPE, axis=1)

    z_ref[...] = jnp.dot(hn, wz_ref[...], preferred_element_type=F32)
    stage_ref[SUBLANES:, :] = jnp.dot(hn, wxbc_ref[...], preferred_element_type=F32)

    small = jnp.dot(hn, wsm_ref[...], preferred_element_type=F32)
    dt_ref[...] = small[:, LANES:]
    k_rope_t = jnp.where(low, rope(small[:, :LANES]), 0.0).T.astype(BF16)

    c_q = jnp.dot(hn, wqa_ref[...], preferred_element_type=F32)
    cqn = _rms(c_q, gq_ref[...]).astype(BF16)
    c_kv = jnp.dot(hn, wkva_ref[...], preferred_element_type=F32)
    ckvn = _rms(c_kv, gkv_ref[...])
    k_nope_t = jnp.dot(wkt_ref[...], ckvn.T.astype(BF16), preferred_element_type=F32)
    v_all = jnp.dot(ckvn.astype(BF16), wv_ref[...], preferred_element_type=F32)

    for h in range(ATT_HEADS):
        qh = jnp.dot(cqn, wq_ref[:, h * QK_PAD:(h + 1) * QK_PAD], preferred_element_type=F32)
        q_ref[h, :, :QK_NOPE] = (qh[:, :QK_NOPE] * ATT_SCALE_LOG2).astype(BF16)
        q_ref[h, :, QK_NOPE:] = (rope(qh[:, QK_NOPE:]) * ATT_SCALE_LOG2).astype(BF16)
        k_ref[h, :QK_NOPE, :] = k_nope_t[h * QK_NOPE:(h + 1) * QK_NOPE, :].astype(BF16)
        k_ref[h, QK_NOPE:, :] = k_rope_t
        v_ref[h] = v_all[:, h * V_HEAD:(h + 1) * V_HEAD].astype(BF16)


def _input_projection(x2d, meta_pad, rope_blk, rope_loc, g_pre, wqa, wkva, wz, wxbc, wsm, gq, gkv, wq, wkt, wv,
                      conv_w, conv_b):
    seq = x2d.shape[0]
    n_blocks = seq // ROW_BLOCK + 1
    rows_p = seq + PAD
    pblk = lambda i: jnp.minimum(i, n_blocks - 1)
    rblk = lambda i: jnp.maximum(pblk(i) - 1, 0)
    real = lambda i: (rblk(i), 0)
    real3 = lambda i: (0, rblk(i), 0)
    padded = lambda i: (pblk(i), 0)
    padded3 = lambda i: (0, pblk(i), 0)
    lagged = lambda i: (jnp.maximum(i - 1, 0), 0)
    weights = (g_pre, wqa, wkva, wz, wxbc, wsm, gq, gkv, wq, wkt, wv, conv_w, conv_b)
    return pl.pallas_call(
        _proj_kernel,
        grid=(n_blocks + 1,),
        in_specs=[_const_spec(meta_pad.shape),
                  pl.BlockSpec((ROW_BLOCK, D_MODEL), real),
                  pl.BlockSpec((1, SUBLANES, LANES), lambda i: (pblk(i), 0, 0)),
                  _const_spec(rope_loc.shape)] + [_const_spec(w.shape) for w in weights],
        out_specs=[pl.BlockSpec((ATT_HEADS, ROW_BLOCK, QK_PAD), real3),
                   pl.BlockSpec((ATT_HEADS, QK_PAD, ROW_BLOCK), lambda i: (0, 0, pblk(i))),
                   pl.BlockSpec((ATT_HEADS, ROW_BLOCK, V_HEAD), padded3),
                   pl.BlockSpec((ROW_BLOCK, SSM_WIDTH), real),
                   pl.BlockSpec((ROW_BLOCK, CONV_DIM), lagged),
                   pl.BlockSpec((ROW_BLOCK, LANES), padded)],
        out_shape=[jax.ShapeDtypeStruct((ATT_HEADS, seq, QK_PAD), BF16),
                   jax.ShapeDtypeStruct((ATT_HEADS, QK_PAD, rows_p), BF16),
                   jax.ShapeDtypeStruct((ATT_HEADS, rows_p, V_HEAD), BF16),
                   jax.ShapeDtypeStruct((seq, SSM_WIDTH), F32),
                   jax.ShapeDtypeStruct((rows_p, CONV_DIM), F32),
                   jax.ShapeDtypeStruct((rows_p, LANES), F32)],
        scratch_shapes=[pltpu.VMEM((SUBLANES + ROW_BLOCK, CONV_DIM), F32)],
        compiler_params=pltpu.CompilerParams(dimension_semantics=("arbitrary",),
                                             vmem_limit_bytes=VMEM_LIMIT_BYTES),
        name="input_projection",
    )(meta_pad, x2d, rope_blk, rope_loc, *weights)


def _attn_kernel(q_ref, k_ref, v_ref, o_ref, sm_ref, pm_ref, s_ref, p_ref, m_ref, l_ref, acc_ref):
    i = pl.program_id(1)
    n_tiles = ATT_K // LANES
    assert ATT_Q == ATT_K, "one diagonal chunk per query block"
    assert ATT_UNROLL % 2 == 0 and PAD >= LANES

    def chunk_start(j):
        return pl.multiple_of(PAD + j * ATT_K, math.gcd(PAD, ATT_K))

    def scores(buf, j):
        s_ref[buf] = jnp.dot(q_ref[0], k_ref[0, :, pl.ds(chunk_start(j), ATT_K)], preferred_element_type=F32)

    def score_tiles(buf, last):
        tiles = [s_ref[buf, :, t * LANES:(t + 1) * LANES] for t in range(n_tiles)]
        if last:
            rows = lax.broadcasted_iota(jnp.int32, (ATT_Q, LANES), 0)
            cols = lax.broadcasted_iota(jnp.int32, (ATT_Q, LANES), 1)
            tiles = [jnp.where(cols + t * LANES <= rows, tiles[t], -jnp.inf) for t in range(n_tiles)]
            tiles = [jnp.where(cols >= LANES - N_META, sm_ref[...], -jnp.inf)] + tiles
        return tiles

    def consume(buf, j, last=False):
        tiles = score_tiles(buf, last)
        mx = tiles[0]
        for t in range(1, len(tiles)):
            mx = jnp.maximum(mx, tiles[t])
        m_old = m_ref[...]
        m_new = jnp.maximum(m_old, jnp.max(mx, axis=1, keepdims=True))
        m_ref[...] = m_new
        alpha = jnp.exp2(m_old - m_new)
        tiles = score_tiles(buf, last)
        probs = [jnp.exp2(t - m_new) for t in tiles]
        lsum = probs[0]
        for p in probs[1:]:
            lsum = lsum + p
        l_ref[...] = alpha * l_ref[...] + lsum
        pv = None
        if last:
            pm_ref[...] = probs.pop(0).astype(BF16)
            pv = jnp.dot(pm_ref[...], v_ref[0, PAD - LANES:PAD, :], preferred_element_type=F32)
        for t, p in enumerate(probs):
            p_ref[:, t * LANES:(t + 1) * LANES] = p.astype(BF16)
        pv_chunk = jnp.dot(p_ref[...], v_ref[0, pl.ds(chunk_start(j), ATT_K), :], preferred_element_type=F32)
        acc_ref[...] = alpha * acc_ref[...] + (pv_chunk if pv is None else pv + pv_chunk)

    def run(first, count):
        for u in range(count):
            scores((u + 1) % 2, first + u + 1)
            consume(u % 2, first + u)

    m_ref[...] = jnp.full((ATT_Q, LANES), -jnp.inf, F32)
    l_ref[...] = jnp.zeros((ATT_Q, LANES), F32)
    acc_ref[...] = jnp.zeros((ATT_Q, V_HEAD), F32)
    sm_ref[...] = jnp.dot(q_ref[0], k_ref[0, :, PAD - LANES:PAD], preferred_element_type=F32)
    scores(0, 0)

    def trip(t, carry):
        run(ATT_UNROLL * t, ATT_UNROLL)
        return carry

    lax.fori_loop(0, i // ATT_UNROLL, trip, 0)

    done = i - i % ATT_UNROLL
    size = ATT_UNROLL // 2
    while size >= 2:
        @pl.when((i - done) >= size)
        def _():
            run(done, size)
        done = done + jnp.where((i - done) >= size, size, 0)
        size //= 2

    @pl.when(i - done >= 1)
    def _():
        run(done, 1)

    for parity in range(2):
        @pl.when(i % 2 == parity)
        def _():
            consume(parity, i, last=True)
            o_ref[...] = (acc_ref[...] / jnp.sum(l_ref[...], axis=1, keepdims=True)).astype(o_ref.dtype)


def _attention(q, k, v):
    heads, seq, _ = q.shape
    rows_p = v.shape[1]
    n_q = seq // ATT_Q
    return pl.pallas_call(
        _attn_kernel,
        grid=(heads, n_q),
        in_specs=[pl.BlockSpec((1, ATT_Q, QK_PAD), lambda h, i: (h, i, 0)),
                  pl.BlockSpec((1, QK_PAD, rows_p), lambda h, i: (h, 0, 0)),
                  pl.BlockSpec((1, rows_p, V_HEAD), lambda h, i: (h, 0, 0))],
        out_specs=pl.BlockSpec((ATT_Q, V_HEAD), lambda h, i: (i, h)),
        out_shape=jax.ShapeDtypeStruct((seq, ATT_WIDTH), BF16),
        scratch_shapes=[pltpu.VMEM((ATT_Q, LANES), F32),
                        pltpu.VMEM((ATT_Q, LANES), BF16),
                        pltpu.VMEM((2, ATT_Q, ATT_K), F32),
                        pltpu.VMEM((ATT_Q, ATT_K), BF16),
                        pltpu.VMEM((ATT_Q, LANES), F32),
                        pltpu.VMEM((ATT_Q, LANES), F32),
                        pltpu.VMEM((ATT_Q, V_HEAD), F32)],
        compiler_params=pltpu.CompilerParams(dimension_semantics=("arbitrary", "arbitrary"),
                                             vmem_limit_bytes=VMEM_LIMIT_BYTES),
        name="causal_attention",
    )(q, k, v)


def _split_dot(v, mat, terms):
    out = None
    rem = v
    for t in range(terms):
        piece = rem.astype(BF16)
        part = jnp.dot(piece, mat, preferred_element_type=F32)
        out = part if out is None else out + part
        if t + 1 < terms:
            rem = rem - piece.astype(F32)
    return out


def _ssd_kernel(xc_ref, dt_ref, dtb_ref, alog_ref, dskip_ref, tril_ref, expand_ref, y_ref, state_ref):
    i = pl.program_id(0)
    rows_b = xc_ref.shape[0]

    @pl.when(i == 0)
    def _():
        state_ref[...] = jnp.zeros_like(state_ref)

    xc = xc_ref

    lane = lax.broadcasted_iota(jnp.int32, (SSD_CHUNK, LANES), 1)
    row = lax.broadcasted_iota(jnp.int32, (SSD_CHUNK, LANES), 0)
    causal = lax.broadcasted_iota(jnp.int32, (SSD_CHUNK, SSD_CHUNK), 0) >= lax.broadcasted_iota(
        jnp.int32, (SSD_CHUNK, SSD_CHUNK), 1)
    low_half = lane < SSM_HEAD_DIM
    a_row = -jnp.exp(alog_ref[...])
    tril = tril_ref[...]
    heads_per_group = SSM_HEADS // SSM_GROUPS

    for c in range(rows_b // SSD_CHUNK):
        r0 = c * SSD_CHUNK
        valid = (lane < SSM_HEADS) & (row + (i * rows_b + r0) >= PAD - N_META)
        dt = jnp.where(valid, jax.nn.softplus(dt_ref[r0:r0 + SSD_CHUNK, :] + dtb_ref[...]), 0.0)
        da = dt * a_row
        acum = _cumsum_rows(tril, da)
        acum_t = acum.T
        dt_t = dt.T
        a_last = acum[SSD_CHUNK - 1:SSD_CHUNK, :]
        factors = jnp.concatenate([jnp.exp(acum), jnp.exp(a_last - acum) * dt,
                                   jnp.broadcast_to(jnp.exp(a_last), (SUBLANES, LANES))], axis=0)
        rows = slice(r0, r0 + SSD_CHUNK)
        for g in range(SSM_GROUPS):
            spread = _split_dot(factors, expand_ref[:, g * GROUP_WIDTH:(g + 1) * GROUP_WIDTH], 2)
            e_acum = spread[:SSD_CHUNK, :]
            w_end = spread[SSD_CHUNK:2 * SSD_CHUNK, :]
            c_decay = spread[2 * SSD_CHUNK:2 * SSD_CHUNK + 1, :]
            b_g = xc[rows, SSM_WIDTH + g * SSM_STATE:SSM_WIDTH + (g + 1) * SSM_STATE]
            c_g = xc[rows, SSM_WIDTH + (SSM_GROUPS + g) * SSM_STATE:
                     SSM_WIDTH + (SSM_GROUPS + g + 1) * SSM_STATE].astype(BF16)
            b_t = b_g.T.astype(BF16)
            cb = jnp.dot(c_g, b_t, preferred_element_type=F32)
            gcols = slice(g * GROUP_WIDTH, (g + 1) * GROUP_WIDTH)
            h_prev = state_ref[g]
            y_off = jnp.dot(c_g, h_prev.astype(BF16), preferred_element_type=F32)
            for jp in range(heads_per_group // 2):
                ws = []
                for r in (g * heads_per_group + 2 * jp, g * heads_per_group + 2 * jp + 1):
                    seg = acum[:, r:r + 1] - acum_t[r:r + 1, :]
                    decay = jnp.where(causal, jnp.exp(seg), 0.0)
                    ws.append((cb * decay * dt_t[r:r + 1, :]).astype(BF16))
                cols = slice(g * GROUP_WIDTH + jp * LANES, g * GROUP_WIDTH + (jp + 1) * LANES)
                x_pair = xc[rows, cols]
                x_cat = jnp.concatenate([jnp.where(low_half, x_pair, 0.0), jnp.where(low_half, 0.0, x_pair)],
                                        axis=0).astype(BF16)
                y_diag = jnp.dot(jnp.concatenate(ws, axis=1), x_cat, preferred_element_type=F32)
                pair = slice(jp * LANES, (jp + 1) * LANES)
                y_ref[rows, cols] = y_diag + y_off[:, pair] * e_acum[:, pair] + x_pair * dskip_ref[:, cols]
            x_end = (xc[rows, gcols] * w_end).astype(BF16)
            state_ref[g] = c_decay * h_prev + jnp.dot(b_t, x_end, preferred_element_type=F32)


def _cumsum_rows(tril, v):
    out = None
    rem = v
    for t in range(3):
        piece = rem.astype(BF16)
        part = jnp.dot(tril, piece, preferred_element_type=F32)
        out = part if out is None else out + part
        if t < 2:
            rem = rem - piece.astype(F32)
    return out


def _ssd_scan(xc, dt_raw, dt_bias, a_log, d_skip_x, tril, expand, seq):
    n_blocks = seq // ROW_BLOCK + 1
    consts = (dt_bias, a_log, d_skip_x, tril, expand)
    return pl.pallas_call(
        _ssd_kernel,
        grid=(n_blocks,),
        in_specs=[pl.BlockSpec((ROW_BLOCK, CONV_DIM), lambda i: (i, 0)),
                  pl.BlockSpec((ROW_BLOCK, LANES), lambda i: (i, 0))] + [_const_spec(c.shape) for c in consts],
        out_specs=pl.BlockSpec((ROW_BLOCK, SSM_WIDTH), lambda i: (jnp.maximum(i - 1, 0), 0)),
        out_shape=jax.ShapeDtypeStruct((seq, SSM_WIDTH), F32),
        scratch_shapes=[pltpu.VMEM((SSM_GROUPS, SSM_STATE, GROUP_WIDTH), F32)],
        compiler_params=pltpu.CompilerParams(dimension_semantics=("arbitrary",),
                                             vmem_limit_bytes=VMEM_LIMIT_BYTES),
        name="ssd_scan",
    )(xc, dt_raw, *consts)


def _out_mlp_kernel(att_ref, y_ref, z_ref, x_ref, g_ssm_ref, wout_ref, g_post_ref, g_mpre_ref,
                    wup_ref, wdown_ref, g_mpost_ref, o_ref):
    z = z_ref[...]
    gated = y_ref[...] * (z * jax.nn.sigmoid(z))
    parts = []
    for g in range(SSM_GROUPS):
        gg = gated[:, g * GROUP_WIDTH:(g + 1) * GROUP_WIDTH]
        parts.append(gg * lax.rsqrt(jnp.mean(jnp.square(gg), axis=-1, keepdims=True) + EPS))
    ssm = (jnp.concatenate(parts, axis=1) * g_ssm_ref[...]).astype(BF16)
    mix = (jnp.dot(att_ref[...], wout_ref[:ATT_WIDTH, :], preferred_element_type=F32)
           + jnp.dot(ssm, wout_ref[ATT_WIDTH:, :], preferred_element_type=F32))
    h1 = x_ref[...] + _rms(mix, g_post_ref[...])
    hn = _rms(h1, g_mpre_ref[...]).astype(BF16)
    f = None
    for c in range(D_FF // FF_CHUNK):
        u = jnp.dot(hn, wup_ref[:, c * FF_CHUNK:(c + 1) * FF_CHUNK], preferred_element_type=F32)
        u = jnp.square(jnp.maximum(u, 0.0)).astype(BF16)
        part = jnp.dot(u, wdown_ref[c * FF_CHUNK:(c + 1) * FF_CHUNK, :], preferred_element_type=F32)
        f = part if f is None else f + part
    o_ref[...] = h1 + _rms(f, g_mpost_ref[...])


def _out_mlp(att, y, z, x2d, g_ssm, wout, g_post, g_mpre, wup, wdown, g_mpost):
    seq = x2d.shape[0]
    weights = (g_ssm, wout, g_post, g_mpre, wup, wdown, g_mpost)
    row = lambda i: (i, 0)
    return pl.pallas_call(
        _out_mlp_kernel,
        grid=(seq // MLP_BLOCK,),
        in_specs=[pl.BlockSpec((MLP_BLOCK, ATT_WIDTH), row),
                  pl.BlockSpec((MLP_BLOCK, SSM_WIDTH), row),
                  pl.BlockSpec((MLP_BLOCK, SSM_WIDTH), row),
                  pl.BlockSpec((MLP_BLOCK, D_MODEL), row)] + [_const_spec(w.shape) for w in weights],
        out_specs=pl.BlockSpec((MLP_BLOCK, D_MODEL), row),
        out_shape=jax.ShapeDtypeStruct((seq, D_MODEL), F32),
        compiler_params=pltpu.CompilerParams(dimension_semantics=("arbitrary",),
                                             vmem_limit_bytes=VMEM_LIMIT_BYTES),
        name="out_proj_mlp",
    )(att, y, z, x2d, *weights)


def _rot_cols(w):
    half = QK_ROPE // 2
    return jnp.concatenate([-w[..., half:], w[..., :half]], axis=-1)


def _lane_pad(v, width=LANES):
    return jnp.pad(v, [(0, 0)] * (v.ndim - 1) + [(0, width - v.shape[-1])])


def kernel(x, meta_tokens, norm_mix_pre, w_in, q_a_norm, w_q_up, kv_a_norm, w_kv_up, conv_w, conv_b, dt_bias,
           a_log, d_skip, ssm_norm, w_out, norm_mix_post, norm_mlp_pre, w_mlp_up, w_mlp_down, norm_mlp_post):
    bsz, seq, _ = x.shape
    assert bsz == 1 and norm_mix_pre.shape[0] == 1, "single sequence, single layer"
    assert seq % ATT_Q == 0 and seq % ROW_BLOCK == 0 and seq % MLP_BLOCK == 0
    assert PAD == ROW_BLOCK, "the front pad is exactly the first block of the projection / SSD grids"
    x2d = x[0]
    row = lambda v: v.reshape(1, -1)

    o_ckv, o_kr, o_z, o_xbc, o_dt = Q_LORA, Q_LORA + KV_LORA, Q_LORA + KV_LORA + QK_ROPE, \
        Q_LORA + KV_LORA + QK_ROPE + SSM_WIDTH, Q_LORA + KV_LORA + QK_ROPE + SSM_WIDTH + CONV_DIM
    w_in0 = w_in.astype(BF16)[0]
    w_kr = w_in0[:, o_kr:o_z]
    wsm = jnp.concatenate([w_kr, _rot_cols(w_kr), _lane_pad(w_in0[:, o_dt:])], axis=1)
    wq3 = w_q_up[0].reshape(Q_LORA, ATT_HEADS, QK_DIM)
    wq = jnp.concatenate([wq3, _rot_cols(wq3[..., QK_NOPE:])], axis=-1).reshape(Q_LORA, ATT_HEADS * QK_PAD)
    wkv3 = w_kv_up[0].reshape(KV_LORA, ATT_HEADS, QK_NOPE + V_HEAD)
    wkt = wkv3[..., :QK_NOPE].reshape(KV_LORA, ATT_HEADS * QK_NOPE).T
    wv = wkv3[..., QK_NOPE:].reshape(KV_LORA, ATT_HEADS * V_HEAD)

    inv_freq = ROPE_THETA ** (-jnp.arange(0, QK_ROPE, 2, dtype=F32) / QK_ROPE)
    freq = jnp.tile(inv_freq, 4)
    low = jnp.arange(LANES) < QK_ROPE
    base = (jnp.arange(seq // ROW_BLOCK + 1) * ROW_BLOCK - (PAD - N_META)).astype(F32)
    ang_b = base[:, None] * freq[None, :]
    rope_blk = jnp.stack([jnp.cos(ang_b), jnp.where(low, -jnp.sin(ang_b), jnp.sin(ang_b))], axis=1)
    rope_blk = jnp.pad(rope_blk, ((0, 0), (0, SUBLANES - 2), (0, 0)))
    ang_l = jnp.arange(ROW_BLOCK, dtype=F32)[:, None] * freq[None, :]
    rope_loc = jnp.stack([jnp.where(low, jnp.cos(ang_l), jnp.sin(ang_l)),
                          jnp.where(low, jnp.sin(ang_l), jnp.cos(ang_l))])

    meta_pad = jnp.concatenate([jnp.zeros((PAD - N_META, D_MODEL), F32), meta_tokens.astype(F32)], axis=0)

    q, k, v, z, xc, dt_raw = _input_projection(
        x2d, meta_pad, rope_blk, rope_loc, row(norm_mix_pre[0]),
        w_in0[:, :o_ckv].astype(BF16), w_in0[:, o_ckv:o_kr].astype(BF16), w_in0[:, o_z:o_xbc].astype(BF16),
        w_in0[:, o_xbc:o_dt].astype(BF16), wsm.astype(BF16),
        row(q_a_norm[0]), row(kv_a_norm[0]), wq.astype(BF16), wkt.astype(BF16), wv.astype(BF16),
        conv_w[0], row(conv_b[0]))

    att = _attention(q, k, v)

    tril = jnp.tril(jnp.ones((SSD_CHUNK, SSD_CHUNK), BF16))
    expand = (jnp.arange(LANES)[:, None] == jnp.arange(SSM_WIDTH)[None, :] // SSM_HEAD_DIM).astype(BF16)
    y = _ssd_scan(xc, dt_raw, _lane_pad(row(dt_bias[0])), _lane_pad(row(a_log[0])),
                  row(jnp.repeat(d_skip[0], SSM_HEAD_DIM)), tril, expand, seq)

    out = _out_mlp(att, y, z, x2d, row(ssm_norm[0]), w_out.astype(BF16), row(norm_mix_post[0]),
                   row(norm_mlp_pre[0]), w_mlp_up.astype(BF16), w_mlp_down.astype(BF16),
                   row(norm_mlp_post[0]))
    return out[None]
```

```python
import math

import jax
import jax.numpy as jnp
from jax import lax
from jax.experimental import pallas as pl
from jax.experimental.pallas import tpu as pltpu

D_MODEL = 1024
N_META = 16
EPS = 1e-6
ATT_HEADS = 8
Q_LORA = 384
KV_LORA = 256
QK_NOPE = 128
QK_ROPE = 64
V_HEAD = 128
QK_DIM = QK_NOPE + QK_ROPE
ROPE_THETA = 10000.0
ATT_WIDTH = ATT_HEADS * V_HEAD
SSM_HEADS = 16
SSM_HEAD_DIM = 64
SSM_WIDTH = SSM_HEADS * SSM_HEAD_DIM
SSM_GROUPS = 2
SSM_STATE = 128
GROUP_WIDTH = SSM_WIDTH // SSM_GROUPS
CONV_K = 4
CONV_DIM = SSM_WIDTH + 2 * SSM_GROUPS * SSM_STATE
D_FF = 4 * D_MODEL

LANES = 128
SUBLANES = 8
VMEM_LIMIT_BYTES = 56 * 1024 * 1024

QK_PAD = 2 * LANES
PAD = 512
ROW_BLOCK = 512
MLP_BLOCK = 512
ATT_Q = 512
ATT_K = 512
ATT_UNROLL = 8
SSD_CHUNK = 128
FF_CHUNK = 1024
ATT_SCALE_LOG2 = (QK_DIM ** -0.5) * math.log2(math.e)

BF16 = jnp.bfloat16
F32 = jnp.float32


def _const_spec(shape):
    nd = len(shape)
    block = (None,) + tuple(shape[1:]) if nd == 3 and shape[0] == 1 else tuple(shape)
    return pl.BlockSpec(block, lambda *_: (0,) * nd, pipeline_mode=pl.Buffered(1))


def _rms(x, gain):
    return x * lax.rsqrt(jnp.mean(jnp.square(x), axis=-1, keepdims=True) + EPS) * gain


def _proj_kernel(meta_ref, x_ref, rope_blk_ref, rope_loc_ref, g_pre_ref, wqa_ref, wkva_ref, wz_ref, wxbc_ref,
                 wsm_ref, gq_ref, gkv_ref, wq_ref, wkt_ref, wv_ref, convw_ref, convb_ref,
                 q_ref, k_ref, v_ref, z_ref, xbc_ref, dt_ref, stage_ref):
    i = pl.program_id(0)
    rows_b = x_ref.shape[0]

    @pl.when(i == 0)
    def _():
        stage_ref[...] = jnp.zeros_like(stage_ref)

    u = stage_ref[...]
    u1 = pltpu.roll(u, 1, axis=0)
    older = pltpu.roll(convw_ref[1:2, :] * u + convw_ref[0:1, :] * u1, 2, axis=0)
    conv = (convw_ref[3:4, :] * u + convw_ref[2:3, :] * u1 + older + convb_ref[...])[SUBLANES:, :]
    xbc_ref[...] = conv * jax.nn.sigmoid(conv)
    stage_ref[0:SUBLANES, :] = stage_ref[rows_b:rows_b + SUBLANES, :]

    xin = jnp.where(i == 0, meta_ref[...], x_ref[...])
    hn = _rms(xin, g_pre_ref[...]).astype(BF16)

    blk = rope_blk_ref[0]
    cs = blk[0:1, :] * rope_loc_ref[0] + blk[1:2, :] * rope_loc_ref[1]
    low = lax.broadcasted_iota(jnp.int32, cs.shape, 1) < QK_ROPE

    def rope(t):
        prod = t * cs
        return prod + pltpu.roll(prod, QK_ROPE, axis=1)

    z_ref[...] = jnp.dot(hn, wz_ref[...], preferred_element_type=F32)
    stage_ref[SUBLANES:, :] = jnp.dot(hn, wxbc_ref[...], preferred_element_type=F32)

    small = jnp.dot(hn, wsm_ref[...], preferred_element_type=F32)
    dt_ref[...] = small[:, LANES:]
    k_rope_t = jnp.where(low, rope(small[:, :LANES]), 0.0).T.astype(BF16)

    c_q = jnp.dot(hn, wqa_ref[...], preferred_element_type=F32)
    cqn = _rms(c_q, gq_ref[...]).astype(BF16)
    c_kv = jnp.dot(hn, wkva_ref[...], preferred_element_type=F32)
    ckvn = _rms(c_kv, gkv_ref[...])
    ckvn = ckvn.astype(BF16)
    k_nope_t = lax.dot_general(wkt_ref[...], ckvn, (((1,), (1,)), ((), ())),
                               preferred_element_type=F32)
    v_all = jnp.dot(ckvn, wv_ref[...], preferred_element_type=F32)

    for h in range(ATT_HEADS):
        qh = jnp.dot(cqn, wq_ref[:, h * QK_PAD:(h + 1) * QK_PAD], preferred_element_type=F32)
        q_ref[h, :, :QK_NOPE] = (qh[:, :QK_NOPE] * ATT_SCALE_LOG2).astype(BF16)
        q_ref[h, :, QK_NOPE:] = (rope(qh[:, QK_NOPE:]) * ATT_SCALE_LOG2).astype(BF16)
        k_ref[h, :QK_NOPE, :] = k_nope_t[h * QK_NOPE:(h + 1) * QK_NOPE, :].astype(BF16)
        k_ref[h, QK_NOPE:, :] = k_rope_t
        v_ref[h] = v_all[:, h * V_HEAD:(h + 1) * V_HEAD].astype(BF16)


def _input_projection(x2d, meta_pad, rope_blk, rope_loc, g_pre, wqa, wkva, wz, wxbc, wsm, gq, gkv, wq, wkt, wv,
                      conv_w, conv_b):
    seq = x2d.shape[0]
    n_blocks = seq // ROW_BLOCK + 1
    rows_p = seq + PAD
    pblk = lambda i: jnp.minimum(i, n_blocks - 1)
    rblk = lambda i: jnp.maximum(pblk(i) - 1, 0)
    real = lambda i: (rblk(i), 0)
    real3 = lambda i: (0, rblk(i), 0)
    padded = lambda i: (pblk(i), 0)
    padded3 = lambda i: (0, pblk(i), 0)
    lagged = lambda i: (jnp.maximum(i - 1, 0), 0)
    weights = (g_pre, wqa, wkva, wz, wxbc, wsm, gq, gkv, wq, wkt, wv, conv_w, conv_b)
    return pl.pallas_call(
        _proj_kernel,
        grid=(n_blocks + 1,),
        in_specs=[_const_spec(meta_pad.shape),
                  pl.BlockSpec((ROW_BLOCK, D_MODEL), real),
                  pl.BlockSpec((1, SUBLANES, LANES), lambda i: (pblk(i), 0, 0)),
                  _const_spec(rope_loc.shape)] + [_const_spec(w.shape) for w in weights],
        out_specs=[pl.BlockSpec((ATT_HEADS, ROW_BLOCK, QK_PAD), real3),
                   pl.BlockSpec((ATT_HEADS, QK_PAD, ROW_BLOCK), lambda i: (0, 0, pblk(i))),
                   pl.BlockSpec((ATT_HEADS, ROW_BLOCK, V_HEAD), padded3),
                   pl.BlockSpec((ROW_BLOCK, SSM_WIDTH), real),
                   pl.BlockSpec((ROW_BLOCK, CONV_DIM), lagged),
                   pl.BlockSpec((ROW_BLOCK, LANES), padded)],
        out_shape=[jax.ShapeDtypeStruct((ATT_HEADS, seq, QK_PAD), BF16),
                   jax.ShapeDtypeStruct((ATT_HEADS, QK_PAD, rows_p), BF16),
                   jax.ShapeDtypeStruct((ATT_HEADS, rows_p, V_HEAD), BF16),
                   jax.ShapeDtypeStruct((seq, SSM_WIDTH), F32),
                   jax.ShapeDtypeStruct((rows_p, CONV_DIM), F32),
                   jax.ShapeDtypeStruct((rows_p, LANES), F32)],
        scratch_shapes=[pltpu.VMEM((SUBLANES + ROW_BLOCK, CONV_DIM), F32)],
        compiler_params=pltpu.CompilerParams(dimension_semantics=("arbitrary",),
                                             vmem_limit_bytes=VMEM_LIMIT_BYTES),
        name="input_projection",
    )(meta_pad, x2d, rope_blk, rope_loc, *weights)


def _attn_kernel(q_ref, k_ref, v_ref, o_ref, sm_ref, pm_ref, s_ref, p_ref, m_ref, l_ref, acc_ref):
    i = pl.program_id(1)
    n_tiles = ATT_K // LANES
    assert ATT_Q == ATT_K, "one diagonal chunk per query block"
    assert ATT_UNROLL % 2 == 0 and PAD >= LANES

    def chunk_start(j):
        return pl.multiple_of(PAD + j * ATT_K, math.gcd(PAD, ATT_K))

    def scores(buf, j):
        s_ref[buf] = jnp.dot(q_ref[0], k_ref[0, :, pl.ds(chunk_start(j), ATT_K)], preferred_element_type=F32)

    def score_tiles(buf, last):
        tiles = [s_ref[buf, :, t * LANES:(t + 1) * LANES] for t in range(n_tiles)]
        if last:
            rows = lax.broadcasted_iota(jnp.int32, (ATT_Q, LANES), 0)
            cols = lax.broadcasted_iota(jnp.int32, (ATT_Q, LANES), 1)
            tiles = [jnp.where(cols + t * LANES <= rows, tiles[t], -jnp.inf) for t in range(n_tiles)]
            tiles = [jnp.where(cols >= LANES - N_META, sm_ref[...], -jnp.inf)] + tiles
        return tiles

    def consume(buf, j, last=False):
        tiles = score_tiles(buf, last)
        mx = tiles[0]
        for t in range(1, len(tiles)):
            mx = jnp.maximum(mx, tiles[t])
        m_old = m_ref[...]
        m_new = jnp.maximum(m_old, jnp.max(mx, axis=1, keepdims=True))
        m_ref[...] = m_new
        alpha = jnp.exp2(m_old - m_new)
        tiles = score_tiles(buf, last)
        probs = [jnp.exp2(t - m_new) for t in tiles]
        lsum = probs[0]
        for p in probs[1:]:
            lsum = lsum + p
        l_ref[...] = alpha * l_ref[...] + lsum
        pv = None
        if last:
            pm_ref[...] = probs.pop(0).astype(BF16)
            pv = jnp.dot(pm_ref[...], v_ref[0, PAD - LANES:PAD, :], preferred_element_type=F32)
        for t, p in enumerate(probs):
            p_ref[:, t * LANES:(t + 1) * LANES] = p.astype(BF16)
        pv_chunk = jnp.dot(p_ref[...], v_ref[0, pl.ds(chunk_start(j), ATT_K), :], preferred_element_type=F32)
        acc_ref[...] = alpha * acc_ref[...] + (pv_chunk if pv is None else pv + pv_chunk)

    def run(first, count):
        for u in range(count):
            scores((u + 1) % 2, first + u + 1)
            consume(u % 2, first + u)

    m_ref[...] = jnp.full((ATT_Q, LANES), -jnp.inf, F32)
    l_ref[...] = jnp.zeros((ATT_Q, LANES), F32)
    acc_ref[...] = jnp.zeros((ATT_Q, V_HEAD), F32)
    sm_ref[...] = jnp.dot(q_ref[0], k_ref[0, :, PAD - LANES:PAD], preferred_element_type=F32)
    scores(0, 0)

    def trip(t, carry):
        run(ATT_UNROLL * t, ATT_UNROLL)
        return carry

    lax.fori_loop(0, i // ATT_UNROLL, trip, 0)

    done = i - i % ATT_UNROLL
    size = ATT_UNROLL // 2
    while size >= 2:
        @pl.when((i - done) >= size)
        def _():
            run(done, size)
        done = done + jnp.where((i - done) >= size, size, 0)
        size //= 2

    @pl.when(i - done >= 1)
    def _():
        run(done, 1)

    for parity in range(2):
        @pl.when(i % 2 == parity)
        def _():
            consume(parity, i, last=True)
            o_ref[...] = (acc_ref[...] / jnp.sum(l_ref[...], axis=1, keepdims=True)).astype(o_ref.dtype)


def _attention(q, k, v):
    heads, seq, _ = q.shape
    rows_p = v.shape[1]
    n_q = seq // ATT_Q
    return pl.pallas_call(
        _attn_kernel,
        grid=(heads, n_q),
        in_specs=[pl.BlockSpec((1, ATT_Q, QK_PAD), lambda h, i: (h, i, 0)),
                  pl.BlockSpec((1, QK_PAD, rows_p), lambda h, i: (h, 0, 0)),
                  pl.BlockSpec((1, rows_p, V_HEAD), lambda h, i: (h, 0, 0))],
        out_specs=pl.BlockSpec((ATT_Q, V_HEAD), lambda h, i: (i, h)),
        out_shape=jax.ShapeDtypeStruct((seq, ATT_WIDTH), BF16),
        scratch_shapes=[pltpu.VMEM((ATT_Q, LANES), F32),
                        pltpu.VMEM((ATT_Q, LANES), BF16),
                        pltpu.VMEM((2, ATT_Q, ATT_K), F32),
                        pltpu.VMEM((ATT_Q, ATT_K), BF16),
                        pltpu.VMEM((ATT_Q, LANES), F32),
                        pltpu.VMEM((ATT_Q, LANES), F32),
                        pltpu.VMEM((ATT_Q, V_HEAD), F32)],
        compiler_params=pltpu.CompilerParams(dimension_semantics=("arbitrary", "arbitrary"),
                                             vmem_limit_bytes=VMEM_LIMIT_BYTES),
        name="causal_attention",
    )(q, k, v)


def _split_dot(v, mat, terms):
    out = None
    rem = v
    for t in range(terms):
        piece = rem.astype(BF16)
        part = jnp.dot(piece, mat, preferred_element_type=F32)
        out = part if out is None else out + part
        if t + 1 < terms:
            rem = rem - piece.astype(F32)
    return out


def _ssd_kernel(xc_ref, dt_ref, dtb_ref, alog_ref, dskip_ref, tril_ref, expand_ref, y_ref, state_ref):
    i = pl.program_id(0)
    rows_b = xc_ref.shape[0]

    @pl.when(i == 0)
    def _():
        state_ref[...] = jnp.zeros_like(state_ref)

    xc = xc_ref

    lane = lax.broadcasted_iota(jnp.int32, (SSD_CHUNK, LANES), 1)
    row = lax.broadcasted_iota(jnp.int32, (SSD_CHUNK, LANES), 0)
    causal = lax.broadcasted_iota(jnp.int32, (SSD_CHUNK, SSD_CHUNK), 0) >= lax.broadcasted_iota(
        jnp.int32, (SSD_CHUNK, SSD_CHUNK), 1)
    low_half = lane < SSM_HEAD_DIM
    a_row = -jnp.exp(alog_ref[...])
    tril = tril_ref[...]
    heads_per_group = SSM_HEADS // SSM_GROUPS

    for c in range(rows_b // SSD_CHUNK):
        r0 = c * SSD_CHUNK
        valid = (lane < SSM_HEADS) & (row + (i * rows_b + r0) >= PAD - N_META)
        dt = jnp.where(valid, jax.nn.softplus(dt_ref[r0:r0 + SSD_CHUNK, :] + dtb_ref[...]), 0.0)
        da = dt * a_row
        acum = _cumsum_rows(tril, da)
        acum_t = acum.T
        dt_t = dt.T
        a_last = acum[SSD_CHUNK - 1:SSD_CHUNK, :]
        factors = jnp.concatenate([jnp.exp(acum), jnp.exp(a_last - acum) * dt,
                                   jnp.broadcast_to(jnp.exp(a_last), (SUBLANES, LANES))], axis=0)
        rows = slice(r0, r0 + SSD_CHUNK)
        for g in range(SSM_GROUPS):
            spread = _split_dot(factors, expand_ref[:, g * GROUP_WIDTH:(g + 1) * GROUP_WIDTH], 2)
            e_acum = spread[:SSD_CHUNK, :]
            w_end = spread[SSD_CHUNK:2 * SSD_CHUNK, :]
            c_decay = spread[2 * SSD_CHUNK:2 * SSD_CHUNK + 1, :]
            b_g = xc[rows, SSM_WIDTH + g * SSM_STATE:SSM_WIDTH + (g + 1) * SSM_STATE]
            c_g = xc[rows, SSM_WIDTH + (SSM_GROUPS + g) * SSM_STATE:
                     SSM_WIDTH + (SSM_GROUPS + g + 1) * SSM_STATE].astype(BF16)
            b_t = b_g.T.astype(BF16)
            cb = jnp.dot(c_g, b_t, preferred_element_type=F32)
            gcols = slice(g * GROUP_WIDTH, (g + 1) * GROUP_WIDTH)
            h_prev = state_ref[g]
            y_off = jnp.dot(c_g, h_prev.astype(BF16), preferred_element_type=F32)
            for jp in range(heads_per_group // 2):
                ws = []
                for r in (g * heads_per_group + 2 * jp, g * heads_per_group + 2 * jp + 1):
                    seg = acum[:, r:r + 1] - acum_t[r:r + 1, :]
                    decay = jnp.where(causal, jnp.exp(seg), 0.0)
                    ws.append((cb * decay * dt_t[r:r + 1, :]).astype(BF16))
                cols = slice(g * GROUP_WIDTH + jp * LANES, g * GROUP_WIDTH + (jp + 1) * LANES)
                x_pair = xc[rows, cols]
                x_cat = jnp.concatenate([jnp.where(low_half, x_pair, 0.0), jnp.where(low_half, 0.0, x_pair)],
                                        axis=0).astype(BF16)
                y_diag = jnp.dot(jnp.concatenate(ws, axis=1), x_cat, preferred_element_type=F32)
                pair = slice(jp * LANES, (jp + 1) * LANES)
                y_ref[rows, cols] = y_diag + y_off[:, pair] * e_acum[:, pair] + x_pair * dskip_ref[:, cols]
            x_end = (xc[rows, gcols] * w_end).astype(BF16)
            state_ref[g] = c_decay * h_prev + jnp.dot(b_t, x_end, preferred_element_type=F32)


def _cumsum_rows(tril, v):
    out = None
    rem = v
    for t in range(3):
        piece = rem.astype(BF16)
        part = jnp.dot(tril, piece, preferred_element_type=F32)
        out = part if out is None else out + part
        if t < 2:
            rem = rem - piece.astype(F32)
    return out


def _ssd_scan(xc, dt_raw, dt_bias, a_log, d_skip_x, tril, expand, seq):
    n_blocks = seq // ROW_BLOCK + 1
    consts = (dt_bias, a_log, d_skip_x, tril, expand)
    return pl.pallas_call(
        _ssd_kernel,
        grid=(n_blocks,),
        in_specs=[pl.BlockSpec((ROW_BLOCK, CONV_DIM), lambda i: (i, 0)),
                  pl.BlockSpec((ROW_BLOCK, LANES), lambda i: (i, 0))] + [_const_spec(c.shape) for c in consts],
        out_specs=pl.BlockSpec((ROW_BLOCK, SSM_WIDTH), lambda i: (jnp.maximum(i - 1, 0), 0)),
        out_shape=jax.ShapeDtypeStruct((seq, SSM_WIDTH), F32),
        scratch_shapes=[pltpu.VMEM((SSM_GROUPS, SSM_STATE, GROUP_WIDTH), F32)],
        compiler_params=pltpu.CompilerParams(dimension_semantics=("arbitrary",),
                                             vmem_limit_bytes=VMEM_LIMIT_BYTES),
        name="ssd_scan",
    )(xc, dt_raw, *consts)


def _out_mlp_kernel(att_ref, y_ref, z_ref, x_ref, g_ssm_ref, wout_ref, g_post_ref, g_mpre_ref,
                    wup_ref, wdown_ref, g_mpost_ref, o_ref):
    z = z_ref[...]
    gated = y_ref[...] * (z * jax.nn.sigmoid(z))
    parts = []
    for g in range(SSM_GROUPS):
        gg = gated[:, g * GROUP_WIDTH:(g + 1) * GROUP_WIDTH]
        parts.append(gg * lax.rsqrt(jnp.mean(jnp.square(gg), axis=-1, keepdims=True) + EPS))
    ssm = (jnp.concatenate(parts, axis=1) * g_ssm_ref[...]).astype(BF16)
    mix = (jnp.dot(att_ref[...], wout_ref[:ATT_WIDTH, :], preferred_element_type=F32)
           + jnp.dot(ssm, wout_ref[ATT_WIDTH:, :], preferred_element_type=F32))
    h1 = x_ref[...] + _rms(mix, g_post_ref[...])
    hn = _rms(h1, g_mpre_ref[...]).astype(BF16)
    f = None
    for c in range(D_FF // FF_CHUNK):
        u = jnp.dot(hn, wup_ref[:, c * FF_CHUNK:(c + 1) * FF_CHUNK], preferred_element_type=F32)
        u = jnp.square(jnp.maximum(u, 0.0)).astype(BF16)
        part = jnp.dot(u, wdown_ref[c * FF_CHUNK:(c + 1) * FF_CHUNK, :], preferred_element_type=F32)
        f = part if f is None else f + part
    o_ref[...] = h1 + _rms(f, g_mpost_ref[...])


def _out_mlp(att, y, z, x2d, g_ssm, wout, g_post, g_mpre, wup, wdown, g_mpost):
    seq = x2d.shape[0]
    weights = (g_ssm, wout, g_post, g_mpre, wup, wdown, g_mpost)
    row = lambda i: (i, 0)
    return pl.pallas_call(
        _out_mlp_kernel,
        grid=(seq // MLP_BLOCK,),
        in_specs=[pl.BlockSpec((MLP_BLOCK, ATT_WIDTH), row),
                  pl.BlockSpec((MLP_BLOCK, SSM_WIDTH), row),
                  pl.BlockSpec((MLP_BLOCK, SSM_WIDTH), row),
                  pl.BlockSpec((MLP_BLOCK, D_MODEL), row)] + [_const_spec(w.shape) for w in weights],
        out_specs=pl.BlockSpec((MLP_BLOCK, D_MODEL), row),
        out_shape=jax.ShapeDtypeStruct((seq, D_MODEL), F32),
        compiler_params=pltpu.CompilerParams(dimension_semantics=("arbitrary",),
                                             vmem_limit_bytes=VMEM_LIMIT_BYTES),
        name="out_proj_mlp",
    )(att, y, z, x2d, *weights)


def _rot_cols(w):
    half = QK_ROPE // 2
    return jnp.concatenate([-w[..., half:], w[..., :half]], axis=-1)


def _lane_pad(v, width=LANES):
    return jnp.pad(v, [(0, 0)] * (v.ndim - 1) + [(0, width - v.shape[-1])])


def kernel(x, meta_tokens, norm_mix_pre, w_in, q_a_norm, w_q_up, kv_a_norm, w_kv_up, conv_w, conv_b, dt_bias,
           a_log, d_skip, ssm_norm, w_out, norm_mix_post, norm_mlp_pre, w_mlp_up, w_mlp_down, norm_mlp_post):
    bsz, seq, _ = x.shape
    assert bsz == 1 and norm_mix_pre.shape[0] == 1, "single sequence, single layer"
    assert seq % ATT_Q == 0 and seq % ROW_BLOCK == 0 and seq % MLP_BLOCK == 0
    assert PAD == ROW_BLOCK, "the front pad is exactly the first block of the projection / SSD grids"
    x2d = x[0]
    row = lambda v: v.reshape(1, -1)

    o_ckv, o_kr, o_z, o_xbc, o_dt = Q_LORA, Q_LORA + KV_LORA, Q_LORA + KV_LORA + QK_ROPE, \
        Q_LORA + KV_LORA + QK_ROPE + SSM_WIDTH, Q_LORA + KV_LORA + QK_ROPE + SSM_WIDTH + CONV_DIM
    w_in0 = w_in.astype(BF16)[0]
    w_kr = w_in0[:, o_kr:o_z]
    wsm = jnp.concatenate([w_kr, _rot_cols(w_kr), _lane_pad(w_in0[:, o_dt:])], axis=1)
    wq3 = w_q_up[0].reshape(Q_LORA, ATT_HEADS, QK_DIM)
    wq = jnp.concatenate([wq3, _rot_cols(wq3[..., QK_NOPE:])], axis=-1).reshape(Q_LORA, ATT_HEADS * QK_PAD)
    wkv3 = w_kv_up[0].reshape(KV_LORA, ATT_HEADS, QK_NOPE + V_HEAD)
    wkt = wkv3[..., :QK_NOPE].reshape(KV_LORA, ATT_HEADS * QK_NOPE).T
    wv = wkv3[..., QK_NOPE:].reshape(KV_LORA, ATT_HEADS * V_HEAD)

    inv_freq = ROPE_THETA ** (-jnp.arange(0, QK_ROPE, 2, dtype=F32) / QK_ROPE)
    freq = jnp.tile(inv_freq, 4)
    low = jnp.arange(LANES) < QK_ROPE
    base = (jnp.arange(seq // ROW_BLOCK + 1) * ROW_BLOCK - (PAD - N_META)).astype(F32)
    ang_b = base[:, None] * freq[None, :]
    rope_blk = jnp.stack([jnp.cos(ang_b), jnp.where(low, -jnp.sin(ang_b), jnp.sin(ang_b))], axis=1)
    rope_blk = jnp.pad(rope_blk, ((0, 0), (0, SUBLANES - 2), (0, 0)))
    ang_l = jnp.arange(ROW_BLOCK, dtype=F32)[:, None] * freq[None, :]
    rope_loc = jnp.stack([jnp.where(low, jnp.cos(ang_l), jnp.sin(ang_l)),
                          jnp.where(low, jnp.sin(ang_l), jnp.cos(ang_l))])

    meta_pad = jnp.concatenate([jnp.zeros((PAD - N_META, D_MODEL), F32), meta_tokens.astype(F32)], axis=0)

    q, k, v, z, xc, dt_raw = _input_projection(
        x2d, meta_pad, rope_blk, rope_loc, row(norm_mix_pre[0]),
        w_in0[:, :o_ckv].astype(BF16), w_in0[:, o_ckv:o_kr].astype(BF16), w_in0[:, o_z:o_xbc].astype(BF16),
        w_in0[:, o_xbc:o_dt].astype(BF16), wsm.astype(BF16),
        row(q_a_norm[0]), row(kv_a_norm[0]), wq.astype(BF16), wkt.astype(BF16), wv.astype(BF16),
        conv_w[0], row(conv_b[0]))

    att = _attention(q, k, v)

    tril = jnp.tril(jnp.ones((SSD_CHUNK, SSD_CHUNK), BF16))
    expand = (jnp.arange(LANES)[:, None] == jnp.arange(SSM_WIDTH)[None, :] // SSM_HEAD_DIM).astype(BF16)
    y = _ssd_scan(xc, dt_raw, _lane_pad(row(dt_bias[0])), _lane_pad(row(a_log[0])),
                  row(jnp.repeat(d_skip[0], SSM_HEAD_DIM)), tril, expand, seq)

    out = _out_mlp(att, y, z, x2d, row(ssm_norm[0]), w_out.astype(BF16), row(norm_mix_post[0]),
                   row(norm_mlp_pre[0]), w_mlp_up.astype(BF16), w_mlp_down.astype(BF16),
                   row(norm_mlp_post[0]))
    return out[None]
```

```python
import math

import jax
import jax.numpy as jnp
from jax import lax
from jax.experimental import pallas as pl
from jax.experimental.pallas import tpu as pltpu

D_MODEL = 1024
N_META = 16
EPS = 1e-6
ATT_HEADS = 8
Q_LORA = 384
KV_LORA = 256
QK_NOPE = 128
QK_ROPE = 64
V_HEAD = 128
QK_DIM = QK_NOPE + QK_ROPE
ROPE_THETA = 10000.0
ATT_WIDTH = ATT_HEADS * V_HEAD
SSM_HEADS = 16
SSM_HEAD_DIM = 64
SSM_WIDTH = SSM_HEADS * SSM_HEAD_DIM
SSM_GROUPS = 2
SSM_STATE = 128
GROUP_WIDTH = SSM_WIDTH // SSM_GROUPS
CONV_K = 4
CONV_DIM = SSM_WIDTH + 2 * SSM_GROUPS * SSM_STATE
D_FF = 4 * D_MODEL

LANES = 128
SUBLANES = 8
VMEM_LIMIT_BYTES = 56 * 1024 * 1024

QK_PAD = 2 * LANES
PAD = 512
ROW_BLOCK = 512
MLP_BLOCK = 512
ATT_Q = 512
ATT_K = 512
ATT_UNROLL = 8
SSD_CHUNK = 128
FF_CHUNK = 1024
ATT_SCALE_LOG2 = (QK_DIM ** -0.5) * math.log2(math.e)

BF16 = jnp.bfloat16
F32 = jnp.float32


def _const_spec(shape):
    nd = len(shape)
    block = (None,) + tuple(shape[1:]) if nd == 3 and shape[0] == 1 else tuple(shape)
    return pl.BlockSpec(block, lambda *_: (0,) * nd, pipeline_mode=pl.Buffered(1))


def _rms(x, gain):
    return x * lax.rsqrt(jnp.mean(jnp.square(x), axis=-1, keepdims=True) + EPS) * gain


def _proj_kernel(meta_ref, x_ref, rope_blk_ref, rope_loc_ref, g_pre_ref, wqa_ref, wkva_ref, wz_ref, wxbc_ref,
                 wsm_ref, gq_ref, gkv_ref, wq_ref, wkt_ref, wv_ref, convw_ref, convb_ref,
                 q_ref, k_ref, v_ref, z_ref, xbc_ref, dt_ref, stage_ref):
    i = pl.program_id(0)
    rows_b = x_ref.shape[0]

    @pl.when(i == 0)
    def _():
        stage_ref[...] = jnp.zeros_like(stage_ref)

    u = stage_ref[...]
    u1 = pltpu.roll(u, 1, axis=0)
    older = pltpu.roll(convw_ref[1:2, :] * u + convw_ref[0:1, :] * u1, 2, axis=0)
    conv = (convw_ref[3:4, :] * u + convw_ref[2:3, :] * u1 + older + convb_ref[...])[SUBLANES:, :]
    xbc_ref[...] = conv * jax.nn.sigmoid(conv)
    stage_ref[0:SUBLANES, :] = stage_ref[rows_b:rows_b + SUBLANES, :]

    xin = jnp.where(i == 0, meta_ref[...], x_ref[...])
    hn = _rms(xin, g_pre_ref[...]).astype(BF16)

    blk = rope_blk_ref[0]
    cs = blk[0:1, :] * rope_loc_ref[0] + blk[1:2, :] * rope_loc_ref[1]
    low = lax.broadcasted_iota(jnp.int32, cs.shape, 1) < QK_ROPE

    def rope(t):
        prod = t * cs
        return prod + pltpu.roll(prod, QK_ROPE, axis=1)

    z_ref[...] = jnp.dot(hn, wz_ref[...], preferred_element_type=F32)
    stage_ref[SUBLANES:, :] = jnp.dot(hn, wxbc_ref[...], preferred_element_type=F32)

    small = jnp.dot(hn, wsm_ref[...], preferred_element_type=F32)
    dt_ref[...] = small[:, LANES:]
    k_rope_t = jnp.where(low, rope(small[:, :LANES]), 0.0).T.astype(BF16)

    c_q = jnp.dot(hn, wqa_ref[...], preferred_element_type=F32)
    cqn = _rms(c_q, gq_ref[...]).astype(BF16)
    c_kv = jnp.dot(hn, wkva_ref[...], preferred_element_type=F32)
    ckvn = _rms(c_kv, gkv_ref[...])
    ckvn = ckvn.astype(BF16)
    k_nope_t = lax.dot_general(wkt_ref[...], ckvn, (((1,), (1,)), ((), ())),
                               preferred_element_type=F32)
    v_all = jnp.dot(ckvn, wv_ref[...], preferred_element_type=F32)

    for h in range(ATT_HEADS):
        qh = jnp.dot(cqn, wq_ref[:, h * QK_PAD:(h + 1) * QK_PAD], preferred_element_type=F32)
        q_ref[h, :, :QK_NOPE] = (qh[:, :QK_NOPE] * ATT_SCALE_LOG2).astype(BF16)
        q_ref[h, :, QK_NOPE:] = (rope(qh[:, QK_NOPE:]) * ATT_SCALE_LOG2).astype(BF16)
        k_ref[h, :QK_NOPE, :] = k_nope_t[h * QK_NOPE:(h + 1) * QK_NOPE, :].astype(BF16)
        k_ref[h, QK_NOPE:, :] = k_rope_t
        v_ref[h] = v_all[:, h * V_HEAD:(h + 1) * V_HEAD].astype(BF16)


def _input_projection(x2d, meta_pad, rope_blk, rope_loc, g_pre, wqa, wkva, wz, wxbc, wsm, gq, gkv, wq, wkt, wv,
                      conv_w, conv_b):
    seq = x2d.shape[0]
    n_blocks = seq // ROW_BLOCK + 1
    rows_p = seq + PAD
    pblk = lambda i: jnp.minimum(i, n_blocks - 1)
    rblk = lambda i: jnp.maximum(pblk(i) - 1, 0)
    real = lambda i: (rblk(i), 0)
    real3 = lambda i: (0, rblk(i), 0)
    padded = lambda i: (pblk(i), 0)
    padded3 = lambda i: (0, pblk(i), 0)
    lagged = lambda i: (jnp.maximum(i - 1, 0), 0)
    weights = (g_pre, wqa, wkva, wz, wxbc, wsm, gq, gkv, wq, wkt, wv, conv_w, conv_b)
    return pl.pallas_call(
        _proj_kernel,
        grid=(n_blocks + 1,),
        in_specs=[_const_spec(meta_pad.shape),
                  pl.BlockSpec((ROW_BLOCK, D_MODEL), real),
                  pl.BlockSpec((1, SUBLANES, LANES), lambda i: (pblk(i), 0, 0)),
                  _const_spec(rope_loc.shape)] + [_const_spec(w.shape) for w in weights],
        out_specs=[pl.BlockSpec((ATT_HEADS, ROW_BLOCK, QK_PAD), real3),
                   pl.BlockSpec((ATT_HEADS, QK_PAD, ROW_BLOCK), lambda i: (0, 0, pblk(i))),
                   pl.BlockSpec((ATT_HEADS, ROW_BLOCK, V_HEAD), padded3),
                   pl.BlockSpec((ROW_BLOCK, SSM_WIDTH), real),
                   pl.BlockSpec((ROW_BLOCK, CONV_DIM), lagged),
                   pl.BlockSpec((ROW_BLOCK, LANES), padded)],
        out_shape=[jax.ShapeDtypeStruct((ATT_HEADS, seq, QK_PAD), BF16),
                   jax.ShapeDtypeStruct((ATT_HEADS, QK_PAD, rows_p), BF16),
                   jax.ShapeDtypeStruct((ATT_HEADS, rows_p, V_HEAD), BF16),
                   jax.ShapeDtypeStruct((seq, SSM_WIDTH), F32),
                   jax.ShapeDtypeStruct((rows_p, CONV_DIM), F32),
                   jax.ShapeDtypeStruct((rows_p, LANES), F32)],
        scratch_shapes=[pltpu.VMEM((SUBLANES + ROW_BLOCK, CONV_DIM), F32)],
        compiler_params=pltpu.CompilerParams(dimension_semantics=("arbitrary",),
                                             vmem_limit_bytes=VMEM_LIMIT_BYTES),
        name="input_projection",
    )(meta_pad, x2d, rope_blk, rope_loc, *weights)


def _attn_kernel(q_ref, k_ref, v_ref, o_ref, sm_ref, pm_ref, s_ref, p_ref, m_ref, l_ref, acc_ref):
    i = pl.program_id(1)
    n_tiles = ATT_K // LANES
    assert ATT_Q == ATT_K, "one diagonal chunk per query block"
    assert ATT_UNROLL % 2 == 0 and PAD >= LANES

    def chunk_start(j):
        return pl.multiple_of(PAD + j * ATT_K, math.gcd(PAD, ATT_K))

    def scores(buf, j):
        s_ref[buf] = jnp.dot(q_ref[0], k_ref[0, :, pl.ds(chunk_start(j), ATT_K)], preferred_element_type=F32)

    def score_tiles(buf, last):
        tiles = [s_ref[buf, :, t * LANES:(t + 1) * LANES] for t in range(n_tiles)]
        if last:
            rows = lax.broadcasted_iota(jnp.int32, (ATT_Q, LANES), 0)
            cols = lax.broadcasted_iota(jnp.int32, (ATT_Q, LANES), 1)
            tiles = [jnp.where(cols + t * LANES <= rows, tiles[t], -jnp.inf) for t in range(n_tiles)]
            tiles = [jnp.where(cols >= LANES - N_META, sm_ref[...], -jnp.inf)] + tiles
        return tiles

    def consume(buf, j, last=False):
        tiles = score_tiles(buf, last)
        mx = tiles[0]
        for t in range(1, len(tiles)):
            mx = jnp.maximum(mx, tiles[t])
        m_old = m_ref[...]
        m_new = jnp.maximum(m_old, jnp.max(mx, axis=1, keepdims=True))
        m_ref[...] = m_new
        alpha = jnp.exp2(m_old - m_new)
        tiles = score_tiles(buf, last)
        probs = [jnp.exp2(t - m_new) for t in tiles]
        lsum = probs[0]
        for p in probs[1:]:
            lsum = lsum + p
        l_ref[...] = alpha * l_ref[...] + lsum
        pv = None
        if last:
            pm_ref[...] = probs.pop(0).astype(BF16)
            pv = jnp.dot(pm_ref[...], v_ref[0, PAD - LANES:PAD, :], preferred_element_type=F32)
        for t, p in enumerate(probs):
            p_ref[:, t * LANES:(t + 1) * LANES] = p.astype(BF16)
        pv_chunk = jnp.dot(p_ref[...], v_ref[0, pl.ds(chunk_start(j), ATT_K), :], preferred_element_type=F32)
        acc_ref[...] = alpha * acc_ref[...] + (pv_chunk if pv is None else pv + pv_chunk)

    def run(first, count):
        for u in range(count):
            scores((u + 1) % 2, first + u + 1)
            consume(u % 2, first + u)

    m_ref[...] = jnp.full((ATT_Q, LANES), -jnp.inf, F32)
    l_ref[...] = jnp.zeros((ATT_Q, LANES), F32)
    acc_ref[...] = jnp.zeros((ATT_Q, V_HEAD), F32)
    sm_ref[...] = jnp.dot(q_ref[0], k_ref[0, :, PAD - LANES:PAD], preferred_element_type=F32)
    scores(0, 0)

    def trip(t, carry):
        run(ATT_UNROLL * t, ATT_UNROLL)
        return carry

    lax.fori_loop(0, i // ATT_UNROLL, trip, 0)

    done = i - i % ATT_UNROLL
    for rem in range(1, ATT_UNROLL):
        @pl.when(i - done == rem)
        def _():
            run(done, rem)

    for parity in range(2):
        @pl.when(i % 2 == parity)
        def _():
            consume(parity, i, last=True)
            o_ref[...] = (acc_ref[...] / jnp.sum(l_ref[...], axis=1, keepdims=True)).astype(o_ref.dtype)


def _attention(q, k, v):
    heads, seq, _ = q.shape
    rows_p = v.shape[1]
    n_q = seq // ATT_Q
    return pl.pallas_call(
        _attn_kernel,
        grid=(heads, n_q),
        in_specs=[pl.BlockSpec((1, ATT_Q, QK_PAD), lambda h, i: (h, i, 0)),
                  pl.BlockSpec((1, QK_PAD, rows_p), lambda h, i: (h, 0, 0)),
                  pl.BlockSpec((1, rows_p, V_HEAD), lambda h, i: (h, 0, 0))],
        out_specs=pl.BlockSpec((ATT_Q, V_HEAD), lambda h, i: (i, h)),
        out_shape=jax.ShapeDtypeStruct((seq, ATT_WIDTH), BF16),
        scratch_shapes=[pltpu.VMEM((ATT_Q, LANES), F32),
                        pltpu.VMEM((ATT_Q, LANES), BF16),
                        pltpu.VMEM((2, ATT_Q, ATT_K), F32),
                        pltpu.VMEM((ATT_Q, ATT_K), BF16),
                        pltpu.VMEM((ATT_Q, LANES), F32),
                        pltpu.VMEM((ATT_Q, LANES), F32),
                        pltpu.VMEM((ATT_Q, V_HEAD), F32)],
        compiler_params=pltpu.CompilerParams(dimension_semantics=("arbitrary", "arbitrary"),
                                             vmem_limit_bytes=VMEM_LIMIT_BYTES),
        name="causal_attention",
    )(q, k, v)


def _split_dot(v, mat, terms):
    out = None
    rem = v
    for t in range(terms):
        piece = rem.astype(BF16)
        part = jnp.dot(piece, mat, preferred_element_type=F32)
        out = part if out is None else out + part
        if t + 1 < terms:
            rem = rem - piece.astype(F32)
    return out


def _ssd_kernel(xc_ref, dt_ref, dtb_ref, alog_ref, dskip_ref, tril_ref, expand_ref, y_ref, state_ref):
    i = pl.program_id(0)
    rows_b = xc_ref.shape[0]

    @pl.when(i == 0)
    def _():
        state_ref[...] = jnp.zeros_like(state_ref)

    xc = xc_ref

    lane = lax.broadcasted_iota(jnp.int32, (SSD_CHUNK, LANES), 1)
    row = lax.broadcasted_iota(jnp.int32, (SSD_CHUNK, LANES), 0)
    causal = lax.broadcasted_iota(jnp.int32, (SSD_CHUNK, SSD_CHUNK), 0) >= lax.broadcasted_iota(
        jnp.int32, (SSD_CHUNK, SSD_CHUNK), 1)
    low_half = lane < SSM_HEAD_DIM
    a_row = -jnp.exp(alog_ref[...])
    tril = tril_ref[...]
    heads_per_group = SSM_HEADS // SSM_GROUPS

    for c in range(rows_b // SSD_CHUNK):
        r0 = c * SSD_CHUNK
        valid = (lane < SSM_HEADS) & (row + (i * rows_b + r0) >= PAD - N_META)
        dt = jnp.where(valid, jax.nn.softplus(dt_ref[r0:r0 + SSD_CHUNK, :] + dtb_ref[...]), 0.0)
        da = dt * a_row
        acum = _cumsum_rows(tril, da)
        acum_t = acum.T
        dt_t = dt.T
        a_last = acum[SSD_CHUNK - 1:SSD_CHUNK, :]
        factors = jnp.concatenate([jnp.exp(acum), jnp.exp(a_last - acum) * dt,
                                   jnp.broadcast_to(jnp.exp(a_last), (SUBLANES, LANES))], axis=0)
        rows = slice(r0, r0 + SSD_CHUNK)
        for g in range(SSM_GROUPS):
            spread = _split_dot(factors, expand_ref[:, g * GROUP_WIDTH:(g + 1) * GROUP_WIDTH], 2)
            e_acum = spread[:SSD_CHUNK, :]
            w_end = spread[SSD_CHUNK:2 * SSD_CHUNK, :]
            c_decay = spread[2 * SSD_CHUNK:2 * SSD_CHUNK + 1, :]
            b_g = xc[rows, SSM_WIDTH + g * SSM_STATE:SSM_WIDTH + (g + 1) * SSM_STATE]
            c_g = xc[rows, SSM_WIDTH + (SSM_GROUPS + g) * SSM_STATE:
                     SSM_WIDTH + (SSM_GROUPS + g + 1) * SSM_STATE].astype(BF16)
            b_t = b_g.T.astype(BF16)
            cb = jnp.dot(c_g, b_t, preferred_element_type=F32)
            gcols = slice(g * GROUP_WIDTH, (g + 1) * GROUP_WIDTH)
            h_prev = state_ref[g]
            y_off = jnp.dot(c_g, h_prev.astype(BF16), preferred_element_type=F32)
            for jp in range(heads_per_group // 2):
                ws = []
                for r in (g * heads_per_group + 2 * jp, g * heads_per_group + 2 * jp + 1):
                    seg = acum[:, r:r + 1] - acum_t[r:r + 1, :]
                    decay = jnp.where(causal, jnp.exp(seg), 0.0)
                    ws.append((cb * decay * dt_t[r:r + 1, :]).astype(BF16))
                cols = slice(g * GROUP_WIDTH + jp * LANES, g * GROUP_WIDTH + (jp + 1) * LANES)
                x_pair = xc[rows, cols]
                x_cat = jnp.concatenate([jnp.where(low_half, x_pair, 0.0), jnp.where(low_half, 0.0, x_pair)],
                                        axis=0).astype(BF16)
                y_diag = jnp.dot(jnp.concatenate(ws, axis=1), x_cat, preferred_element_type=F32)
                pair = slice(jp * LANES, (jp + 1) * LANES)
                y_ref[rows, cols] = y_diag + y_off[:, pair] * e_acum[:, pair] + x_pair * dskip_ref[:, cols]
            x_end = (xc[rows, gcols] * w_end).astype(BF16)
            state_ref[g] = c_decay * h_prev + jnp.dot(b_t, x_end, preferred_element_type=F32)


def _cumsum_rows(tril, v):
    out = None
    rem = v
    for t in range(3):
        piece = rem.astype(BF16)
        part = jnp.dot(tril, piece, preferred_element_type=F32)
        out = part if out is None else out + part
        if t < 2:
            rem = rem - piece.astype(F32)
    return out


def _ssd_scan(xc, dt_raw, dt_bias, a_log, d_skip_x, tril, expand, seq):
    n_blocks = seq // ROW_BLOCK + 1
    consts = (dt_bias, a_log, d_skip_x, tril, expand)
    return pl.pallas_call(
        _ssd_kernel,
        grid=(n_blocks,),
        in_specs=[pl.BlockSpec((ROW_BLOCK, CONV_DIM), lambda i: (i, 0)),
                  pl.BlockSpec((ROW_BLOCK, LANES), lambda i: (i, 0))] + [_const_spec(c.shape) for c in consts],
        out_specs=pl.BlockSpec((ROW_BLOCK, SSM_WIDTH), lambda i: (jnp.maximum(i - 1, 0), 0)),
        out_shape=jax.ShapeDtypeStruct((seq, SSM_WIDTH), F32),
        scratch_shapes=[pltpu.VMEM((SSM_GROUPS, SSM_STATE, GROUP_WIDTH), F32)],
        compiler_params=pltpu.CompilerParams(dimension_semantics=("arbitrary",),
                                             vmem_limit_bytes=VMEM_LIMIT_BYTES),
        name="ssd_scan",
    )(xc, dt_raw, *consts)


def _out_mlp_kernel(att_ref, y_ref, z_ref, x_ref, g_ssm_ref, wout_ref, g_post_ref, g_mpre_ref,
                    wup_ref, wdown_ref, g_mpost_ref, o_ref):
    z = z_ref[...]
    gated = y_ref[...] * (z * jax.nn.sigmoid(z))
    parts = []
    for g in range(SSM_GROUPS):
        gg = gated[:, g * GROUP_WIDTH:(g + 1) * GROUP_WIDTH]
        parts.append(gg * lax.rsqrt(jnp.mean(jnp.square(gg), axis=-1, keepdims=True) + EPS))
    ssm = (jnp.concatenate(parts, axis=1) * g_ssm_ref[...]).astype(BF16)
    mix = (jnp.dot(att_ref[...], wout_ref[:ATT_WIDTH, :], preferred_element_type=F32)
           + jnp.dot(ssm, wout_ref[ATT_WIDTH:, :], preferred_element_type=F32))
    h1 = x_ref[...] + _rms(mix, g_post_ref[...])
    hn = _rms(h1, g_mpre_ref[...]).astype(BF16)
    f = None
    for c in range(D_FF // FF_CHUNK):
        u = jnp.dot(hn, wup_ref[:, c * FF_CHUNK:(c + 1) * FF_CHUNK], preferred_element_type=F32)
        u = jnp.square(jnp.maximum(u, 0.0)).astype(BF16)
        part = jnp.dot(u, wdown_ref[c * FF_CHUNK:(c + 1) * FF_CHUNK, :], preferred_element_type=F32)
        f = part if f is None else f + part
    o_ref[...] = h1 + _rms(f, g_mpost_ref[...])


def _out_mlp(att, y, z, x2d, g_ssm, wout, g_post, g_mpre, wup, wdown, g_mpost):
    seq = x2d.shape[0]
    weights = (g_ssm, wout, g_post, g_mpre, wup, wdown, g_mpost)
    row = lambda i: (i, 0)
    return pl.pallas_call(
        _out_mlp_kernel,
        grid=(seq // MLP_BLOCK,),
        in_specs=[pl.BlockSpec((MLP_BLOCK, ATT_WIDTH), row),
                  pl.BlockSpec((MLP_BLOCK, SSM_WIDTH), row),
                  pl.BlockSpec((MLP_BLOCK, SSM_WIDTH), row),
                  pl.BlockSpec((MLP_BLOCK, D_MODEL), row)] + [_const_spec(w.shape) for w in weights],
        out_specs=pl.BlockSpec((MLP_BLOCK, D_MODEL), row),
        out_shape=jax.ShapeDtypeStruct((seq, D_MODEL), F32),
        compiler_params=pltpu.CompilerParams(dimension_semantics=("arbitrary",),
                                             vmem_limit_bytes=VMEM_LIMIT_BYTES),
        name="out_proj_mlp",
    )(att, y, z, x2d, *weights)


def _rot_cols(w):
    half = QK_ROPE // 2
    return jnp.concatenate([-w[..., half:], w[..., :half]], axis=-1)


def _lane_pad(v, width=LANES):
    return jnp.pad(v, [(0, 0)] * (v.ndim - 1) + [(0, width - v.shape[-1])])


def kernel(x, meta_tokens, norm_mix_pre, w_in, q_a_norm, w_q_up, kv_a_norm, w_kv_up, conv_w, conv_b, dt_bias,
           a_log, d_skip, ssm_norm, w_out, norm_mix_post, norm_mlp_pre, w_mlp_up, w_mlp_down, norm_mlp_post):
    bsz, seq, _ = x.shape
    assert bsz == 1 and norm_mix_pre.shape[0] == 1, "single sequence, single layer"
    assert seq % ATT_Q == 0 and seq % ROW_BLOCK == 0 and seq % MLP_BLOCK == 0
    assert PAD == ROW_BLOCK, "the front pad is exactly the first block of the projection / SSD grids"
    x2d = x[0]
    row = lambda v: v.reshape(1, -1)

    o_ckv, o_kr, o_z, o_xbc, o_dt = Q_LORA, Q_LORA + KV_LORA, Q_LORA + KV_LORA + QK_ROPE, \
        Q_LORA + KV_LORA + QK_ROPE + SSM_WIDTH, Q_LORA + KV_LORA + QK_ROPE + SSM_WIDTH + CONV_DIM
    w_in0 = w_in.astype(BF16)[0]
    w_kr = w_in0[:, o_kr:o_z]
    wsm = jnp.concatenate([w_kr, _rot_cols(w_kr), _lane_pad(w_in0[:, o_dt:])], axis=1)
    wq3 = w_q_up[0].reshape(Q_LORA, ATT_HEADS, QK_DIM)
    wq = jnp.concatenate([wq3, _rot_cols(wq3[..., QK_NOPE:])], axis=-1).reshape(Q_LORA, ATT_HEADS * QK_PAD)
    wkv3 = w_kv_up[0].reshape(KV_LORA, ATT_HEADS, QK_NOPE + V_HEAD)
    wkt = wkv3[..., :QK_NOPE].reshape(KV_LORA, ATT_HEADS * QK_NOPE).T
    wv = wkv3[..., QK_NOPE:].reshape(KV_LORA, ATT_HEADS * V_HEAD)

    inv_freq = ROPE_THETA ** (-jnp.arange(0, QK_ROPE, 2, dtype=F32) / QK_ROPE)
    freq = jnp.tile(inv_freq, 4)
    low = jnp.arange(LANES) < QK_ROPE
    base = (jnp.arange(seq // ROW_BLOCK + 1) * ROW_BLOCK - (PAD - N_META)).astype(F32)
    ang_b = base[:, None] * freq[None, :]
    rope_blk = jnp.stack([jnp.cos(ang_b), jnp.where(low, -jnp.sin(ang_b), jnp.sin(ang_b))], axis=1)
    rope_blk = jnp.pad(rope_blk, ((0, 0), (0, SUBLANES - 2), (0, 0)))
    ang_l = jnp.arange(ROW_BLOCK, dtype=F32)[:, None] * freq[None, :]
    rope_loc = jnp.stack([jnp.where(low, jnp.cos(ang_l), jnp.sin(ang_l)),
                          jnp.where(low, jnp.sin(ang_l), jnp.cos(ang_l))])

    meta_pad = jnp.concatenate([jnp.zeros((PAD - N_META, D_MODEL), F32), meta_tokens.astype(F32)], axis=0)

    q, k, v, z, xc, dt_raw = _input_projection(
        x2d, meta_pad, rope_blk, rope_loc, row(norm_mix_pre[0]),
        w_in0[:, :o_ckv].astype(BF16), w_in0[:, o_ckv:o_kr].astype(BF16), w_in0[:, o_z:o_xbc].astype(BF16),
        w_in0[:, o_xbc:o_dt].astype(BF16), wsm.astype(BF16),
        row(q_a_norm[0]), row(kv_a_norm[0]), wq.astype(BF16), wkt.astype(BF16), wv.astype(BF16),
        conv_w[0], row(conv_b[0]))

    att = _attention(q, k, v)

    tril = jnp.tril(jnp.ones((SSD_CHUNK, SSD_CHUNK), BF16))
    expand = (jnp.arange(LANES)[:, None] == jnp.arange(SSM_WIDTH)[None, :] // SSM_HEAD_DIM).astype(BF16)
    y = _ssd_scan(xc, dt_raw, _lane_pad(row(dt_bias[0])), _lane_pad(row(a_log[0])),
                  row(jnp.repeat(d_skip[0], SSM_HEAD_DIM)), tril, expand, seq)

    out = _out_mlp(att, y, z, x2d, row(ssm_norm[0]), w_out.astype(BF16), row(norm_mix_post[0]),
                   row(norm_mlp_pre[0]), w_mlp_up.astype(BF16), w_mlp_down.astype(BF16),
                   row(norm_mlp_post[0]))
    return out[None]
```

```python
import math

import jax
import jax.numpy as jnp
from jax import lax
from jax.experimental import pallas as pl
from jax.experimental.pallas import tpu as pltpu

D_MODEL = 1024
N_META = 16
EPS = 1e-6
ATT_HEADS = 8
Q_LORA = 384
KV_LORA = 256
QK_NOPE = 128
QK_ROPE = 64
V_HEAD = 128
QK_DIM = QK_NOPE + QK_ROPE
ROPE_THETA = 10000.0
ATT_WIDTH = ATT_HEADS * V_HEAD
SSM_HEADS = 16
SSM_HEAD_DIM = 64
SSM_WIDTH = SSM_HEADS * SSM_HEAD_DIM
SSM_GROUPS = 2
SSM_STATE = 128
GROUP_WIDTH = SSM_WIDTH // SSM_GROUPS
CONV_K = 4
CONV_DIM = SSM_WIDTH + 2 * SSM_GROUPS * SSM_STATE
D_FF = 4 * D_MODEL

LANES = 128
SUBLANES = 8
MIB = 1024 * 1024
VMEM_PROJ = 52 * MIB
VMEM_ATTN = 44 * MIB
VMEM_SSD = 32 * MIB
VMEM_MLP = 52 * MIB

QK_PAD = 2 * LANES
PAD = 512
ROW_BLOCK = 512
MLP_BLOCK = 512
ATT_Q = 512
ATT_K = 512
ATT_UNROLL = 8
SSD_CHUNK = 128
FF_CHUNK = 1024
ATT_SCALE_LOG2 = (QK_DIM ** -0.5) * math.log2(math.e)

BF16 = jnp.bfloat16
F32 = jnp.float32


def _const_spec(shape):
    nd = len(shape)
    block = (None,) + tuple(shape[1:]) if nd == 3 and shape[0] == 1 else tuple(shape)
    return pl.BlockSpec(block, lambda *_: (0,) * nd, pipeline_mode=pl.Buffered(1))


def _rms(x, gain):
    return x * lax.rsqrt(jnp.mean(jnp.square(x), axis=-1, keepdims=True) + EPS) * gain


def _proj_kernel(meta_ref, x_ref, rope_blk_ref, rope_loc_ref, g_pre_ref, wqa_ref, wkva_ref, wz_ref, wxbc_ref,
                 wsm_ref, gq_ref, gkv_ref, wq_ref, wkt_ref, wv_ref, convw_ref, convb_ref,
                 q_ref, k_ref, v_ref, z_ref, xbc_ref, dt_ref, stage_ref):
    i = pl.program_id(0)
    rows_b = x_ref.shape[0]

    @pl.when(i == 0)
    def _():
        stage_ref[...] = jnp.zeros_like(stage_ref)

    u = stage_ref[...]
    u1 = pltpu.roll(u, 1, axis=0)
    older = pltpu.roll(convw_ref[1:2, :] * u + convw_ref[0:1, :] * u1, 2, axis=0)
    conv = (convw_ref[3:4, :] * u + convw_ref[2:3, :] * u1 + older + convb_ref[...])[SUBLANES:, :]
    xbc_ref[...] = conv * jax.nn.sigmoid(conv)
    stage_ref[0:SUBLANES, :] = stage_ref[rows_b:rows_b + SUBLANES, :]

    xin = jnp.where(i == 0, meta_ref[...], x_ref[...])
    hn = _rms(xin, g_pre_ref[...]).astype(BF16)

    blk = rope_blk_ref[0]
    cs = blk[0:1, :] * rope_loc_ref[0] + blk[1:2, :] * rope_loc_ref[1]
    low = lax.broadcasted_iota(jnp.int32, cs.shape, 1) < QK_ROPE

    def rope(t):
        prod = t * cs
        return prod + pltpu.roll(prod, QK_ROPE, axis=1)

    z_ref[...] = jnp.dot(hn, wz_ref[...], preferred_element_type=F32)
    stage_ref[SUBLANES:, :] = jnp.dot(hn, wxbc_ref[...], preferred_element_type=F32)

    small = jnp.dot(hn, wsm_ref[...], preferred_element_type=F32)
    dt_ref[...] = small[:, LANES:]
    k_rope_t = jnp.where(low, rope(small[:, :LANES]), 0.0).T.astype(BF16)

    c_q = jnp.dot(hn, wqa_ref[...], preferred_element_type=F32)
    cqn = _rms(c_q, gq_ref[...]).astype(BF16)
    c_kv = jnp.dot(hn, wkva_ref[...], preferred_element_type=F32)
    ckvn = _rms(c_kv, gkv_ref[...])
    ckvn = ckvn.astype(BF16)
    k_nope_t = lax.dot_general(wkt_ref[...], ckvn, (((1,), (1,)), ((), ())),
                               preferred_element_type=F32)
    v_all = jnp.dot(ckvn, wv_ref[...], preferred_element_type=F32)

    for h in range(ATT_HEADS):
        qh = jnp.dot(cqn, wq_ref[:, h * QK_PAD:(h + 1) * QK_PAD], preferred_element_type=F32)
        q_ref[h, :, :QK_NOPE] = (qh[:, :QK_NOPE] * ATT_SCALE_LOG2).astype(BF16)
        q_ref[h, :, QK_NOPE:] = (rope(qh[:, QK_NOPE:]) * ATT_SCALE_LOG2).astype(BF16)
        k_ref[h, :QK_NOPE, :] = k_nope_t[h * QK_NOPE:(h + 1) * QK_NOPE, :].astype(BF16)
        k_ref[h, QK_NOPE:, :] = k_rope_t
        v_ref[h] = v_all[:, h * V_HEAD:(h + 1) * V_HEAD].astype(BF16)


def _input_projection(x2d, meta_pad, rope_blk, rope_loc, g_pre, wqa, wkva, wz, wxbc, wsm, gq, gkv, wq, wkt, wv,
                      conv_w, conv_b):
    seq = x2d.shape[0]
    n_blocks = seq // ROW_BLOCK + 1
    rows_p = seq + PAD
    pblk = lambda i: jnp.minimum(i, n_blocks - 1)
    rblk = lambda i: jnp.maximum(pblk(i) - 1, 0)
    real = lambda i: (rblk(i), 0)
    real3 = lambda i: (0, rblk(i), 0)
    padded = lambda i: (pblk(i), 0)
    padded3 = lambda i: (0, pblk(i), 0)
    lagged = lambda i: (jnp.maximum(i - 1, 0), 0)
    weights = (g_pre, wqa, wkva, wz, wxbc, wsm, gq, gkv, wq, wkt, wv, conv_w, conv_b)
    return pl.pallas_call(
        _proj_kernel,
        grid=(n_blocks + 1,),
        in_specs=[_const_spec(meta_pad.shape),
                  pl.BlockSpec((ROW_BLOCK, D_MODEL), real),
                  pl.BlockSpec((1, SUBLANES, LANES), lambda i: (pblk(i), 0, 0)),
                  _const_spec(rope_loc.shape)] + [_const_spec(w.shape) for w in weights],
        out_specs=[pl.BlockSpec((ATT_HEADS, ROW_BLOCK, QK_PAD), real3),
                   pl.BlockSpec((ATT_HEADS, QK_PAD, ROW_BLOCK), lambda i: (0, 0, pblk(i))),
                   pl.BlockSpec((ATT_HEADS, ROW_BLOCK, V_HEAD), padded3),
                   pl.BlockSpec((ROW_BLOCK, SSM_WIDTH), real),
                   pl.BlockSpec((ROW_BLOCK, CONV_DIM), lagged),
                   pl.BlockSpec((ROW_BLOCK, LANES), padded)],
        out_shape=[jax.ShapeDtypeStruct((ATT_HEADS, seq, QK_PAD), BF16),
                   jax.ShapeDtypeStruct((ATT_HEADS, QK_PAD, rows_p), BF16),
                   jax.ShapeDtypeStruct((ATT_HEADS, rows_p, V_HEAD), BF16),
                   jax.ShapeDtypeStruct((seq, SSM_WIDTH), F32),
                   jax.ShapeDtypeStruct((rows_p, CONV_DIM), F32),
                   jax.ShapeDtypeStruct((rows_p, LANES), F32)],
        scratch_shapes=[pltpu.VMEM((SUBLANES + ROW_BLOCK, CONV_DIM), F32)],
        compiler_params=pltpu.CompilerParams(dimension_semantics=("arbitrary",),
                                             vmem_limit_bytes=VMEM_PROJ),
        name="input_projection",
    )(meta_pad, x2d, rope_blk, rope_loc, *weights)


def _attn_kernel(q_ref, k_ref, v_ref, o_ref, sm_ref, pm_ref, s_ref, p_ref, m_ref, l_ref, acc_ref):
    i = pl.program_id(1)
    n_tiles = ATT_K // LANES
    assert ATT_Q == ATT_K, "one diagonal chunk per query block"
    assert ATT_UNROLL % 2 == 0 and PAD >= LANES

    def chunk_start(j):
        return pl.multiple_of(PAD + j * ATT_K, math.gcd(PAD, ATT_K))

    def scores(buf, j):
        s_ref[buf] = jnp.dot(q_ref[0], k_ref[0, :, pl.ds(chunk_start(j), ATT_K)], preferred_element_type=F32)

    def score_tiles(buf, last):
        tiles = [s_ref[buf, :, t * LANES:(t + 1) * LANES] for t in range(n_tiles)]
        if last:
            rows = lax.broadcasted_iota(jnp.int32, (ATT_Q, LANES), 0)
            cols = lax.broadcasted_iota(jnp.int32, (ATT_Q, LANES), 1)
            tiles = [jnp.where(cols + t * LANES <= rows, tiles[t], -jnp.inf) for t in range(n_tiles)]
            tiles = [jnp.where(cols >= LANES - N_META, sm_ref[...], -jnp.inf)] + tiles
        return tiles

    def consume(buf, j, last=False):
        tiles = score_tiles(buf, last)
        mx = tiles[0]
        for t in range(1, len(tiles)):
            mx = jnp.maximum(mx, tiles[t])
        m_old = m_ref[...]
        m_new = jnp.maximum(m_old, jnp.max(mx, axis=1, keepdims=True))
        m_ref[...] = m_new
        alpha = jnp.exp2(m_old - m_new)
        tiles = score_tiles(buf, last)
        probs = [jnp.exp2(t - m_new) for t in tiles]
        lsum = probs[0]
        for p in probs[1:]:
            lsum = lsum + p
        l_ref[...] = alpha * l_ref[...] + lsum
        pv = None
        if last:
            pm_ref[...] = probs.pop(0).astype(BF16)
            pv = jnp.dot(pm_ref[...], v_ref[0, PAD - LANES:PAD, :], preferred_element_type=F32)
        for t, p in enumerate(probs):
            p_ref[:, t * LANES:(t + 1) * LANES] = p.astype(BF16)
        pv_chunk = jnp.dot(p_ref[...], v_ref[0, pl.ds(chunk_start(j), ATT_K), :], preferred_element_type=F32)
        acc_ref[...] = alpha * acc_ref[...] + (pv_chunk if pv is None else pv + pv_chunk)

    def run(first, count):
        for u in range(count):
            scores((u + 1) % 2, first + u + 1)
            consume(u % 2, first + u)

    m_ref[...] = jnp.full((ATT_Q, LANES), -jnp.inf, F32)
    l_ref[...] = jnp.zeros((ATT_Q, LANES), F32)
    acc_ref[...] = jnp.zeros((ATT_Q, V_HEAD), F32)
    sm_ref[...] = jnp.dot(q_ref[0], k_ref[0, :, PAD - LANES:PAD], preferred_element_type=F32)
    scores(0, 0)

    def trip(t, carry):
        run(ATT_UNROLL * t, ATT_UNROLL)
        return carry

    lax.fori_loop(0, i // ATT_UNROLL, trip, 0)

    done = i - i % ATT_UNROLL
    for rem in range(1, ATT_UNROLL):
        @pl.when(i - done == rem)
        def _():
            run(done, rem)

    for parity in range(2):
        @pl.when(i % 2 == parity)
        def _():
            consume(parity, i, last=True)
            o_ref[...] = (acc_ref[...] / jnp.sum(l_ref[...], axis=1, keepdims=True)).astype(o_ref.dtype)


def _attention(q, k, v):
    heads, seq, _ = q.shape
    rows_p = v.shape[1]
    n_q = seq // ATT_Q
    return pl.pallas_call(
        _attn_kernel,
        grid=(heads, n_q),
        in_specs=[pl.BlockSpec((1, ATT_Q, QK_PAD), lambda h, i: (h, i, 0)),
                  pl.BlockSpec((1, QK_PAD, rows_p), lambda h, i: (h, 0, 0)),
                  pl.BlockSpec((1, rows_p, V_HEAD), lambda h, i: (h, 0, 0))],
        out_specs=pl.BlockSpec((ATT_Q, V_HEAD), lambda h, i: (i, h)),
        out_shape=jax.ShapeDtypeStruct((seq, ATT_WIDTH), BF16),
        scratch_shapes=[pltpu.VMEM((ATT_Q, LANES), F32),
                        pltpu.VMEM((ATT_Q, LANES), BF16),
                        pltpu.VMEM((2, ATT_Q, ATT_K), F32),
                        pltpu.VMEM((ATT_Q, ATT_K), BF16),
                        pltpu.VMEM((ATT_Q, LANES), F32),
                        pltpu.VMEM((ATT_Q, LANES), F32),
                        pltpu.VMEM((ATT_Q, V_HEAD), F32)],
        compiler_params=pltpu.CompilerParams(dimension_semantics=("arbitrary", "arbitrary"),
                                             vmem_limit_bytes=VMEM_ATTN),
        name="causal_attention",
    )(q, k, v)


def _split_dot(v, mat, terms):
    out = None
    rem = v
    for t in range(terms):
        piece = rem.astype(BF16)
        part = jnp.dot(piece, mat, preferred_element_type=F32)
        out = part if out is None else out + part
        if t + 1 < terms:
            rem = rem - piece.astype(F32)
    return out


def _ssd_kernel(xc_ref, dt_ref, dtb_ref, alog_ref, dskip_ref, tril_ref, expand_ref, y_ref, state_ref):
    i = pl.program_id(0)
    rows_b = xc_ref.shape[0]

    @pl.when(i == 0)
    def _():
        state_ref[...] = jnp.zeros_like(state_ref)

    xc = xc_ref

    lane = lax.broadcasted_iota(jnp.int32, (SSD_CHUNK, LANES), 1)
    row = lax.broadcasted_iota(jnp.int32, (SSD_CHUNK, LANES), 0)
    causal = lax.broadcasted_iota(jnp.int32, (SSD_CHUNK, SSD_CHUNK), 0) >= lax.broadcasted_iota(
        jnp.int32, (SSD_CHUNK, SSD_CHUNK), 1)
    low_half = lane < SSM_HEAD_DIM
    a_row = -jnp.exp(alog_ref[...])
    tril = tril_ref[...]
    heads_per_group = SSM_HEADS // SSM_GROUPS

    for c in range(rows_b // SSD_CHUNK):
        r0 = c * SSD_CHUNK
        valid = (lane < SSM_HEADS) & (row + (i * rows_b + r0) >= PAD - N_META)
        dt = jnp.where(valid, jax.nn.softplus(dt_ref[r0:r0 + SSD_CHUNK, :] + dtb_ref[...]), 0.0)
        da = dt * a_row
        acum = _cumsum_rows(tril, da)
        acum_t = acum.T
        dt_t = dt.T
        a_last = acum[SSD_CHUNK - 1:SSD_CHUNK, :]
        factors = jnp.concatenate([jnp.exp(acum), jnp.exp(a_last - acum) * dt,
                                   jnp.broadcast_to(jnp.exp(a_last), (SUBLANES, LANES))], axis=0)
        rows = slice(r0, r0 + SSD_CHUNK)
        for g in range(SSM_GROUPS):
            spread = _split_dot(factors, expand_ref[:, g * GROUP_WIDTH:(g + 1) * GROUP_WIDTH], 2)
            e_acum = spread[:SSD_CHUNK, :]
            w_end = spread[SSD_CHUNK:2 * SSD_CHUNK, :]
            c_decay = spread[2 * SSD_CHUNK:2 * SSD_CHUNK + 1, :]
            b_g = xc[rows, SSM_WIDTH + g * SSM_STATE:SSM_WIDTH + (g + 1) * SSM_STATE]
            c_g = xc[rows, SSM_WIDTH + (SSM_GROUPS + g) * SSM_STATE:
                     SSM_WIDTH + (SSM_GROUPS + g + 1) * SSM_STATE].astype(BF16)
            b_t = b_g.T.astype(BF16)
            cb = jnp.dot(c_g, b_t, preferred_element_type=F32)
            gcols = slice(g * GROUP_WIDTH, (g + 1) * GROUP_WIDTH)
            h_prev = state_ref[g]
            y_off = jnp.dot(c_g, h_prev.astype(BF16), preferred_element_type=F32)
            for jp in range(heads_per_group // 2):
                ws = []
                for r in (g * heads_per_group + 2 * jp, g * heads_per_group + 2 * jp + 1):
                    seg = acum[:, r:r + 1] - acum_t[r:r + 1, :]
                    decay = jnp.where(causal, jnp.exp(seg), 0.0)
                    ws.append((cb * decay * dt_t[r:r + 1, :]).astype(BF16))
                cols = slice(g * GROUP_WIDTH + jp * LANES, g * GROUP_WIDTH + (jp + 1) * LANES)
                x_pair = xc[rows, cols]
                x_cat = jnp.concatenate([jnp.where(low_half, x_pair, 0.0), jnp.where(low_half, 0.0, x_pair)],
                                        axis=0).astype(BF16)
                y_diag = jnp.dot(jnp.concatenate(ws, axis=1), x_cat, preferred_element_type=F32)
                pair = slice(jp * LANES, (jp + 1) * LANES)
                y_ref[rows, cols] = y_diag + y_off[:, pair] * e_acum[:, pair] + x_pair * dskip_ref[:, cols]
            x_end = (xc[rows, gcols] * w_end).astype(BF16)
            state_ref[g] = c_decay * h_prev + jnp.dot(b_t, x_end, preferred_element_type=F32)


def _cumsum_rows(tril, v):
    out = None
    rem = v
    for t in range(3):
        piece = rem.astype(BF16)
        part = jnp.dot(tril, piece, preferred_element_type=F32)
        out = part if out is None else out + part
        if t < 2:
            rem = rem - piece.astype(F32)
    return out


def _ssd_scan(xc, dt_raw, dt_bias, a_log, d_skip_x, tril, expand, seq):
    n_blocks = seq // ROW_BLOCK + 1
    consts = (dt_bias, a_log, d_skip_x, tril, expand)
    return pl.pallas_call(
        _ssd_kernel,
        grid=(n_blocks,),
        in_specs=[pl.BlockSpec((ROW_BLOCK, CONV_DIM), lambda i: (i, 0)),
                  pl.BlockSpec((ROW_BLOCK, LANES), lambda i: (i, 0))] + [_const_spec(c.shape) for c in consts],
        out_specs=pl.BlockSpec((ROW_BLOCK, SSM_WIDTH), lambda i: (jnp.maximum(i - 1, 0), 0)),
        out_shape=jax.ShapeDtypeStruct((seq, SSM_WIDTH), F32),
        scratch_shapes=[pltpu.VMEM((SSM_GROUPS, SSM_STATE, GROUP_WIDTH), F32)],
        compiler_params=pltpu.CompilerParams(dimension_semantics=("arbitrary",),
                                             vmem_limit_bytes=VMEM_SSD),
        name="ssd_scan",
    )(xc, dt_raw, *consts)


def _out_mlp_kernel(att_ref, y_ref, z_ref, x_ref, g_ssm_ref, wout_ref, g_post_ref, g_mpre_ref,
                    wup_ref, wdown_ref, g_mpost_ref, o_ref):
    z = z_ref[...]
    gated = y_ref[...] * (z * jax.nn.sigmoid(z))
    parts = []
    for g in range(SSM_GROUPS):
        gg = gated[:, g * GROUP_WIDTH:(g + 1) * GROUP_WIDTH]
        parts.append(gg * lax.rsqrt(jnp.mean(jnp.square(gg), axis=-1, keepdims=True) + EPS))
    ssm = (jnp.concatenate(parts, axis=1) * g_ssm_ref[...]).astype(BF16)
    mix = (jnp.dot(att_ref[...], wout_ref[:ATT_WIDTH, :], preferred_element_type=F32)
           + jnp.dot(ssm, wout_ref[ATT_WIDTH:, :], preferred_element_type=F32))
    h1 = x_ref[...] + _rms(mix, g_post_ref[...])
    hn = _rms(h1, g_mpre_ref[...]).astype(BF16)
    f = None
    for c in range(D_FF // FF_CHUNK):
        u = jnp.dot(hn, wup_ref[:, c * FF_CHUNK:(c + 1) * FF_CHUNK], preferred_element_type=F32)
        u = jnp.square(jnp.maximum(u, 0.0)).astype(BF16)
        part = jnp.dot(u, wdown_ref[c * FF_CHUNK:(c + 1) * FF_CHUNK, :], preferred_element_type=F32)
        f = part if f is None else f + part
    o_ref[...] = h1 + _rms(f, g_mpost_ref[...])


def _out_mlp(att, y, z, x2d, g_ssm, wout, g_post, g_mpre, wup, wdown, g_mpost):
    seq = x2d.shape[0]
    weights = (g_ssm, wout, g_post, g_mpre, wup, wdown, g_mpost)
    row = lambda i: (i, 0)
    return pl.pallas_call(
        _out_mlp_kernel,
        grid=(seq // MLP_BLOCK,),
        in_specs=[pl.BlockSpec((MLP_BLOCK, ATT_WIDTH), row),
                  pl.BlockSpec((MLP_BLOCK, SSM_WIDTH), row),
                  pl.BlockSpec((MLP_BLOCK, SSM_WIDTH), row),
                  pl.BlockSpec((MLP_BLOCK, D_MODEL), row)] + [_const_spec(w.shape) for w in weights],
        out_specs=pl.BlockSpec((MLP_BLOCK, D_MODEL), row),
        out_shape=jax.ShapeDtypeStruct((seq, D_MODEL), F32),
        compiler_params=pltpu.CompilerParams(dimension_semantics=("arbitrary",),
                                             vmem_limit_bytes=VMEM_MLP),
        name="out_proj_mlp",
    )(att, y, z, x2d, *weights)


def _rot_cols(w):
    half = QK_ROPE // 2
    return jnp.concatenate([-w[..., half:], w[..., :half]], axis=-1)


def _lane_pad(v, width=LANES):
    return jnp.pad(v, [(0, 0)] * (v.ndim - 1) + [(0, width - v.shape[-1])])


def kernel(x, meta_tokens, norm_mix_pre, w_in, q_a_norm, w_q_up, kv_a_norm, w_kv_up, conv_w, conv_b, dt_bias,
           a_log, d_skip, ssm_norm, w_out, norm_mix_post, norm_mlp_pre, w_mlp_up, w_mlp_down, norm_mlp_post):
    bsz, seq, _ = x.shape
    assert bsz == 1 and norm_mix_pre.shape[0] == 1, "single sequence, single layer"
    assert seq % ATT_Q == 0 and seq % ROW_BLOCK == 0 and seq % MLP_BLOCK == 0
    assert PAD == ROW_BLOCK, "the front pad is exactly the first block of the projection / SSD grids"
    x2d = x[0]
    row = lambda v: v.reshape(1, -1)

    o_ckv, o_kr, o_z, o_xbc, o_dt = Q_LORA, Q_LORA + KV_LORA, Q_LORA + KV_LORA + QK_ROPE, \
        Q_LORA + KV_LORA + QK_ROPE + SSM_WIDTH, Q_LORA + KV_LORA + QK_ROPE + SSM_WIDTH + CONV_DIM
    w_in0 = w_in.astype(BF16)[0]
    w_kr = w_in0[:, o_kr:o_z]
    wsm = jnp.concatenate([w_kr, _rot_cols(w_kr), _lane_pad(w_in0[:, o_dt:])], axis=1)
    wq3 = w_q_up[0].reshape(Q_LORA, ATT_HEADS, QK_DIM)
    wq = jnp.concatenate([wq3, _rot_cols(wq3[..., QK_NOPE:])], axis=-1).reshape(Q_LORA, ATT_HEADS * QK_PAD)
    wkv3 = w_kv_up[0].reshape(KV_LORA, ATT_HEADS, QK_NOPE + V_HEAD)
    wkt = wkv3[..., :QK_NOPE].reshape(KV_LORA, ATT_HEADS * QK_NOPE).T
    wv = wkv3[..., QK_NOPE:].reshape(KV_LORA, ATT_HEADS * V_HEAD)

    inv_freq = ROPE_THETA ** (-jnp.arange(0, QK_ROPE, 2, dtype=F32) / QK_ROPE)
    freq = jnp.tile(inv_freq, 4)
    low = jnp.arange(LANES) < QK_ROPE
    base = (jnp.arange(seq // ROW_BLOCK + 1) * ROW_BLOCK - (PAD - N_META)).astype(F32)
    ang_b = base[:, None] * freq[None, :]
    rope_blk = jnp.stack([jnp.cos(ang_b), jnp.where(low, -jnp.sin(ang_b), jnp.sin(ang_b))], axis=1)
    rope_blk = jnp.pad(rope_blk, ((0, 0), (0, SUBLANES - 2), (0, 0)))
    ang_l = jnp.arange(ROW_BLOCK, dtype=F32)[:, None] * freq[None, :]
    rope_loc = jnp.stack([jnp.where(low, jnp.cos(ang_l), jnp.sin(ang_l)),
                          jnp.where(low, jnp.sin(ang_l), jnp.cos(ang_l))])

    meta_pad = jnp.concatenate([jnp.zeros((PAD - N_META, D_MODEL), F32), meta_tokens.astype(F32)], axis=0)

    q, k, v, z, xc, dt_raw = _input_projection(
        x2d, meta_pad, rope_blk, rope_loc, row(norm_mix_pre[0]),
        w_in0[:, :o_ckv].astype(BF16), w_in0[:, o_ckv:o_kr].astype(BF16), w_in0[:, o_z:o_xbc].astype(BF16),
        w_in0[:, o_xbc:o_dt].astype(BF16), wsm.astype(BF16),
        row(q_a_norm[0]), row(kv_a_norm[0]), wq.astype(BF16), wkt.astype(BF16), wv.astype(BF16),
        conv_w[0], row(conv_b[0]))

    att = _attention(q, k, v)

    tril = jnp.tril(jnp.ones((SSD_CHUNK, SSD_CHUNK), BF16))
    expand = (jnp.arange(LANES)[:, None] == jnp.arange(SSM_WIDTH)[None, :] // SSM_HEAD_DIM).astype(BF16)
    y = _ssd_scan(xc, dt_raw, _lane_pad(row(dt_bias[0])), _lane_pad(row(a_log[0])),
                  row(jnp.repeat(d_skip[0], SSM_HEAD_DIM)), tril, expand, seq)

    out = _out_mlp(att, y, z, x2d, row(ssm_norm[0]), w_out.astype(BF16), row(norm_mix_post[0]),
                   row(norm_mlp_pre[0]), w_mlp_up.astype(BF16), w_mlp_down.astype(BF16),
                   row(norm_mlp_post[0]))
    return out[None]
```

```python
import math

import jax
import jax.numpy as jnp
from jax import lax
from jax.experimental import pallas as pl
from jax.experimental.pallas import tpu as pltpu

D_MODEL = 1024
N_META = 16
EPS = 1e-6
ATT_HEADS = 8
Q_LORA = 384
KV_LORA = 256
QK_NOPE = 128
QK_ROPE = 64
V_HEAD = 128
QK_DIM = QK_NOPE + QK_ROPE
ROPE_THETA = 10000.0
ATT_WIDTH = ATT_HEADS * V_HEAD
SSM_HEADS = 16
SSM_HEAD_DIM = 64
SSM_WIDTH = SSM_HEADS * SSM_HEAD_DIM
SSM_GROUPS = 2
SSM_STATE = 128
GROUP_WIDTH = SSM_WIDTH // SSM_GROUPS
CONV_K = 4
CONV_DIM = SSM_WIDTH + 2 * SSM_GROUPS * SSM_STATE
D_FF = 4 * D_MODEL

LANES = 128
SUBLANES = 8
VMEM_LIMIT_BYTES = 56 * 1024 * 1024

QK_PAD = 2 * LANES
PAD = 512
ROW_BLOCK = 512
MLP_BLOCK = 512
ATT_Q = 512
ATT_K = 512
ATT_UNROLL = 8
SSD_CHUNK = 128
FF_CHUNK = 1024
ATT_SCALE_LOG2 = (QK_DIM ** -0.5) * math.log2(math.e)

BF16 = jnp.bfloat16
F32 = jnp.float32


def _const_spec(shape):
    nd = len(shape)
    block = (None,) + tuple(shape[1:]) if nd == 3 and shape[0] == 1 else tuple(shape)
    return pl.BlockSpec(block, lambda *_: (0,) * nd, pipeline_mode=pl.Buffered(1))


def _rms(x, gain):
    return x * lax.rsqrt(jnp.mean(jnp.square(x), axis=-1, keepdims=True) + EPS) * gain


def _proj_kernel(meta_ref, x_ref, rope_blk_ref, rope_loc_ref, g_pre_ref, wqa_ref, wkva_ref, wz_ref, wxbc_ref,
                 wsm_ref, gq_ref, gkv_ref, wq_ref, wkt_ref, wv_ref, convw_ref, convb_ref,
                 q_ref, k_ref, v_ref, z_ref, xbc_ref, dt_ref, stage_ref):
    i = pl.program_id(0)
    rows_b = x_ref.shape[0]

    @pl.when(i == 0)
    def _():
        stage_ref[...] = jnp.zeros_like(stage_ref)

    u = stage_ref[...]
    u1 = pltpu.roll(u, 1, axis=0)
    older = pltpu.roll(convw_ref[1:2, :] * u + convw_ref[0:1, :] * u1, 2, axis=0)
    conv = (convw_ref[3:4, :] * u + convw_ref[2:3, :] * u1 + older + convb_ref[...])[SUBLANES:, :]
    xbc_ref[...] = conv * jax.nn.sigmoid(conv)
    stage_ref[0:SUBLANES, :] = stage_ref[rows_b:rows_b + SUBLANES, :]

    xin = jnp.where(i == 0, meta_ref[...], x_ref[...])
    hn = _rms(xin, g_pre_ref[...]).astype(BF16)

    blk = rope_blk_ref[0]
    cs = blk[0:1, :] * rope_loc_ref[0] + blk[1:2, :] * rope_loc_ref[1]
    low = lax.broadcasted_iota(jnp.int32, cs.shape, 1) < QK_ROPE

    def rope(t):
        prod = t * cs
        return prod + pltpu.roll(prod, QK_ROPE, axis=1)

    z_ref[...] = jnp.dot(hn, wz_ref[...], preferred_element_type=F32)
    stage_ref[SUBLANES:, :] = jnp.dot(hn, wxbc_ref[...], preferred_element_type=F32)

    small = jnp.dot(hn, wsm_ref[...], preferred_element_type=F32)
    dt_ref[...] = small[:, LANES:]
    k_rope_t = jnp.where(low, rope(small[:, :LANES]), 0.0).T.astype(BF16)

    c_q = jnp.dot(hn, wqa_ref[...], preferred_element_type=F32)
    cqn = _rms(c_q, gq_ref[...]).astype(BF16)
    c_kv = jnp.dot(hn, wkva_ref[...], preferred_element_type=F32)
    ckvn = _rms(c_kv, gkv_ref[...])
    ckvn = ckvn.astype(BF16)
    k_nope_t = lax.dot_general(wkt_ref[...], ckvn, (((1,), (1,)), ((), ())),
                               preferred_element_type=F32)
    v_all = jnp.dot(ckvn, wv_ref[...], preferred_element_type=F32)

    for h in range(ATT_HEADS):
        qh = jnp.dot(cqn, wq_ref[:, h * QK_PAD:(h + 1) * QK_PAD], preferred_element_type=F32)
        q_ref[h, :, :QK_NOPE] = (qh[:, :QK_NOPE] * ATT_SCALE_LOG2).astype(BF16)
        q_ref[h, :, QK_NOPE:] = (rope(qh[:, QK_NOPE:]) * ATT_SCALE_LOG2).astype(BF16)
        k_ref[h, :QK_NOPE, :] = k_nope_t[h * QK_NOPE:(h + 1) * QK_NOPE, :].astype(BF16)
        k_ref[h, QK_NOPE:, :] = k_rope_t
        v_ref[h] = v_all[:, h * V_HEAD:(h + 1) * V_HEAD].astype(BF16)


def _input_projection(x2d, meta_pad, rope_blk, rope_loc, g_pre, wqa, wkva, wz, wxbc, wsm, gq, gkv, wq, wkt, wv,
                      conv_w, conv_b):
    seq = x2d.shape[0]
    n_blocks = seq // ROW_BLOCK + 1
    rows_p = seq + PAD
    pblk = lambda i: jnp.minimum(i, n_blocks - 1)
    rblk = lambda i: jnp.maximum(pblk(i) - 1, 0)
    real = lambda i: (rblk(i), 0)
    real3 = lambda i: (0, rblk(i), 0)
    padded = lambda i: (pblk(i), 0)
    padded3 = lambda i: (0, pblk(i), 0)
    lagged = lambda i: (jnp.maximum(i - 1, 0), 0)
    weights = (g_pre, wqa, wkva, wz, wxbc, wsm, gq, gkv, wq, wkt, wv, conv_w, conv_b)
    return pl.pallas_call(
        _proj_kernel,
        grid=(n_blocks + 1,),
        in_specs=[_const_spec(meta_pad.shape),
                  pl.BlockSpec((ROW_BLOCK, D_MODEL), real),
                  pl.BlockSpec((1, SUBLANES, LANES), lambda i: (pblk(i), 0, 0)),
                  _const_spec(rope_loc.shape)] + [_const_spec(w.shape) for w in weights],
        out_specs=[pl.BlockSpec((ATT_HEADS, ROW_BLOCK, QK_PAD), real3),
                   pl.BlockSpec((ATT_HEADS, QK_PAD, ROW_BLOCK), lambda i: (0, 0, pblk(i))),
                   pl.BlockSpec((ATT_HEADS, ROW_BLOCK, V_HEAD), padded3),
                   pl.BlockSpec((ROW_BLOCK, SSM_WIDTH), real),
                   pl.BlockSpec((ROW_BLOCK, CONV_DIM), lagged),
                   pl.BlockSpec((ROW_BLOCK, LANES), padded)],
        out_shape=[jax.ShapeDtypeStruct((ATT_HEADS, seq, QK_PAD), BF16),
                   jax.ShapeDtypeStruct((ATT_HEADS, QK_PAD, rows_p), BF16),
                   jax.ShapeDtypeStruct((ATT_HEADS, rows_p, V_HEAD), BF16),
                   jax.ShapeDtypeStruct((seq, SSM_WIDTH), F32),
                   jax.ShapeDtypeStruct((rows_p, CONV_DIM), F32),
                   jax.ShapeDtypeStruct((rows_p, LANES), F32)],
        scratch_shapes=[pltpu.VMEM((SUBLANES + ROW_BLOCK, CONV_DIM), F32)],
        compiler_params=pltpu.CompilerParams(dimension_semantics=("arbitrary",),
                                             vmem_limit_bytes=VMEM_LIMIT_BYTES),
        name="input_projection",
    )(meta_pad, x2d, rope_blk, rope_loc, *weights)


def _attn_kernel(q_ref, k_ref, v_ref, o_ref, sm_ref, pm_ref, s_ref, p_ref, m_ref, l_ref, acc_ref):
    i = pl.program_id(1)
    n_tiles = ATT_K // LANES
    assert ATT_Q == ATT_K, "one diagonal chunk per query block"
    assert ATT_UNROLL % 2 == 0 and PAD >= LANES

    def chunk_start(j):
        return pl.multiple_of(PAD + j * ATT_K, math.gcd(PAD, ATT_K))

    def scores(buf, j):
        s_ref[buf] = jnp.dot(q_ref[0], k_ref[0, :, pl.ds(chunk_start(j), ATT_K)], preferred_element_type=F32)

    def score_tiles(buf, last):
        tiles = [s_ref[buf, :, t * LANES:(t + 1) * LANES] for t in range(n_tiles)]
        if last:
            rows = lax.broadcasted_iota(jnp.int32, (ATT_Q, LANES), 0)
            cols = lax.broadcasted_iota(jnp.int32, (ATT_Q, LANES), 1)
            tiles = [jnp.where(cols + t * LANES <= rows, tiles[t], -jnp.inf) for t in range(n_tiles)]
            tiles = [jnp.where(cols >= LANES - N_META, sm_ref[...], -jnp.inf)] + tiles
        return tiles

    def consume(buf, j, last=False):
        tiles = score_tiles(buf, last)
        mx = tiles[0]
        for t in range(1, len(tiles)):
            mx = jnp.maximum(mx, tiles[t])
        m_old = m_ref[...]
        m_new = jnp.maximum(m_old, jnp.max(mx, axis=1, keepdims=True))
        m_ref[...] = m_new
        alpha = jnp.exp2(m_old - m_new)
        tiles = score_tiles(buf, last)
        probs = [jnp.exp2(t - m_new) for t in tiles]
        lsum = probs[0]
        for p in probs[1:]:
            lsum = lsum + p
        l_ref[...] = alpha * l_ref[...] + lsum
        pv = None
        if last:
            pm_ref[...] = probs.pop(0).astype(BF16)
            pv = jnp.dot(pm_ref[...], v_ref[0, PAD - LANES:PAD, :], preferred_element_type=F32)
        for t, p in enumerate(probs):
            p_ref[:, t * LANES:(t + 1) * LANES] = p.astype(BF16)
        pv_chunk = jnp.dot(p_ref[...], v_ref[0, pl.ds(chunk_start(j), ATT_K), :], preferred_element_type=F32)
        acc_ref[...] = alpha * acc_ref[...] + (pv_chunk if pv is None else pv + pv_chunk)

    def run(first, count):
        for u in range(count):
            scores((u + 1) % 2, first + u + 1)
            consume(u % 2, first + u)

    m_ref[...] = jnp.full((ATT_Q, LANES), -jnp.inf, F32)
    l_ref[...] = jnp.zeros((ATT_Q, LANES), F32)
    acc_ref[...] = jnp.zeros((ATT_Q, V_HEAD), F32)
    sm_ref[...] = jnp.dot(q_ref[0], k_ref[0, :, PAD - LANES:PAD], preferred_element_type=F32)
    scores(0, 0)

    def trip(t, carry):
        run(ATT_UNROLL * t, ATT_UNROLL)
        return carry

    lax.fori_loop(0, i // ATT_UNROLL, trip, 0)

    done = i - i % ATT_UNROLL
    for rem in range(1, ATT_UNROLL):
        @pl.when(i - done == rem)
        def _():
            run(done, rem)

    for parity in range(2):
        @pl.when(i % 2 == parity)
        def _():
            consume(parity, i, last=True)
            o_ref[...] = (acc_ref[...] / jnp.sum(l_ref[...], axis=1, keepdims=True)).astype(o_ref.dtype)


def _attention(q, k, v):
    heads, seq, _ = q.shape
    rows_p = v.shape[1]
    n_q = seq // ATT_Q
    return pl.pallas_call(
        _attn_kernel,
        grid=(heads, n_q),
        in_specs=[pl.BlockSpec((1, ATT_Q, QK_PAD), lambda h, i: (h, i, 0)),
                  pl.BlockSpec((1, QK_PAD, rows_p), lambda h, i: (h, 0, 0)),
                  pl.BlockSpec((1, rows_p, V_HEAD), lambda h, i: (h, 0, 0))],
        out_specs=pl.BlockSpec((ATT_Q, V_HEAD), lambda h, i: (i, h)),
        out_shape=jax.ShapeDtypeStruct((seq, ATT_WIDTH), BF16),
        scratch_shapes=[pltpu.VMEM((ATT_Q, LANES), F32),
                        pltpu.VMEM((ATT_Q, LANES), BF16),
                        pltpu.VMEM((2, ATT_Q, ATT_K), F32),
                        pltpu.VMEM((ATT_Q, ATT_K), BF16),
                        pltpu.VMEM((ATT_Q, LANES), F32),
                        pltpu.VMEM((ATT_Q, LANES), F32),
                        pltpu.VMEM((ATT_Q, V_HEAD), F32)],
        compiler_params=pltpu.CompilerParams(dimension_semantics=("arbitrary", "arbitrary"),
                                             vmem_limit_bytes=VMEM_LIMIT_BYTES + 4 * 1024 * 1024),
        name="causal_attention",
    )(q, k, v)


def _split_dot(v, mat, terms):
    out = None
    rem = v
    for t in range(terms):
        piece = rem.astype(BF16)
        part = jnp.dot(piece, mat, preferred_element_type=F32)
        out = part if out is None else out + part
        if t + 1 < terms:
            rem = rem - piece.astype(F32)
    return out


def _ssd_kernel(xc_ref, dt_ref, dtb_ref, alog_ref, dskip_ref, tril_ref, expand_ref, y_ref, state_ref):
    i = pl.program_id(0)
    rows_b = xc_ref.shape[0]

    @pl.when(i == 0)
    def _():
        state_ref[...] = jnp.zeros_like(state_ref)

    xc = xc_ref

    lane = lax.broadcasted_iota(jnp.int32, (SSD_CHUNK, LANES), 1)
    row = lax.broadcasted_iota(jnp.int32, (SSD_CHUNK, LANES), 0)
    causal = lax.broadcasted_iota(jnp.int32, (SSD_CHUNK, SSD_CHUNK), 0) >= lax.broadcasted_iota(
        jnp.int32, (SSD_CHUNK, SSD_CHUNK), 1)
    low_half = lane < SSM_HEAD_DIM
    a_row = -jnp.exp(alog_ref[...])
    tril = tril_ref[...]
    heads_per_group = SSM_HEADS // SSM_GROUPS

    for c in range(rows_b // SSD_CHUNK):
        r0 = c * SSD_CHUNK
        valid = (lane < SSM_HEADS) & (row + (i * rows_b + r0) >= PAD - N_META)
        dt = jnp.where(valid, jax.nn.softplus(dt_ref[r0:r0 + SSD_CHUNK, :] + dtb_ref[...]), 0.0)
        da = dt * a_row
        acum = _cumsum_rows(tril, da)
        acum_t = acum.T
        dt_t = dt.T
        a_last = acum[SSD_CHUNK - 1:SSD_CHUNK, :]
        factors = jnp.concatenate([jnp.exp(acum), jnp.exp(a_last - acum) * dt,
                                   jnp.broadcast_to(jnp.exp(a_last), (SUBLANES, LANES))], axis=0)
        rows = slice(r0, r0 + SSD_CHUNK)
        for g in range(SSM_GROUPS):
            spread = _split_dot(factors, expand_ref[:, g * GROUP_WIDTH:(g + 1) * GROUP_WIDTH], 2)
            e_acum = spread[:SSD_CHUNK, :]
            w_end = spread[SSD_CHUNK:2 * SSD_CHUNK, :]
            c_decay = spread[2 * SSD_CHUNK:2 * SSD_CHUNK + 1, :]
            b_g = xc[rows, SSM_WIDTH + g * SSM_STATE:SSM_WIDTH + (g + 1) * SSM_STATE]
            c_g = xc[rows, SSM_WIDTH + (SSM_GROUPS + g) * SSM_STATE:
                     SSM_WIDTH + (SSM_GROUPS + g + 1) * SSM_STATE].astype(BF16)
            b_t = b_g.T.astype(BF16)
            cb = jnp.dot(c_g, b_t, preferred_element_type=F32)
            gcols = slice(g * GROUP_WIDTH, (g + 1) * GROUP_WIDTH)
            h_prev = state_ref[g]
            y_off = jnp.dot(c_g, h_prev.astype(BF16), preferred_element_type=F32)
            for jp in range(heads_per_group // 2):
                ws = []
                for r in (g * heads_per_group + 2 * jp, g * heads_per_group + 2 * jp + 1):
                    seg = acum[:, r:r + 1] - acum_t[r:r + 1, :]
                    decay = jnp.where(causal, jnp.exp(seg), 0.0)
                    ws.append((cb * decay * dt_t[r:r + 1, :]).astype(BF16))
                cols = slice(g * GROUP_WIDTH + jp * LANES, g * GROUP_WIDTH + (jp + 1) * LANES)
                x_pair = xc[rows, cols]
                x_cat = jnp.concatenate([jnp.where(low_half, x_pair, 0.0), jnp.where(low_half, 0.0, x_pair)],
                                        axis=0).astype(BF16)
                y_diag = jnp.dot(jnp.concatenate(ws, axis=1), x_cat, preferred_element_type=F32)
                pair = slice(jp * LANES, (jp + 1) * LANES)
                y_ref[rows, cols] = y_diag + y_off[:, pair] * e_acum[:, pair] + x_pair * dskip_ref[:, cols]
            x_end = (xc[rows, gcols] * w_end).astype(BF16)
            state_ref[g] = c_decay * h_prev + jnp.dot(b_t, x_end, preferred_element_type=F32)


def _cumsum_rows(tril, v):
    out = None
    rem = v
    for t in range(3):
        piece = rem.astype(BF16)
        part = jnp.dot(tril, piece, preferred_element_type=F32)
        out = part if out is None else out + part
        if t < 2:
            rem = rem - piece.astype(F32)
    return out


def _ssd_scan(xc, dt_raw, dt_bias, a_log, d_skip_x, tril, expand, seq):
    n_blocks = seq // ROW_BLOCK + 1
    consts = (dt_bias, a_log, d_skip_x, tril, expand)
    return pl.pallas_call(
        _ssd_kernel,
        grid=(n_blocks,),
        in_specs=[pl.BlockSpec((ROW_BLOCK, CONV_DIM), lambda i: (i, 0)),
                  pl.BlockSpec((ROW_BLOCK, LANES), lambda i: (i, 0))] + [_const_spec(c.shape) for c in consts],
        out_specs=pl.BlockSpec((ROW_BLOCK, SSM_WIDTH), lambda i: (jnp.maximum(i - 1, 0), 0)),
        out_shape=jax.ShapeDtypeStruct((seq, SSM_WIDTH), F32),
        scratch_shapes=[pltpu.VMEM((SSM_GROUPS, SSM_STATE, GROUP_WIDTH), F32)],
        compiler_params=pltpu.CompilerParams(dimension_semantics=("arbitrary",),
                                             vmem_limit_bytes=VMEM_LIMIT_BYTES),
        name="ssd_scan",
    )(xc, dt_raw, *consts)


def _out_mlp_kernel(att_ref, y_ref, z_ref, x_ref, g_ssm_ref, wout_ref, g_post_ref, g_mpre_ref,
                    wup_ref, wdown_ref, g_mpost_ref, o_ref):
    z = z_ref[...]
    gated = y_ref[...] * (z * jax.nn.sigmoid(z))
    parts = []
    for g in range(SSM_GROUPS):
        gg = gated[:, g * GROUP_WIDTH:(g + 1) * GROUP_WIDTH]
        parts.append(gg * lax.rsqrt(jnp.mean(jnp.square(gg), axis=-1, keepdims=True) + EPS))
    ssm = (jnp.concatenate(parts, axis=1) * g_ssm_ref[...]).astype(BF16)
    mix = (jnp.dot(att_ref[...], wout_ref[:ATT_WIDTH, :], preferred_element_type=F32)
           + jnp.dot(ssm, wout_ref[ATT_WIDTH:, :], preferred_element_type=F32))
    h1 = x_ref[...] + _rms(mix, g_post_ref[...])
    hn = _rms(h1, g_mpre_ref[...]).astype(BF16)
    f = None
    for c in range(D_FF // FF_CHUNK):
        u = jnp.dot(hn, wup_ref[:, c * FF_CHUNK:(c + 1) * FF_CHUNK], preferred_element_type=F32)
        u = jnp.square(jnp.maximum(u, 0.0)).astype(BF16)
        part = jnp.dot(u, wdown_ref[c * FF_CHUNK:(c + 1) * FF_CHUNK, :], preferred_element_type=F32)
        f = part if f is None else f + part
    o_ref[...] = h1 + _rms(f, g_mpost_ref[...])


def _out_mlp(att, y, z, x2d, g_ssm, wout, g_post, g_mpre, wup, wdown, g_mpost):
    seq = x2d.shape[0]
    weights = (g_ssm, wout, g_post, g_mpre, wup, wdown, g_mpost)
    row = lambda i: (i, 0)
    return pl.pallas_call(
        _out_mlp_kernel,
        grid=(seq // MLP_BLOCK,),
        in_specs=[pl.BlockSpec((MLP_BLOCK, ATT_WIDTH), row),
                  pl.BlockSpec((MLP_BLOCK, SSM_WIDTH), row),
                  pl.BlockSpec((MLP_BLOCK, SSM_WIDTH), row),
                  pl.BlockSpec((MLP_BLOCK, D_MODEL), row)] + [_const_spec(w.shape) for w in weights],
        out_specs=pl.BlockSpec((MLP_BLOCK, D_MODEL), row),
        out_shape=jax.ShapeDtypeStruct((seq, D_MODEL), F32),
        compiler_params=pltpu.CompilerParams(dimension_semantics=("arbitrary",),
                                             vmem_limit_bytes=VMEM_LIMIT_BYTES),
        name="out_proj_mlp",
    )(att, y, z, x2d, *weights)


def _rot_cols(w):
    half = QK_ROPE // 2
    return jnp.concatenate([-w[..., half:], w[..., :half]], axis=-1)


def _lane_pad(v, width=LANES):
    return jnp.pad(v, [(0, 0)] * (v.ndim - 1) + [(0, width - v.shape[-1])])


def kernel(x, meta_tokens, norm_mix_pre, w_in, q_a_norm, w_q_up, kv_a_norm, w_kv_up, conv_w, conv_b, dt_bias,
           a_log, d_skip, ssm_norm, w_out, norm_mix_post, norm_mlp_pre, w_mlp_up, w_mlp_down, norm_mlp_post):
    bsz, seq, _ = x.shape
    assert bsz == 1 and norm_mix_pre.shape[0] == 1, "single sequence, single layer"
    assert seq % ATT_Q == 0 and seq % ROW_BLOCK == 0 and seq % MLP_BLOCK == 0
    assert PAD == ROW_BLOCK, "the front pad is exactly the first block of the projection / SSD grids"
    x2d = x[0]
    row = lambda v: v.reshape(1, -1)

    o_ckv, o_kr, o_z, o_xbc, o_dt = Q_LORA, Q_LORA + KV_LORA, Q_LORA + KV_LORA + QK_ROPE, \
        Q_LORA + KV_LORA + QK_ROPE + SSM_WIDTH, Q_LORA + KV_LORA + QK_ROPE + SSM_WIDTH + CONV_DIM
    w_in0 = w_in.astype(BF16)[0]
    w_kr = w_in0[:, o_kr:o_z]
    wsm = jnp.concatenate([w_kr, _rot_cols(w_kr), _lane_pad(w_in0[:, o_dt:])], axis=1)
    wq3 = w_q_up[0].reshape(Q_LORA, ATT_HEADS, QK_DIM)
    wq = jnp.concatenate([wq3, _rot_cols(wq3[..., QK_NOPE:])], axis=-1).reshape(Q_LORA, ATT_HEADS * QK_PAD)
    wkv3 = w_kv_up[0].reshape(KV_LORA, ATT_HEADS, QK_NOPE + V_HEAD)
    wkt = wkv3[..., :QK_NOPE].reshape(KV_LORA, ATT_HEADS * QK_NOPE).T
    wv = wkv3[..., QK_NOPE:].reshape(KV_LORA, ATT_HEADS * V_HEAD)

    inv_freq = ROPE_THETA ** (-jnp.arange(0, QK_ROPE, 2, dtype=F32) / QK_ROPE)
    freq = jnp.tile(inv_freq, 4)
    low = jnp.arange(LANES) < QK_ROPE
    base = (jnp.arange(seq // ROW_BLOCK + 1) * ROW_BLOCK - (PAD - N_META)).astype(F32)
    ang_b = base[:, None] * freq[None, :]
    rope_blk = jnp.stack([jnp.cos(ang_b), jnp.where(low, -jnp.sin(ang_b), jnp.sin(ang_b))], axis=1)
    rope_blk = jnp.pad(rope_blk, ((0, 0), (0, SUBLANES - 2), (0, 0)))
    ang_l = jnp.arange(ROW_BLOCK, dtype=F32)[:, None] * freq[None, :]
    rope_loc = jnp.stack([jnp.where(low, jnp.cos(ang_l), jnp.sin(ang_l)),
                          jnp.where(low, jnp.sin(ang_l), jnp.cos(ang_l))])

    meta_pad = jnp.concatenate([jnp.zeros((PAD - N_META, D_MODEL), F32), meta_tokens.astype(F32)], axis=0)

    q, k, v, z, xc, dt_raw = _input_projection(
        x2d, meta_pad, rope_blk, rope_loc, row(norm_mix_pre[0]),
        w_in0[:, :o_ckv].astype(BF16), w_in0[:, o_ckv:o_kr].astype(BF16), w_in0[:, o_z:o_xbc].astype(BF16),
        w_in0[:, o_xbc:o_dt].astype(BF16), wsm.astype(BF16),
        row(q_a_norm[0]), row(kv_a_norm[0]), wq.astype(BF16), wkt.astype(BF16), wv.astype(BF16),
        conv_w[0], row(conv_b[0]))

    att = _attention(q, k, v)

    tril = jnp.tril(jnp.ones((SSD_CHUNK, SSD_CHUNK), BF16))
    expand = (jnp.arange(LANES)[:, None] == jnp.arange(SSM_WIDTH)[None, :] // SSM_HEAD_DIM).astype(BF16)
    y = _ssd_scan(xc, dt_raw, _lane_pad(row(dt_bias[0])), _lane_pad(row(a_log[0])),
                  row(jnp.repeat(d_skip[0], SSM_HEAD_DIM)), tril, expand, seq)

    out = _out_mlp(att, y, z, x2d, row(ssm_norm[0]), w_out.astype(BF16), row(norm_mix_post[0]),
                   row(norm_mlp_pre[0]), w_mlp_up.astype(BF16), w_mlp_down.astype(BF16),
                   row(norm_mlp_post[0]))
    return out[None]
```

```python
import math

import jax
import jax.numpy as jnp
from jax import lax
from jax.experimental import pallas as pl
from jax.experimental.pallas import tpu as pltpu

D_MODEL = 1024
N_META = 16
EPS = 1e-6
ATT_HEADS = 8
Q_LORA = 384
KV_LORA = 256
QK_NOPE = 128
QK_ROPE = 64
V_HEAD = 128
QK_DIM = QK_NOPE + QK_ROPE
ROPE_THETA = 10000.0
ATT_WIDTH = ATT_HEADS * V_HEAD
SSM_HEADS = 16
SSM_HEAD_DIM = 64
SSM_WIDTH = SSM_HEADS * SSM_HEAD_DIM
SSM_GROUPS = 2
SSM_STATE = 128
GROUP_WIDTH = SSM_WIDTH // SSM_GROUPS
CONV_K = 4
CONV_DIM = SSM_WIDTH + 2 * SSM_GROUPS * SSM_STATE
D_FF = 4 * D_MODEL

LANES = 128
SUBLANES = 8
VMEM_LIMIT_BYTES = 56 * 1024 * 1024

QK_PAD = 2 * LANES
PAD = 512
ROW_BLOCK = 512
MLP_BLOCK = 512
ATT_Q = 512
ATT_K = 512
ATT_UNROLL = 8
SSD_CHUNK = 128
FF_CHUNK = 1024
ATT_SCALE_LOG2 = (QK_DIM ** -0.5) * math.log2(math.e)

BF16 = jnp.bfloat16
F32 = jnp.float32


def _const_spec(shape):
    nd = len(shape)
    block = (None,) + tuple(shape[1:]) if nd == 3 and shape[0] == 1 else tuple(shape)
    return pl.BlockSpec(block, lambda *_: (0,) * nd, pipeline_mode=pl.Buffered(1))


def _rms(x, gain):
    return x * lax.rsqrt(jnp.mean(jnp.square(x), axis=-1, keepdims=True) + EPS) * gain


def _proj_kernel(meta_ref, x_ref, rope_blk_ref, rope_loc_ref, g_pre_ref, wqa_ref, wkva_ref, wz_ref, wxbc_ref,
                 wsm_ref, gq_ref, gkv_ref, wq_ref, wkt_ref, wv_ref, convw_ref, convb_ref,
                 q_ref, k_ref, v_ref, z_ref, xbc_ref, dt_ref, stage_ref):
    i = pl.program_id(0)
    rows_b = x_ref.shape[0]

    @pl.when(i == 0)
    def _():
        stage_ref[...] = jnp.zeros_like(stage_ref)

    u = stage_ref[...]
    u1 = pltpu.roll(u, 1, axis=0)
    older = pltpu.roll(convw_ref[1:2, :] * u + convw_ref[0:1, :] * u1, 2, axis=0)
    conv = (convw_ref[3:4, :] * u + convw_ref[2:3, :] * u1 + older + convb_ref[...])[SUBLANES:, :]
    xbc_ref[...] = conv * jax.nn.sigmoid(conv)
    stage_ref[0:SUBLANES, :] = stage_ref[rows_b:rows_b + SUBLANES, :]

    xin = jnp.where(i == 0, meta_ref[...], x_ref[...])
    hn = _rms(xin, g_pre_ref[...]).astype(BF16)

    blk = rope_blk_ref[0]
    cs = blk[0:1, :] * rope_loc_ref[0] + blk[1:2, :] * rope_loc_ref[1]
    low = lax.broadcasted_iota(jnp.int32, cs.shape, 1) < QK_ROPE

    def rope(t):
        prod = t * cs
        return prod + pltpu.roll(prod, QK_ROPE, axis=1)

    z_ref[...] = jnp.dot(hn, wz_ref[...], preferred_element_type=F32)
    stage_ref[SUBLANES:, :] = jnp.dot(hn, wxbc_ref[...], preferred_element_type=F32)

    small = jnp.dot(hn, wsm_ref[...], preferred_element_type=F32)
    dt_ref[...] = small[:, LANES:]
    k_rope_t = jnp.where(low, rope(small[:, :LANES]), 0.0).T.astype(BF16)

    c_q = jnp.dot(hn, wqa_ref[...], preferred_element_type=F32)
    cqn = _rms(c_q, gq_ref[...]).astype(BF16)
    c_kv = jnp.dot(hn, wkva_ref[...], preferred_element_type=F32)
    ckvn = _rms(c_kv, gkv_ref[...])
    ckvn = ckvn.astype(BF16)
    k_nope_t = lax.dot_general(wkt_ref[...], ckvn, (((1,), (1,)), ((), ())),
                               preferred_element_type=F32)
    v_all = jnp.dot(ckvn, wv_ref[...], preferred_element_type=F32)

    for h in range(ATT_HEADS):
        qh = jnp.dot(cqn, wq_ref[:, h * QK_PAD:(h + 1) * QK_PAD], preferred_element_type=F32)
        q_ref[h, :, :QK_NOPE] = (qh[:, :QK_NOPE] * ATT_SCALE_LOG2).astype(BF16)
        q_ref[h, :, QK_NOPE:] = (rope(qh[:, QK_NOPE:]) * ATT_SCALE_LOG2).astype(BF16)
        k_ref[h, :QK_NOPE, :] = k_nope_t[h * QK_NOPE:(h + 1) * QK_NOPE, :].astype(BF16)
        k_ref[h, QK_NOPE:, :] = k_rope_t
        v_ref[h] = v_all[:, h * V_HEAD:(h + 1) * V_HEAD].astype(BF16)


def _input_projection(x2d, meta_pad, rope_blk, rope_loc, g_pre, wqa, wkva, wz, wxbc, wsm, gq, gkv, wq, wkt, wv,
                      conv_w, conv_b):
    seq = x2d.shape[0]
    n_blocks = seq // ROW_BLOCK + 1
    rows_p = seq + PAD
    pblk = lambda i: jnp.minimum(i, n_blocks - 1)
    rblk = lambda i: jnp.maximum(pblk(i) - 1, 0)
    real = lambda i: (rblk(i), 0)
    real3 = lambda i: (0, rblk(i), 0)
    padded = lambda i: (pblk(i), 0)
    padded3 = lambda i: (0, pblk(i), 0)
    lagged = lambda i: (jnp.maximum(i - 1, 0), 0)
    weights = (g_pre, wqa, wkva, wz, wxbc, wsm, gq, gkv, wq, wkt, wv, conv_w, conv_b)
    return pl.pallas_call(
        _proj_kernel,
        grid=(n_blocks + 1,),
        in_specs=[_const_spec(meta_pad.shape),
                  pl.BlockSpec((ROW_BLOCK, D_MODEL), real),
                  pl.BlockSpec((1, SUBLANES, LANES), lambda i: (pblk(i), 0, 0)),
                  _const_spec(rope_loc.shape)] + [_const_spec(w.shape) for w in weights],
        out_specs=[pl.BlockSpec((ATT_HEADS, ROW_BLOCK, QK_PAD), real3),
                   pl.BlockSpec((ATT_HEADS, QK_PAD, ROW_BLOCK), lambda i: (0, 0, pblk(i))),
                   pl.BlockSpec((ATT_HEADS, ROW_BLOCK, V_HEAD), padded3),
                   pl.BlockSpec((ROW_BLOCK, SSM_WIDTH), real),
                   pl.BlockSpec((ROW_BLOCK, CONV_DIM), lagged),
                   pl.BlockSpec((ROW_BLOCK, LANES), padded)],
        out_shape=[jax.ShapeDtypeStruct((ATT_HEADS, seq, QK_PAD), BF16),
                   jax.ShapeDtypeStruct((ATT_HEADS, QK_PAD, rows_p), BF16),
                   jax.ShapeDtypeStruct((ATT_HEADS, rows_p, V_HEAD), BF16),
                   jax.ShapeDtypeStruct((seq, SSM_WIDTH), F32),
                   jax.ShapeDtypeStruct((rows_p, CONV_DIM), F32),
                   jax.ShapeDtypeStruct((rows_p, LANES), F32)],
        scratch_shapes=[pltpu.VMEM((SUBLANES + ROW_BLOCK, CONV_DIM), F32)],
        compiler_params=pltpu.CompilerParams(dimension_semantics=("arbitrary",),
                                             vmem_limit_bytes=VMEM_LIMIT_BYTES),
        name="input_projection",
    )(meta_pad, x2d, rope_blk, rope_loc, *weights)


def _attn_kernel(q_ref, k_ref, v_ref, o_ref, sm_ref, pm_ref, s_ref, p_ref, m_ref, l_ref, acc_ref):
    i = pl.program_id(1)
    n_tiles = ATT_K // LANES
    assert ATT_Q == ATT_K, "one diagonal chunk per query block"
    assert ATT_UNROLL % 2 == 0 and PAD >= LANES

    def chunk_start(j):
        return pl.multiple_of(PAD + j * ATT_K, math.gcd(PAD, ATT_K))

    def scores(buf, j):
        s_ref[buf] = jnp.dot(q_ref[0], k_ref[0, :, pl.ds(chunk_start(j), ATT_K)], preferred_element_type=F32)

    def score_tiles(buf, last):
        tiles = [s_ref[buf, :, t * LANES:(t + 1) * LANES] for t in range(n_tiles)]
        if last:
            rows = lax.broadcasted_iota(jnp.int32, (ATT_Q, LANES), 0)
            cols = lax.broadcasted_iota(jnp.int32, (ATT_Q, LANES), 1)
            tiles = [jnp.where(cols + t * LANES <= rows, tiles[t], -jnp.inf) for t in range(n_tiles)]
            tiles = [jnp.where(cols >= LANES - N_META, sm_ref[...], -jnp.inf)] + tiles
        return tiles

    def consume(buf, j, last=False):
        tiles = score_tiles(buf, last)
        mx = tiles[0]
        for t in range(1, len(tiles)):
            mx = jnp.maximum(mx, tiles[t])
        m_old = m_ref[...]
        m_new = jnp.maximum(m_old, jnp.max(mx, axis=1, keepdims=True))
        m_ref[...] = m_new
        alpha = jnp.exp2(m_old - m_new)
        tiles = score_tiles(buf, last)
        probs = [jnp.exp2(t - m_new) for t in tiles]
        lsum = probs[0]
        for p in probs[1:]:
            lsum = lsum + p
        l_ref[...] = alpha * l_ref[...] + lsum
        pv = None
        if last:
            pm_ref[...] = probs.pop(0).astype(BF16)
            pv = jnp.dot(pm_ref[...], v_ref[0, PAD - LANES:PAD, :], preferred_element_type=F32)
        for t, p in enumerate(probs):
            p_ref[:, t * LANES:(t + 1) * LANES] = p.astype(BF16)
        pv_chunk = jnp.dot(p_ref[...], v_ref[0, pl.ds(chunk_start(j), ATT_K), :], preferred_element_type=F32)
        acc_ref[...] = alpha * acc_ref[...] + (pv_chunk if pv is None else pv + pv_chunk)

    def run(first, count):
        for u in range(count):
            scores((u + 1) % 2, first + u + 1)
            consume(u % 2, first + u)

    m_ref[...] = jnp.full((ATT_Q, LANES), -jnp.inf, F32)
    l_ref[...] = jnp.zeros((ATT_Q, LANES), F32)
    acc_ref[...] = jnp.zeros((ATT_Q, V_HEAD), F32)
    sm_ref[...] = jnp.dot(q_ref[0], k_ref[0, :, PAD - LANES:PAD], preferred_element_type=F32)
    scores(0, 0)

    def trip(t, carry):
        run(ATT_UNROLL * t, ATT_UNROLL)
        return carry

    lax.fori_loop(0, i // ATT_UNROLL, trip, 0)

    done = i - i % ATT_UNROLL
    for rem in range(1, ATT_UNROLL):
        @pl.when(i - done == rem)
        def _():
            run(done, rem)

    for parity in range(2):
        @pl.when(i % 2 == parity)
        def _():
            consume(parity, i, last=True)
            o_ref[...] = (acc_ref[...] / jnp.sum(l_ref[...], axis=1, keepdims=True)).astype(o_ref.dtype)


def _attention(q, k, v):
    heads, seq, _ = q.shape
    rows_p = v.shape[1]
    n_q = seq // ATT_Q
    return pl.pallas_call(
        _attn_kernel,
        grid=(heads, n_q),
        in_specs=[pl.BlockSpec((1, ATT_Q, QK_PAD), lambda h, i: (h, i, 0)),
                  pl.BlockSpec((1, QK_PAD, rows_p), lambda h, i: (h, 0, 0)),
                  pl.BlockSpec((1, rows_p, V_HEAD), lambda h, i: (h, 0, 0))],
        out_specs=pl.BlockSpec((ATT_Q, V_HEAD), lambda h, i: (i, h)),
        out_shape=jax.ShapeDtypeStruct((seq, ATT_WIDTH), BF16),
        scratch_shapes=[pltpu.VMEM((ATT_Q, LANES), F32),
                        pltpu.VMEM((ATT_Q, LANES), BF16),
                        pltpu.VMEM((2, ATT_Q, ATT_K), F32),
                        pltpu.VMEM((ATT_Q, ATT_K), BF16),
                        pltpu.VMEM((ATT_Q, LANES), F32),
                        pltpu.VMEM((ATT_Q, LANES), F32),
                        pltpu.VMEM((ATT_Q, V_HEAD), F32)],
        compiler_params=pltpu.CompilerParams(dimension_semantics=("arbitrary", "arbitrary"),
                                             vmem_limit_bytes=VMEM_LIMIT_BYTES + 6 * 1024 * 1024),
        name="causal_attention",
    )(q, k, v)


def _split_dot(v, mat, terms):
    out = None
    rem = v
    for t in range(terms):
        piece = rem.astype(BF16)
        part = jnp.dot(piece, mat, preferred_element_type=F32)
        out = part if out is None else out + part
        if t + 1 < terms:
            rem = rem - piece.astype(F32)
    return out


def _ssd_kernel(xc_ref, dt_ref, dtb_ref, alog_ref, dskip_ref, tril_ref, expand_ref, y_ref, state_ref):
    i = pl.program_id(0)
    rows_b = xc_ref.shape[0]

    @pl.when(i == 0)
    def _():
        state_ref[...] = jnp.zeros_like(state_ref)

    xc = xc_ref

    lane = lax.broadcasted_iota(jnp.int32, (SSD_CHUNK, LANES), 1)
    row = lax.broadcasted_iota(jnp.int32, (SSD_CHUNK, LANES), 0)
    causal = lax.broadcasted_iota(jnp.int32, (SSD_CHUNK, SSD_CHUNK), 0) >= lax.broadcasted_iota(
        jnp.int32, (SSD_CHUNK, SSD_CHUNK), 1)
    low_half = lane < SSM_HEAD_DIM
    a_row = -jnp.exp(alog_ref[...])
    tril = tril_ref[...]
    heads_per_group = SSM_HEADS // SSM_GROUPS

    for c in range(rows_b // SSD_CHUNK):
        r0 = c * SSD_CHUNK
        valid = (lane < SSM_HEADS) & (row + (i * rows_b + r0) >= PAD - N_META)
        dt = jnp.where(valid, jax.nn.softplus(dt_ref[r0:r0 + SSD_CHUNK, :] + dtb_ref[...]), 0.0)
        da = dt * a_row
        acum = _cumsum_rows(tril, da)
        acum_t = acum.T
        dt_t = dt.T
        a_last = acum[SSD_CHUNK - 1:SSD_CHUNK, :]
        factors = jnp.concatenate([jnp.exp(acum), jnp.exp(a_last - acum) * dt,
                                   jnp.broadcast_to(jnp.exp(a_last), (SUBLANES, LANES))], axis=0)
        rows = slice(r0, r0 + SSD_CHUNK)
        for g in range(SSM_GROUPS):
            spread = _split_dot(factors, expand_ref[:, g * GROUP_WIDTH:(g + 1) * GROUP_WIDTH], 2)
            e_acum = spread[:SSD_CHUNK, :]
            w_end = spread[SSD_CHUNK:2 * SSD_CHUNK, :]
            c_decay = spread[2 * SSD_CHUNK:2 * SSD_CHUNK + 1, :]
            b_g = xc[rows, SSM_WIDTH + g * SSM_STATE:SSM_WIDTH + (g + 1) * SSM_STATE]
            c_g = xc[rows, SSM_WIDTH + (SSM_GROUPS + g) * SSM_STATE:
                     SSM_WIDTH + (SSM_GROUPS + g + 1) * SSM_STATE].astype(BF16)
            b_t = b_g.T.astype(BF16)
            cb = jnp.dot(c_g, b_t, preferred_element_type=F32)
            gcols = slice(g * GROUP_WIDTH, (g + 1) * GROUP_WIDTH)
            h_prev = state_ref[g]
            y_off = jnp.dot(c_g, h_prev.astype(BF16), preferred_element_type=F32)
            for jp in range(heads_per_group // 2):
                ws = []
                for r in (g * heads_per_group + 2 * jp, g * heads_per_group + 2 * jp + 1):
                    seg = acum[:, r:r + 1] - acum_t[r:r + 1, :]
                    decay = jnp.where(causal, jnp.exp(seg), 0.0)
                    ws.append((cb * decay * dt_t[r:r + 1, :]).astype(BF16))
                cols = slice(g * GROUP_WIDTH + jp * LANES, g * GROUP_WIDTH + (jp + 1) * LANES)
                x_pair = xc[rows, cols]
                x_cat = jnp.concatenate([jnp.where(low_half, x_pair, 0.0), jnp.where(low_half, 0.0, x_pair)],
                                        axis=0).astype(BF16)
                y_diag = jnp.dot(jnp.concatenate(ws, axis=1), x_cat, preferred_element_type=F32)
                pair = slice(jp * LANES, (jp + 1) * LANES)
                y_ref[rows, cols] = y_diag + y_off[:, pair] * e_acum[:, pair] + x_pair * dskip_ref[:, cols]
            x_end = (xc[rows, gcols] * w_end).astype(BF16)
            state_ref[g] = c_decay * h_prev + jnp.dot(b_t, x_end, preferred_element_type=F32)


def _cumsum_rows(tril, v):
    out = None
    rem = v
    for t in range(3):
        piece = rem.astype(BF16)
        part = jnp.dot(tril, piece, preferred_element_type=F32)
        out = part if out is None else out + part
        if t < 2:
            rem = rem - piece.astype(F32)
    return out


def _ssd_scan(xc, dt_raw, dt_bias, a_log, d_skip_x, tril, expand, seq):
    n_blocks = seq // ROW_BLOCK + 1
    consts = (dt_bias, a_log, d_skip_x, tril, expand)
    return pl.pallas_call(
        _ssd_kernel,
        grid=(n_blocks,),
        in_specs=[pl.BlockSpec((ROW_BLOCK, CONV_DIM), lambda i: (i, 0)),
                  pl.BlockSpec((ROW_BLOCK, LANES), lambda i: (i, 0))] + [_const_spec(c.shape) for c in consts],
        out_specs=pl.BlockSpec((ROW_BLOCK, SSM_WIDTH), lambda i: (jnp.maximum(i - 1, 0), 0)),
        out_shape=jax.ShapeDtypeStruct((seq, SSM_WIDTH), F32),
        scratch_shapes=[pltpu.VMEM((SSM_GROUPS, SSM_STATE, GROUP_WIDTH), F32)],
        compiler_params=pltpu.CompilerParams(dimension_semantics=("arbitrary",),
                                             vmem_limit_bytes=VMEM_LIMIT_BYTES),
        name="ssd_scan",
    )(xc, dt_raw, *consts)


def _out_mlp_kernel(att_ref, y_ref, z_ref, x_ref, g_ssm_ref, wout_ref, g_post_ref, g_mpre_ref,
                    wup_ref, wdown_ref, g_mpost_ref, o_ref):
    z = z_ref[...]
    gated = y_ref[...] * (z * jax.nn.sigmoid(z))
    parts = []
    for g in range(SSM_GROUPS):
        gg = gated[:, g * GROUP_WIDTH:(g + 1) * GROUP_WIDTH]
        parts.append(gg * lax.rsqrt(jnp.mean(jnp.square(gg), axis=-1, keepdims=True) + EPS))
    ssm = (jnp.concatenate(parts, axis=1) * g_ssm_ref[...]).astype(BF16)
    mix = (jnp.dot(att_ref[...], wout_ref[:ATT_WIDTH, :], preferred_element_type=F32)
           + jnp.dot(ssm, wout_ref[ATT_WIDTH:, :], preferred_element_type=F32))
    h1 = x_ref[...] + _rms(mix, g_post_ref[...])
    hn = _rms(h1, g_mpre_ref[...]).astype(BF16)
    f = None
    for c in range(D_FF // FF_CHUNK):
        u = jnp.dot(hn, wup_ref[:, c * FF_CHUNK:(c + 1) * FF_CHUNK], preferred_element_type=F32)
        u = jnp.square(jnp.maximum(u, 0.0)).astype(BF16)
        part = jnp.dot(u, wdown_ref[c * FF_CHUNK:(c + 1) * FF_CHUNK, :], preferred_element_type=F32)
        f = part if f is None else f + part
    o_ref[...] = h1 + _rms(f, g_mpost_ref[...])


def _out_mlp(att, y, z, x2d, g_ssm, wout, g_post, g_mpre, wup, wdown, g_mpost):
    seq = x2d.shape[0]
    weights = (g_ssm, wout, g_post, g_mpre, wup, wdown, g_mpost)
    row = lambda i: (i, 0)
    return pl.pallas_call(
        _out_mlp_kernel,
        grid=(seq // MLP_BLOCK,),
        in_specs=[pl.BlockSpec((MLP_BLOCK, ATT_WIDTH), row),
                  pl.BlockSpec((MLP_BLOCK, SSM_WIDTH), row),
                  pl.BlockSpec((MLP_BLOCK, SSM_WIDTH), row),
                  pl.BlockSpec((MLP_BLOCK, D_MODEL), row)] + [_const_spec(w.shape) for w in weights],
        out_specs=pl.BlockSpec((MLP_BLOCK, D_MODEL), row),
        out_shape=jax.ShapeDtypeStruct((seq, D_MODEL), F32),
        compiler_params=pltpu.CompilerParams(dimension_semantics=("arbitrary",),
                                             vmem_limit_bytes=VMEM_LIMIT_BYTES),
        name="out_proj_mlp",
    )(att, y, z, x2d, *weights)


def _rot_cols(w):
    half = QK_ROPE // 2
    return jnp.concatenate([-w[..., half:], w[..., :half]], axis=-1)


def _lane_pad(v, width=LANES):
    return jnp.pad(v, [(0, 0)] * (v.ndim - 1) + [(0, width - v.shape[-1])])


def kernel(x, meta_tokens, norm_mix_pre, w_in, q_a_norm, w_q_up, kv_a_norm, w_kv_up, conv_w, conv_b, dt_bias,
           a_log, d_skip, ssm_norm, w_out, norm_mix_post, norm_mlp_pre, w_mlp_up, w_mlp_down, norm_mlp_post):
    bsz, seq, _ = x.shape
    assert bsz == 1 and norm_mix_pre.shape[0] == 1, "single sequence, single layer"
    assert seq % ATT_Q == 0 and seq % ROW_BLOCK == 0 and seq % MLP_BLOCK == 0
    assert PAD == ROW_BLOCK, "the front pad is exactly the first block of the projection / SSD grids"
    x2d = x[0]
    row = lambda v: v.reshape(1, -1)

    o_ckv, o_kr, o_z, o_xbc, o_dt = Q_LORA, Q_LORA + KV_LORA, Q_LORA + KV_LORA + QK_ROPE, \
        Q_LORA + KV_LORA + QK_ROPE + SSM_WIDTH, Q_LORA + KV_LORA + QK_ROPE + SSM_WIDTH + CONV_DIM
    w_in0 = w_in.astype(BF16)[0]
    w_kr = w_in0[:, o_kr:o_z]
    wsm = jnp.concatenate([w_kr, _rot_cols(w_kr), _lane_pad(w_in0[:, o_dt:])], axis=1)
    wq3 = w_q_up[0].reshape(Q_LORA, ATT_HEADS, QK_DIM)
    wq = jnp.concatenate([wq3, _rot_cols(wq3[..., QK_NOPE:])], axis=-1).reshape(Q_LORA, ATT_HEADS * QK_PAD)
    wkv3 = w_kv_up[0].reshape(KV_LORA, ATT_HEADS, QK_NOPE + V_HEAD)
    wkt = wkv3[..., :QK_NOPE].reshape(KV_LORA, ATT_HEADS * QK_NOPE).T
    wv = wkv3[..., QK_NOPE:].reshape(KV_LORA, ATT_HEADS * V_HEAD)

    inv_freq = ROPE_THETA ** (-jnp.arange(0, QK_ROPE, 2, dtype=F32) / QK_ROPE)
    freq = jnp.tile(inv_freq, 4)
    low = jnp.arange(LANES) < QK_ROPE
    base = (jnp.arange(seq // ROW_BLOCK + 1) * ROW_BLOCK - (PAD - N_META)).astype(F32)
    ang_b = base[:, None] * freq[None, :]
    rope_blk = jnp.stack([jnp.cos(ang_b), jnp.where(low, -jnp.sin(ang_b), jnp.sin(ang_b))], axis=1)
    rope_blk = jnp.pad(rope_blk, ((0, 0), (0, SUBLANES - 2), (0, 0)))
    ang_l = jnp.arange(ROW_BLOCK, dtype=F32)[:, None] * freq[None, :]
    rope_loc = jnp.stack([jnp.where(low, jnp.cos(ang_l), jnp.sin(ang_l)),
                          jnp.where(low, jnp.sin(ang_l), jnp.cos(ang_l))])

    meta_pad = jnp.concatenate([jnp.zeros((PAD - N_META, D_MODEL), F32), meta_tokens.astype(F32)], axis=0)

    q, k, v, z, xc, dt_raw = _input_projection(
        x2d, meta_pad, rope_blk, rope_loc, row(norm_mix_pre[0]),
        w_in0[:, :o_ckv].astype(BF16), w_in0[:, o_ckv:o_kr].astype(BF16), w_in0[:, o_z:o_xbc].astype(BF16),
        w_in0[:, o_xbc:o_dt].astype(BF16), wsm.astype(BF16),
        row(q_a_norm[0]), row(kv_a_norm[0]), wq.astype(BF16), wkt.astype(BF16), wv.astype(BF16),
        conv_w[0], row(conv_b[0]))

    att = _attention(q, k, v)

    tril = jnp.tril(jnp.ones((SSD_CHUNK, SSD_CHUNK), BF16))
    expand = (jnp.arange(LANES)[:, None] == jnp.arange(SSM_WIDTH)[None, :] // SSM_HEAD_DIM).astype(BF16)
    y = _ssd_scan(xc, dt_raw, _lane_pad(row(dt_bias[0])), _lane_pad(row(a_log[0])),
                  row(jnp.repeat(d_skip[0], SSM_HEAD_DIM)), tril, expand, seq)

    out = _out_mlp(att, y, z, x2d, row(ssm_norm[0]), w_out.astype(BF16), row(norm_mix_post[0]),
                   row(norm_mlp_pre[0]), w_mlp_up.astype(BF16), w_mlp_down.astype(BF16),
                   row(norm_mlp_post[0]))
    return out[None]
```
